```python
import jax, jax.numpy as jnp
from jax import lax
import numpy as np

D_MODEL = 1024
BATCH = 2
SEQ = 8192
DEPTH = 1
DEC_BATCH = 1
DEC_SEQ = 16384
PAST_LEN = 128

HEAD_DIM = 64
SSD_HEADS = 8
SSD_INNER = SSD_HEADS * HEAD_DIM
SSD_GROUPS = 2
SSD_STATE = 128
CONV_K = 5
CHUNK = 128
NA_HEADS = 8
NA_INNER = NA_HEADS * HEAD_DIM
NA_ROWS = 8
NA_COLS = 16
GRID_W = 64
MIX_WIDTH = SSD_INNER + NA_INNER
CONV_CH = SSD_INNER + 2 * SSD_GROUPS * SSD_STATE
IN_PROJ = SSD_INNER + CONV_CH + 2 * SSD_HEADS + 3 * NA_INNER
D_FF = ((8 * D_MODEL // 3 + 255) // 256) * 256
EPS = 1e-6

kernel_name = "hymba_ssd_natten2d_encoder"


def rmsnorm(x, g):
    xf = x.astype(jnp.float32)
    y = xf * lax.rsqrt(jnp.mean(xf * xf, axis=-1, keepdims=True) + EPS)
    return (y * g.astype(jnp.float32)).astype(x.dtype)


def depthwise_conv_centred(u, w, b):
    ch = u.shape[-1]
    y = lax.conv_general_dilated(u, w[:, None, :], window_strides=(1,),
                                 padding=[(CONV_K // 2, CONV_K // 2)],
                                 dimension_numbers=('NWC', 'WIO', 'NWC'),
                                 feature_group_count=ch)
    return y + b


def ssd_scan(x, dt, A, Bm, Cm):
    b, L, H, P = x.shape
    G, N = Bm.shape[2], Bm.shape[3]
    rep = H // G
    nc = L // CHUNK
    xc = x.reshape(b, nc, CHUNK, H, P)
    Bc = Bm.reshape(b, nc, CHUNK, G, N)
    Cc = Cm.reshape(b, nc, CHUNK, G, N)
    dtc = dt.reshape(b, nc, CHUNK, H)
    a_cum = jnp.cumsum(dtc * A, axis=2)
    seg = a_cum[:, :, :, None, :] - a_cum[:, :, None, :, :]
    mask = jnp.tril(jnp.ones((CHUNK, CHUNK), dtype=bool))
    decay = jnp.exp(jnp.where(mask[None, None, :, :, None], seg, -jnp.inf))
    cb = jnp.einsum('bcign,bcjgn->bcijg', Cc, Bc).astype(jnp.float32)
    cb = jnp.repeat(cb, rep, axis=-1)
    w = cb * decay * dtc[:, :, None, :, :]
    y_diag = jnp.einsum('bcijh,bcjhp->bcihp', w, xc)
    Bh = jnp.repeat(Bc, rep, axis=3)
    d_states = jnp.exp(a_cum[:, :, -1:, :] - a_cum) * dtc
    states = jnp.einsum('bcjhn,bcjh,bcjhp->bchpn', Bh, d_states, xc).astype(jnp.float32)
    chunk_decay = jnp.exp(a_cum[:, :, -1, :])

    def step(h, inp):
        s, dcy = inp
        return h * dcy[:, :, None, None] + s, h

    h0 = jnp.zeros((b, H, P, N), jnp.float32)
    _, h_in = lax.scan(step, h0, (jnp.moveaxis(states, 1, 0), jnp.moveaxis(chunk_decay, 1, 0)))
    h_in = jnp.moveaxis(h_in, 0, 1)
    Ch = jnp.repeat(Cc, rep, axis=3)
    y_off = jnp.einsum('bcihn,bchpn,bcih->bcihp', Ch, h_in, jnp.exp(a_cum))
    return (y_diag + y_off).reshape(b, L, H, P)


def neighbourhood_attention_2d(q, k, v, rpb):
    b, L, H, Dh = q.shape
    rows = L // GRID_W
    kr = min(NA_ROWS, rows)
    kc = NA_COLS
    qg = q.reshape(b, rows, GRID_W, H, Dh)
    kg = k.reshape(b, rows, GRID_W, H, Dh)
    vg = v.reshape(b, rows, GRID_W, H, Dh)
    cols = jnp.arange(GRID_W)
    cs = jnp.clip(cols - kc // 2, 0, GRID_W - kc)
    col_idx = cs[:, None] + jnp.arange(kc)[None, :]
    col_off = col_idx - cols[:, None] + (NA_COLS - 1)
    scale = Dh ** -0.5

    def one_row(r):
        rs = jnp.clip(r - kr // 2, 0, rows - kr)
        q_r = lax.dynamic_index_in_dim(qg, r, axis=1, keepdims=False)
        k_blk = lax.dynamic_slice_in_dim(kg, rs, kr, axis=1)
        v_blk = lax.dynamic_slice_in_dim(vg, rs, kr, axis=1)
        k_nb = k_blk[:, :, col_idx]
        v_nb = v_blk[:, :, col_idx]
        row_off = rs + jnp.arange(kr) - r + (NA_ROWS - 1)
        bias = rpb[:, row_off[:, None, None], col_off[None, :, :]]
        s = jnp.einsum('bwhd,biwjhd->bhwij', q_r, k_nb).astype(jnp.float32) * scale
        s = s + jnp.transpose(bias, (0, 2, 1, 3)).astype(jnp.float32)[None]
        p = jax.nn.softmax(s.reshape(b, H, GRID_W, kr * kc), axis=-1)
        p = p.reshape(b, H, GRID_W, kr, kc).astype(v.dtype)
        return jnp.einsum('bhwij,biwjhd->bwhd', p, v_nb)

    out = lax.map(one_row, jnp.arange(rows))
    return jnp.transpose(out, (1, 0, 2, 3, 4)).reshape(b, L, H * Dh)


def hybrid_layer(x, c, w_ada, b_ada, g_mix, w_in, conv_w, conv_b, dt_bias_fwd, dt_bias_bwd,
                 a_log_fwd, a_log_bwd, d_skip, g_ssd, g_q, g_k, rpb, w_out, g_ffn, w_gate, w_up, w_down):
    b, L, _ = x.shape
    mod = jax.nn.silu(c) @ w_ada + b_ada
    sh1, sc1, gt1, sh2, sc2, gt2 = jnp.split(mod[:, None, :], 6, axis=-1)

    h = rmsnorm(x, g_mix) * (1 + sc1) + sh1
    proj = h @ w_in
    o1 = SSD_INNER
    o2 = o1 + CONV_CH
    o3 = o2 + 2 * SSD_HEADS
    o4 = o3 + NA_INNER
    o5 = o4 + NA_INNER
    z = proj[..., :o1]
    xbc = proj[..., o1:o2]
    dt_raw = proj[..., o2:o3].astype(jnp.float32)
    q = proj[..., o3:o4]
    k = proj[..., o4:o5]
    v = proj[..., o5:]

    xbc = jax.nn.silu(depthwise_conv_centred(xbc, conv_w, conv_b))
    xs = xbc[..., :SSD_INNER].reshape(b, L, SSD_HEADS, HEAD_DIM)
    Bm = xbc[..., SSD_INNER:SSD_INNER + SSD_GROUPS * SSD_STATE].reshape(b, L, SSD_GROUPS, SSD_STATE)
    Cm = xbc[..., SSD_INNER + SSD_GROUPS * SSD_STATE:].reshape(b, L, SSD_GROUPS, SSD_STATE)
    dt_f = jax.nn.softplus(dt_raw[..., :SSD_HEADS] + dt_bias_fwd.astype(jnp.float32))
    dt_b = jax.nn.softplus(dt_raw[..., SSD_HEADS:] + dt_bias_bwd.astype(jnp.float32))
    A_f = -jnp.exp(a_log_fwd.astype(jnp.float32))
    A_b = -jnp.exp(a_log_bwd.astype(jnp.float32))
    flip = lambda a: jnp.flip(a, axis=1)
    y_f = ssd_scan(xs, dt_f, A_f, Bm, Cm)
    y_b = flip(ssd_scan(flip(xs), flip(dt_b), A_b, flip(Bm), flip(Cm)))
    y = (y_f + y_b + xs * d_skip[:, None]).astype(x.dtype).reshape(b, L, SSD_INNER)
    y_ssd = rmsnorm(y * jax.nn.silu(z), g_ssd)

    qn = rmsnorm(q.reshape(b, L, NA_HEADS, HEAD_DIM), g_q)
    kn = rmsnorm(k.reshape(b, L, NA_HEADS, HEAD_DIM), g_k)
    y_na = neighbourhood_attention_2d(qn, kn, v.reshape(b, L, NA_HEADS, HEAD_DIM), rpb)

    mix = jnp.concatenate([y_ssd, y_na], axis=-1) @ w_out
    x = x + gt1 * mix

    h2 = rmsnorm(x, g_ffn) * (1 + sc2) + sh2
    ff = (jax.nn.silu(h2 @ w_gate) * (h2 @ w_up)) @ w_down
    return x + gt2 * ff


def setup_inputs(seed: int = 0) -> dict:
    key = jax.random.key(seed)
    ks = jax.random.split(key, 24)
    f32 = jnp.float32
    nrm = lambda k, s, sc: jax.random.normal(k, s, f32) * sc

    def dt_bias(k):
        u = jax.random.uniform(k, (DEPTH, SSD_HEADS), f32)
        dt = jnp.exp(u * (np.log(0.1) - np.log(0.001)) + np.log(0.001))
        return dt + jnp.log(-jnp.expm1(-dt))

    return {
        "x_prompt": nrm(ks[0], (BATCH, SEQ, D_MODEL), 1.0),
        "x_sample": nrm(ks[1], (DEC_BATCH, DEC_SEQ, D_MODEL), 1.0),
        "c_prompt": nrm(ks[2], (BATCH, D_MODEL), 1.0),
        "c_sample": nrm(ks[3], (DEC_BATCH, D_MODEL), 1.0),
        "w_ada": nrm(ks[4], (DEPTH, D_MODEL, 6 * D_MODEL), 0.5 * D_MODEL ** -0.5),
        "b_ada": nrm(ks[5], (DEPTH, 6 * D_MODEL), 0.01),
        "g_mix": 1.0 + nrm(ks[6], (DEPTH, D_MODEL), 0.01),
        "w_in": nrm(ks[7], (DEPTH, D_MODEL, IN_PROJ), D_MODEL ** -0.5),
        "conv_w": nrm(ks[8], (DEPTH, CONV_K, CONV_CH), CONV_K ** -0.5),
        "conv_b": nrm(ks[9], (DEPTH, CONV_CH), 0.01),
        "dt_bias_fwd": dt_bias(ks[10]),
        "dt_bias_bwd": dt_bias(ks[11]),
        "a_log_fwd": jnp.log(jax.random.uniform(ks[12], (DEPTH, SSD_HEADS), f32, 1.0, 16.0)),
        "a_log_bwd": jnp.log(jax.random.uniform(ks[13], (DEPTH, SSD_HEADS), f32, 1.0, 16.0)),
        "d_skip": 1.0 + nrm(ks[14], (DEPTH, SSD_HEADS), 0.01),
        "g_ssd": 1.0 + nrm(ks[15], (DEPTH, SSD_INNER), 0.01),
        "g_q": 1.0 + nrm(ks[16], (DEPTH, HEAD_DIM), 0.01),
        "g_k": 1.0 + nrm(ks[17], (DEPTH, HEAD_DIM), 0.01),
        "rpb": nrm(ks[18], (DEPTH, NA_HEADS, 2 * NA_ROWS - 1, 2 * NA_COLS - 1), 0.02),
        "w_out": nrm(ks[19], (DEPTH, MIX_WIDTH, D_MODEL), MIX_WIDTH ** -0.5),
        "g_ffn": 1.0 + nrm(ks[20], (DEPTH, D_MODEL), 0.01),
        "w_gate": nrm(ks[21], (DEPTH, D_MODEL, D_FF), D_MODEL ** -0.5),
        "w_up": nrm(ks[22], (DEPTH, D_MODEL, D_FF), D_MODEL ** -0.5),
        "w_down": nrm(ks[23], (DEPTH, D_FF, D_MODEL), D_FF ** -0.5),
    }


def reference(x_prompt, x_sample, c_prompt, c_sample, w_ada, b_ada, g_mix, w_in, conv_w, conv_b,
              dt_bias_fwd, dt_bias_bwd, a_log_fwd, a_log_bwd, d_skip, g_ssd, g_q, g_k, rpb, w_out,
              g_ffn, w_gate, w_up, w_down):
    def trunk(x, c):
        for l in range(DEPTH):
            x = hybrid_layer(x, c, w_ada[l], b_ada[l], g_mix[l], w_in[l], conv_w[l], conv_b[l],
                             dt_bias_fwd[l], dt_bias_bwd[l], a_log_fwd[l], a_log_bwd[l], d_skip[l],
                             g_ssd[l], g_q[l], g_k[l], rpb[l], w_out[l], g_ffn[l], w_gate[l],
                             w_up[l], w_down[l])
        return x

    y_prompt = trunk(x_prompt, c_prompt)
    y_sample = trunk(x_sample, c_sample)
    return (y_prompt, y_sample)
```

```python
import functools
import math

import numpy as np
import jax
import jax.numpy as jnp
from jax import lax
from jax.experimental import pallas as pl
from jax.experimental.pallas import tpu as pltpu

D_MODEL = 1024
HEAD_DIM = 64
SSD_HEADS = 8
SSD_INNER = SSD_HEADS * HEAD_DIM
SSD_GROUPS = 2
SSD_STATE = 128
CONV_K = 5
CHUNK = 128
NA_HEADS = 8
NA_INNER = NA_HEADS * HEAD_DIM
NA_ROWS = 8
NA_COLS = 16
GRID_W = 64
CONV_CH = SSD_INNER + 2 * SSD_GROUPS * SSD_STATE
D_FF = 2816
EPS = 1e-6

LANES = 128
BF16_SUBLANES = 16
GROUP_W = (SSD_HEADS // SSD_GROUPS) * HEAD_DIM
HEAD_PAIRS = NA_HEADS // 2
LOG2E = math.log2(math.e)
NEG_BIG = -1e30

TOKEN_TILE = 512
SSD_TILE = 512
NA_WIN_ROWS = 10
NA_WIN = NA_WIN_ROWS * GRID_W
NA_PAIRS_PER_STEP = 8
FF_BLOCK = 256
VMEM_LIMIT = 56 * 1024 * 1024

f32 = jnp.float32
bf16 = jnp.bfloat16


def _dot(a, b):
    return jnp.dot(a, b, preferred_element_type=f32)


def _dot_nt(a, b):
    return lax.dot_general(a, b, (((1,), (1,)), ((), ())), preferred_element_type=f32)


def _split3(v):
    p1 = v.astype(bf16)
    r1 = v - p1.astype(f32)
    p2 = r1.astype(bf16)
    p3 = (r1 - p2.astype(f32)).astype(bf16)
    return p1, p2, p3


def _even_lane_mask():
    return lax.broadcasted_iota(jnp.int32, (1, LANES), 1) < HEAD_DIM


def _silu(v):
    return v * jax.nn.sigmoid(v)


def _mod_kernel(c_ref, w_ref, b_ref, o_ref):
    s = _silu(c_ref[...])
    o_ref[...] = jnp.dot(s, w_ref[...], precision=lax.Precision.HIGHEST,
                         preferred_element_type=f32) + b_ref[...]


def _modulation(c_all, w_ada, b_ada):
    n = 6 * D_MODEL
    return pl.pallas_call(
        _mod_kernel,
        grid=(n // D_MODEL,),
        in_specs=[pl.BlockSpec((8, D_MODEL), lambda j: (0, 0)),
                  pl.BlockSpec((D_MODEL, D_MODEL), lambda j: (0, j)),
                  pl.BlockSpec((1, D_MODEL), lambda j: (0, j))],
        out_specs=pl.BlockSpec((8, D_MODEL), lambda j: (0, j)),
        out_shape=jax.ShapeDtypeStruct((8, n), f32),
        name="mod",
    )(c_all, w_ada, b_ada.reshape(1, n))


def _head_rmsnorm(v, gain):
    me = _even_lane_mask()
    s = v * v
    se = jnp.sum(jnp.where(me, s, 0.0), axis=-1, keepdims=True)
    so = jnp.sum(jnp.where(me, 0.0, s), axis=-1, keepdims=True)
    ms = jnp.where(me, se, so) * (1.0 / HEAD_DIM)
    return v * lax.rsqrt(ms + EPS) * gain


def _in_kernel(x_ref, mod_ref, g_ref, w_ref, wdt_ref, wvt_ref, gq_ref, gk_ref, dtb_ref, alog_ref,
               z_ref, xbc_ref, dtp_ref, q_ref, k_ref, vt_ref):
    x = x_ref[0]
    ms = jnp.mean(x * x, axis=-1, keepdims=True)
    y = x * lax.rsqrt(ms + EPS) * g_ref[...]
    h = (y * (1.0 + mod_ref[0, 1:2, :]) + mod_ref[0, 0:1, :]).astype(bf16)

    z_ref[0] = _dot(h, w_ref[:, 0:SSD_INNER]).astype(bf16)
    o = SSD_INNER
    xbc_ref[0] = _dot(h, w_ref[:, o:o + CONV_CH]).astype(bf16)
    o += CONV_CH
    for j in range(HEAD_PAIRS):
        sl = slice(j * LANES, (j + 1) * LANES)
        qj = _dot(h, w_ref[:, o + j * LANES:o + (j + 1) * LANES])
        q_ref[0, :, sl] = _head_rmsnorm(qj, gq_ref[:, sl]).astype(bf16)
    o += NA_INNER
    for j in range(HEAD_PAIRS):
        sl = slice(j * LANES, (j + 1) * LANES)
        kj = _dot(h, w_ref[:, o + j * LANES:o + (j + 1) * LANES])
        k_ref[0, :, sl] = _head_rmsnorm(kj, gk_ref[:, sl]).astype(bf16)

    vt = _dot_nt(wvt_ref[...], h).astype(bf16)
    for p in range(HEAD_PAIRS):
        for tb in range(TOKEN_TILE // LANES):
            vt_ref[0, p, tb] = vt[p * LANES:(p + 1) * LANES, tb * LANES:(tb + 1) * LANES]

    raw = _dot(h, wdt_ref[...]) + dtb_ref[...]
    dt = jnp.maximum(raw, 0.0) + jnp.log1p(jnp.exp(-jnp.abs(raw)))
    lane = lax.broadcasted_iota(jnp.int32, (1, LANES), 1)
    a_neg = -jnp.exp(alog_ref[...])
    mul = jnp.where(lane < 2 * SSD_HEADS, a_neg,
                    jnp.where((lane >= 64) & (lane < 64 + 2 * SSD_HEADS), 1.0, 0.0))
    dtp_ref[0] = dt * mul


def _in_proj(x, mod, g_mix, w_main, w_dt, w_vt, gq, gk, dt_bias, a_log):
    B, L, _ = x.shape
    T = TOKEN_TILE
    nw = w_main.shape[1]
    const = lambda b, i: (0, 0)
    tok = lambda b, i: (b, i, 0)
    return pl.pallas_call(
        _in_kernel,
        grid=(B, L // T),
        in_specs=[pl.BlockSpec((1, T, D_MODEL), tok),
                  pl.BlockSpec((1, 8, D_MODEL), lambda b, i: (b, 0, 0)),
                  pl.BlockSpec((1, D_MODEL), const),
                  pl.BlockSpec((D_MODEL, nw), const),
                  pl.BlockSpec((D_MODEL, LANES), const),
                  pl.BlockSpec((NA_INNER, D_MODEL), const),
                  pl.BlockSpec((1, NA_INNER), const),
                  pl.BlockSpec((1, NA_INNER), const),
                  pl.BlockSpec((1, LANES), const),
                  pl.BlockSpec((1, LANES), const)],
        out_specs=[pl.BlockSpec((1, T, SSD_INNER), tok),
                   pl.BlockSpec((1, T, CONV_CH), tok),
                   pl.BlockSpec((1, T, LANES), tok),
                   pl.BlockSpec((1, T, NA_INNER), tok),
                   pl.BlockSpec((1, T, NA_INNER), tok),
                   pl.BlockSpec((1, HEAD_PAIRS, T // LANES, LANES, LANES), lambda b, i: (b, 0, i, 0, 0))],
        out_shape=[jax.ShapeDtypeStruct((B, L, SSD_INNER), bf16),
                   jax.ShapeDtypeStruct((B, L, CONV_CH), bf16),
                   jax.ShapeDtypeStruct((B, L, LANES), f32),
                   jax.ShapeDtypeStruct((B, L, NA_INNER), bf16),
                   jax.ShapeDtypeStruct((B, L, NA_INNER), bf16),
                   jax.ShapeDtypeStruct((B, HEAD_PAIRS, L // LANES, LANES, LANES), bf16)],
        compiler_params=pltpu.CompilerParams(dimension_semantics=("arbitrary", "arbitrary"),
                                             vmem_limit_bytes=VMEM_LIMIT),
        name="in_proj",
    )(x, mod, g_mix, w_main, w_dt, w_vt, gq, gk, dt_bias, a_log)


def _expand_pair(v, l0, l1, rows):
    a = jnp.broadcast_to(v[:, l0:l0 + 1], (rows, LANES))
    b = jnp.broadcast_to(v[:, l1:l1 + 1], (rows, LANES))
    return jnp.where(_even_lane_mask(), a, b)


def _expand_group(v, base, g, rows):
    h0 = base + 4 * g
    return jnp.concatenate([_expand_pair(v, h0, h0 + 1, rows), _expand_pair(v, h0 + 2, h0 + 3, rows)], axis=1)


def _chunk_decays(p):
    row = lax.broadcasted_iota(jnp.int32, (CHUNK, CHUNK), 0)
    col = lax.broadcasted_iota(jnp.int32, (CHUNK, CHUNK), 1)
    tri = (col <= row).astype(bf16)
    p1, p2, p3 = _split3(p)
    incl = _dot(tri, p1) + _dot(tri, p2) + _dot(tri, p3)
    excl = incl - p
    dts = pltpu.roll(p, 64, 1)
    tot = incl[CHUNK - 1:CHUNK, :]
    is_fwd = lax.broadcasted_iota(jnp.int32, (1, LANES), 1) < SSD_HEADS
    w = jnp.exp(jnp.where(is_fwd, tot - incl, excl)) * dts
    sc = jnp.exp(jnp.where(is_fwd, incl, tot - excl))
    return incl, excl, dts, w, sc, jnp.exp(tot)


def _conv_silu_tile(win_ref, cw_ref, cb_ref, out_ref):
    for c in range(SSD_TILE // CHUNK):
        for cb in range(CONV_CH // 256):
            cols = slice(cb * 256, (cb + 1) * 256)
            acc = jnp.broadcast_to(cb_ref[:, cols], (CHUNK, 256))
            for k in range(CONV_K):
                start = BF16_SUBLANES - CONV_K // 2 + k + c * CHUNK
                acc = acc + win_ref[start:start + CHUNK, cols] * cw_ref[k:k + 1, cols]
            out_ref[0, c * CHUNK:(c + 1) * CHUNK, cols] = _silu(acc).astype(bf16)


def _state_step(xo_ref, rows, st_ref, w, sc, dec, base):
    outs = []
    for g in range(SSD_GROUPS):
        xs_g = xo_ref[0, rows, g * GROUP_W:(g + 1) * GROUP_W].astype(f32)
        b_g = xo_ref[0, rows, SSD_INNER + g * SSD_STATE:SSD_INNER + (g + 1) * SSD_STATE].astype(f32)
        c_g = xo_ref[0, rows, SSD_INNER + (SSD_GROUPS + g) * SSD_STATE:
                     SSD_INNER + (SSD_GROUPS + g + 1) * SSD_STATE]
        s_g = st_ref[g]
        y_off = _dot(c_g, s_g.astype(bf16)) * _expand_group(sc, base, g, CHUNK)
        xw = (xs_g * _expand_group(w, base, g, CHUNK)).astype(bf16)
        st_ref[g] = s_g * _expand_group(dec, base, g, 1) + _dot(b_g.T.astype(bf16), xw)
        outs.append(y_off)
    return jnp.concatenate(outs, axis=1)


def _ssd_bwd_kernel(xc_ref, xp_ref, xn_ref, dtp_ref, cw_ref, cb_ref, xo_ref, yb_ref, win_ref, st_ref, *, n_tiles):
    i = pl.program_id(1)
    t = n_tiles - 1 - i

    @pl.when(i == 0)
    def _():
        st_ref[...] = jnp.zeros_like(st_ref)

    H = BF16_SUBLANES
    win_ref[0:H, :] = jnp.where(t > 0, xp_ref[0].astype(f32), 0.0)
    win_ref[H:H + SSD_TILE, :] = xc_ref[0].astype(f32)
    win_ref[H + SSD_TILE:2 * H + SSD_TILE, :] = jnp.where(t < n_tiles - 1, xn_ref[0].astype(f32), 0.0)
    _conv_silu_tile(win_ref, cw_ref, cb_ref, xo_ref)

    def body(j, carry):
        c = SSD_TILE // CHUNK - 1 - j
        rows = pl.ds(pl.multiple_of(c * CHUNK, CHUNK), CHUNK)
        _, _, _, w, sc, dec = _chunk_decays(dtp_ref[0, rows, :])
        yb_ref[0, rows, :] = _state_step(xo_ref, rows, st_ref, w, sc, dec, SSD_HEADS).astype(bf16)
        return carry

    lax.fori_loop(0, SSD_TILE // CHUNK, body, 0)


def _ssd_bwd(xbc, dtp, conv_w, conv_b):
    B, L, _ = xbc.shape
    T = SSD_TILE
    nT = L // T
    hb = T // BF16_SUBLANES
    cur = lambda b, i: (b, nT - 1 - i, 0)
    prev = lambda b, i: (b, jnp.maximum((nT - 1 - i) * hb - 1, 0), 0)
    nxt = lambda b, i: (b, jnp.minimum((nT - i) * hb, L // BF16_SUBLANES - 1), 0)
    const = lambda b, i: (0, 0)
    return pl.pallas_call(
        functools.partial(_ssd_bwd_kernel, n_tiles=nT),
        grid=(B, nT),
        in_specs=[pl.BlockSpec((1, T, CONV_CH), cur),
                  pl.BlockSpec((1, BF16_SUBLANES, CONV_CH), prev),
                  pl.BlockSpec((1, BF16_SUBLANES, CONV_CH), nxt),
                  pl.BlockSpec((1, T, LANES), cur),
                  pl.BlockSpec((CONV_K, CONV_CH), const),
                  pl.BlockSpec((1, CONV_CH), const)],
        out_specs=[pl.BlockSpec((1, T, CONV_CH), cur),
                   pl.BlockSpec((1, T, SSD_INNER), cur)],
        out_shape=[jax.ShapeDtypeStruct((B, L, CONV_CH), bf16),
                   jax.ShapeDtypeStruct((B, L, SSD_INNER), bf16)],
        scratch_shapes=[pltpu.VMEM((T + 2 * BF16_SUBLANES, CONV_CH), f32),
                        pltpu.VMEM((SSD_GROUPS, SSD_STATE, GROUP_W), f32)],
        compiler_params=pltpu.CompilerParams(dimension_semantics=("arbitrary", "arbitrary"),
                                             vmem_limit_bytes=VMEM_LIMIT),
        name="ssd_bwd",
    )(xbc, xbc, xbc, dtp, conv_w, conv_b)


def _ssd_fwd_kernel(xo_ref, dtp_ref, yb_ref, z_ref, dsk_ref, gs_ref, o_ref, st_ref):
    @pl.when(pl.program_id(1) == 0)
    def _():
        st_ref[...] = jnp.zeros_like(st_ref)

    row = lax.broadcasted_iota(jnp.int32, (CHUNK, CHUNK), 0)
    col = lax.broadcasted_iota(jnp.int32, (CHUNK, CHUNK), 1)
    low = col < row
    diag = col == row
    me = _even_lane_mask()

    def body(c, carry):
        rows = pl.ds(pl.multiple_of(c * CHUNK, CHUNK), CHUNK)
        p = dtp_ref[0, rows, :]
        incl, excl, dts, w, sc, dec = _chunk_decays(p)
        y = _state_step(xo_ref, rows, st_ref, w, sc, dec, 0)

        incl_t = incl.T
        excl_t = excl.T
        dts_t = dts.T
        y_diag = []
        for g in range(SSD_GROUPS):
            b_g = xo_ref[0, rows, SSD_INNER + g * SSD_STATE:SSD_INNER + (g + 1) * SSD_STATE].astype(f32)
            c_g = xo_ref[0, rows, SSD_INNER + (SSD_GROUPS + g) * SSD_STATE:
                         SSD_INNER + (SSD_GROUPS + g + 1) * SSD_STATE]
            cb = _dot(c_g, b_g.T.astype(bf16))
            for pr in range(2):
                xs_pair = xo_ref[0, rows, g * GROUP_W + pr * LANES:g * GROUP_W + (pr + 1) * LANES]
                acc = None
                for e in range(2):
                    hd = 4 * g + 2 * pr + e
                    af_c = jnp.broadcast_to(incl[:, hd:hd + 1], (CHUNK, CHUNK))
                    eb_c = jnp.broadcast_to(excl[:, SSD_HEADS + hd:SSD_HEADS + hd + 1], (CHUNK, CHUNK))
                    af_r = incl_t[hd:hd + 1, :]
                    eb_r = excl_t[SSD_HEADS + hd:SSD_HEADS + hd + 1, :]
                    df_r = dts_t[hd:hd + 1, :]
                    db_r = dts_t[SSD_HEADS + hd:SSD_HEADS + hd + 1, :]
                    arg = jnp.where(low, af_c - af_r, eb_r - eb_c)
                    coef = jnp.where(low, df_r, jnp.where(diag, df_r + db_r, db_r))
                    wm = (cb * jnp.exp(arg) * coef).astype(bf16)
                    xm = jnp.where(me, xs_pair, jnp.zeros_like(xs_pair)) if e == 0 else \
                        jnp.where(me, jnp.zeros_like(xs_pair), xs_pair)
                    d = _dot(wm, xm)
                    acc = d if acc is None else acc + d
                y_diag.append(acc)
        xs = xo_ref[0, rows, 0:SSD_INNER].astype(f32)
        y = y + jnp.concatenate(y_diag, axis=1) + yb_ref[0, rows, :].astype(f32) + xs * dsk_ref[...]
        gated = y * _silu(z_ref[0, rows, :].astype(f32))
        ms = jnp.mean(gated * gated, axis=-1, keepdims=True)
        o_ref[0, rows, :] = (gated * lax.rsqrt(ms + EPS) * gs_ref[...]).astype(bf16)
        return carry

    lax.fori_loop(0, SSD_TILE // CHUNK, body, 0)


def _ssd_fwd(xo, dtp, yb, z, d_skip_x, g_ssd):
    B, L, _ = xo.shape
    T = SSD_TILE
    tok = lambda b, i: (b, i, 0)
    const = lambda b, i: (0, 0)
    return pl.pallas_call(
        _ssd_fwd_kernel,
        grid=(B, L // T),
        in_specs=[pl.BlockSpec((1, T, CONV_CH), tok),
                  pl.BlockSpec((1, T, LANES), tok),
                  pl.BlockSpec((1, T, SSD_INNER), tok),
                  pl.BlockSpec((1, T, SSD_INNER), tok),
                  pl.BlockSpec((1, SSD_INNER), const),
                  pl.BlockSpec((1, SSD_INNER), const)],
        out_specs=pl.BlockSpec((1, T, SSD_INNER), tok),
        out_shape=jax.ShapeDtypeStruct((B, L, SSD_INNER), bf16),
        scratch_shapes=[pltpu.VMEM((SSD_GROUPS, SSD_STATE, GROUP_W), f32)],
        compiler_params=pltpu.CompilerParams(dimension_semantics=("arbitrary", "arbitrary"),
                                             vmem_limit_bytes=VMEM_LIMIT),
        name="ssd_fwd",
    )(xo, dtp, yb, z, d_skip_x, g_ssd)


def _na_bias_tables(rpb):
    variants = [(0, 0, 0), (2, 0, 0), (4, 0, 1), (6, 2, 2), (8, 2, 2)]
    kr = np.arange(NA_WIN)[:, None] // GRID_W
    kc = np.arange(NA_WIN)[:, None] % GRID_W
    par = np.arange(LANES)[None, :] // GRID_W
    w = np.arange(LANES)[None, :] % GRID_W
    cs = np.clip(w - NA_COLS // 2, 0, GRID_W - NA_COLS)
    col_ok = (kc >= cs) & (kc < cs + NA_COLS)
    ci = np.clip(kc - w + NA_COLS - 1, 0, 2 * NA_COLS - 2)
    ri_all, ok_all = [], []
    for roff, st0, st1 in variants:
        st = np.where(par == 0, st0, st1)
        row_ok = (kr >= st) & (kr < st + NA_ROWS)
        ri_all.append(np.clip(kr - (roff + par) + NA_ROWS - 1, 0, 2 * NA_ROWS - 2))
        ok_all.append(row_ok & col_ok)
    ri = np.stack(ri_all)
    ok = np.stack(ok_all)
    ci = np.broadcast_to(ci, ri.shape)
    vals = rpb.astype(f32)[:, ri, ci] * LOG2E
    tab = jnp.where(ok[None], vals, NEG_BIG)
    tab = tab.reshape(HEAD_PAIRS, 2, len(variants), NA_WIN, LANES)
    return jnp.transpose(tab, (2, 0, 3, 1, 4)).reshape(len(variants), HEAD_PAIRS, NA_WIN, 2 * LANES)


def _na_kernel(q_ref, k_ref, vt_ref, tab_ref, o_ref, *, grid_rows):
    step = pl.program_id(2)
    me = _even_lane_mask()

    def body(i, carry):
        r = 2 * (step * NA_PAIRS_PER_STEP + i)
        u0 = jnp.clip(r - NA_ROWS // 2, 0, grid_rows - NA_WIN_ROWS)
        var = jnp.where(r < 4, r // 2, jnp.where(r >= grid_rows - 4, 3 + (r - (grid_rows - 4)) // 2, 2))
        qrows = pl.ds(pl.multiple_of(i * LANES, LANES), LANES)
        q2 = q_ref[0, qrows, :]
        zero = jnp.zeros_like(q2)
        qm = jnp.concatenate([jnp.where(me, q2, zero), jnp.where(me, zero, q2)], axis=0)
        kk = k_ref[0, pl.ds(pl.multiple_of(u0 * GRID_W, LANES), NA_WIN), :]
        s = _dot_nt(kk, qm) + tab_ref[var, 0]
        m = jnp.max(s, axis=0, keepdims=True)
        p = jnp.exp2(s - m)
        l = jnp.sum(p, axis=0, keepdims=True)
        j0 = u0 // 2
        vtw = jnp.concatenate([vt_ref[0, 0, j0 + j] for j in range(NA_WIN // LANES)], axis=1)
        ot = _dot(vtw, p.astype(bf16)) * (1.0 / l)
        oc = jnp.concatenate([ot[0:HEAD_DIM, 0:LANES], ot[HEAD_DIM:LANES, LANES:2 * LANES]], axis=0)
        o_ref[0, qrows, :] = oc.T.astype(bf16)
        return carry

    lax.fori_loop(0, NA_PAIRS_PER_STEP, body, 0)


def _na(q, k, vt, tab):
    B, L, _ = q.shape
    grid_rows = L // GRID_W
    QT = NA_PAIRS_PER_STEP * LANES
    nv = tab.shape[0]
    return pl.pallas_call(
        functools.partial(_na_kernel, grid_rows=grid_rows),
        grid=(B, HEAD_PAIRS, L // QT),
        in_specs=[pl.BlockSpec((1, QT, LANES), lambda b, p, s: (b, s, p)),
                  pl.BlockSpec((1, L, LANES), lambda b, p, s: (b, 0, p)),
                  pl.BlockSpec((1, 1, L // LANES, LANES, LANES), lambda b, p, s: (b, p, 0, 0, 0)),
                  pl.BlockSpec((nv, 1, NA_WIN, 2 * LANES), lambda b, p, s: (0, p, 0, 0))],
        out_specs=pl.BlockSpec((1, QT, LANES), lambda b, p, s: (b, s, p)),
        out_shape=jax.ShapeDtypeStruct((B, L, NA_INNER), bf16),
        compiler_params=pltpu.CompilerParams(dimension_semantics=("arbitrary", "arbitrary", "arbitrary"),
                                             vmem_limit_bytes=VMEM_LIMIT),
        name="na",
    )(q, k, vt, tab)


def _out_kernel(x_ref, ys_ref, yn_ref, mod_ref, wo_ref, gf_ref, wg_ref, wu_ref, wd_ref, o_ref, a_ref):
    mix = _dot(ys_ref[0], wo_ref[0:SSD_INNER, :]) + _dot(yn_ref[0], wo_ref[SSD_INNER:, :])
    x1 = x_ref[0] + mod_ref[0, 2:3, :] * mix
    ms = jnp.mean(x1 * x1, axis=-1, keepdims=True)
    h2 = x1 * lax.rsqrt(ms + EPS) * gf_ref[...]
    h2 = (h2 * (1.0 + mod_ref[0, 4:5, :]) + mod_ref[0, 3:4, :]).astype(bf16)
    for c in range(D_FF // FF_BLOCK):
        cols = slice(c * FF_BLOCK, (c + 1) * FF_BLOCK)
        gate = _dot(h2, wg_ref[:, cols])
        up = _dot(h2, wu_ref[:, cols])
        a_ref[:, cols] = (_silu(gate) * up).astype(bf16)
    ff = _dot(a_ref[...], wd_ref[...])
    o_ref[0] = x1 + mod_ref[0, 5:6, :] * ff


def _out_ffn(x, y_ssd, y_na, mod, w_out, g_ffn, w_gate, w_up, w_down):
    B, L, _ = x.shape
    T = TOKEN_TILE
    tok = lambda b, i: (b, i, 0)
    const = lambda b, i: (0, 0)
    single = pl.Buffered(1)
    return pl.pallas_call(
        _out_kernel,
        grid=(B, L // T),
        in_specs=[pl.BlockSpec((1, T, D_MODEL), tok),
                  pl.BlockSpec((1, T, SSD_INNER), tok),
                  pl.BlockSpec((1, T, NA_INNER), tok),
                  pl.BlockSpec((1, 8, D_MODEL), lambda b, i: (b, 0, 0)),
                  pl.BlockSpec((D_MODEL, D_MODEL), const, pipeline_mode=single),
                  pl.BlockSpec((1, D_MODEL), const),
                  pl.BlockSpec((D_MODEL, D_FF), const, pipeline_mode=single),
                  pl.BlockSpec((D_MODEL, D_FF), const, pipeline_mode=single),
                  pl.BlockSpec((D_FF, D_MODEL), const, pipeline_mode=single)],
        out_specs=pl.BlockSpec((1, T, D_MODEL), tok),
        out_shape=jax.ShapeDtypeStruct((B, L, D_MODEL), f32),
        scratch_shapes=[pltpu.VMEM((T, D_FF), bf16)],
        compiler_params=pltpu.CompilerParams(dimension_semantics=("arbitrary", "arbitrary"),
                                             vmem_limit_bytes=VMEM_LIMIT),
        name="out_ffn",
    )(x, y_ssd, y_na, mod, w_out, g_ffn, w_gate, w_up, w_down)


def _pad_lanes(parts):
    row = jnp.zeros((LANES,), f32)
    for off, v in parts:
        row = row.at[off:off + v.shape[0]].set(v.astype(f32))
    return row.reshape(1, LANES)


def kernel(x_prompt, x_sample, c_prompt, c_sample, w_ada, b_ada, g_mix, w_in, conv_w, conv_b, dt_bias_fwd, dt_bias_bwd, a_log_fwd, a_log_bwd, d_skip, g_ssd, g_q, g_k, rpb, w_out, g_ffn, w_gate, w_up, w_down):
    l = 0
    nb_p = x_prompt.shape[0]
    nb_s = x_sample.shape[0]

    c_all = jnp.zeros((8, D_MODEL), f32).at[:nb_p].set(c_prompt).at[nb_p:nb_p + nb_s].set(c_sample)
    mod = _modulation(c_all, w_ada[l], b_ada[l]).reshape(8, 6, D_MODEL)
    mod = jnp.pad(mod, ((0, 0), (0, 2), (0, 0)))

    o1 = SSD_INNER
    o2 = o1 + CONV_CH
    o3 = o2 + 2 * SSD_HEADS
    o4 = o3 + NA_INNER
    o5 = o4 + NA_INNER
    wi = w_in[l]
    w_main = jnp.concatenate([wi[:, :o2], wi[:, o3:o5]], axis=1).astype(bf16)
    w_dt16 = wi[:, o2:o3]
    w_dt = jnp.zeros((D_MODEL, LANES), f32).at[:, 0:16].set(w_dt16).at[:, 64:80].set(w_dt16).astype(bf16)
    w_vt = wi[:, o5:].T.astype(bf16)
    dtb = _pad_lanes([(0, dt_bias_fwd[l]), (8, dt_bias_bwd[l]), (64, dt_bias_fwd[l]), (72, dt_bias_bwd[l])])
    alog = _pad_lanes([(0, a_log_fwd[l]), (8, a_log_bwd[l])])
    gq = jnp.tile(g_q[l].astype(f32) * (HEAD_DIM ** -0.5 * LOG2E), NA_HEADS).reshape(1, NA_INNER)
    gk = jnp.tile(g_k[l].astype(f32), NA_HEADS).reshape(1, NA_INNER)
    dsk = jnp.repeat(d_skip[l].astype(f32), HEAD_DIM).reshape(1, SSD_INNER)
    tab = _na_bias_tables(rpb[l])
    wo = w_out[l].astype(bf16)
    wg = w_gate[l].astype(bf16)
    wu = w_up[l].astype(bf16)
    wd = w_down[l].astype(bf16)
    row = lambda v: v.astype(f32).reshape(1, -1)

    def trunk(x, m):
        z, xbc, dtp, q, k, vt = _in_proj(x, m, row(g_mix[l]), w_main, w_dt, w_vt, gq, gk, dtb, alog)
        xo, yb = _ssd_bwd(xbc, dtp, conv_w[l].astype(f32), row(conv_b[l]))
        y_ssd = _ssd_fwd(xo, dtp, yb, z, dsk, row(g_ssd[l]))
        y_na = _na(q, k, vt, tab)
        return _out_ffn(x, y_ssd, y_na, m, wo, row(g_ffn[l]), wg, wu, wd)

    return (trunk(x_prompt, mod[:nb_p]), trunk(x_sample, mod[nb_p:nb_p + nb_s]))
```

```python
import functools
import math

import numpy as np
import jax
import jax.numpy as jnp
from jax import lax
from jax.experimental import pallas as pl
from jax.experimental.pallas import tpu as pltpu

D_MODEL = 1024
HEAD_DIM = 64
SSD_HEADS = 8
SSD_INNER = SSD_HEADS * HEAD_DIM
SSD_GROUPS = 2
SSD_STATE = 128
CONV_K = 5
CHUNK = 128
NA_HEADS = 8
NA_INNER = NA_HEADS * HEAD_DIM
NA_ROWS = 8
NA_COLS = 16
GRID_W = 64
CONV_CH = SSD_INNER + 2 * SSD_GROUPS * SSD_STATE
D_FF = 2816
EPS = 1e-6

LANES = 128
BF16_SUBLANES = 16
GROUP_W = (SSD_HEADS // SSD_GROUPS) * HEAD_DIM
HEAD_PAIRS = NA_HEADS // 2
LOG2E = math.log2(math.e)
NEG_BIG = -1e30

TOKEN_TILE = 512
SSD_TILE = 512
SSD_UNROLL = 2
NA_WIN_ROWS = 10
NA_WIN = NA_WIN_ROWS * GRID_W
NA_PAIRS_PER_STEP = 8
NA_UNROLL = 4
FF_BLOCK = 256
VMEM_LIMIT = 56 * 1024 * 1024

f32 = jnp.float32
bf16 = jnp.bfloat16


def _dot(a, b):
    return jnp.dot(a, b, preferred_element_type=f32)


def _dot_nt(a, b):
    return lax.dot_general(a, b, (((1,), (1,)), ((), ())), preferred_element_type=f32)


def _split3(v):
    p1 = v.astype(bf16)
    r1 = v - p1.astype(f32)
    p2 = r1.astype(bf16)
    p3 = (r1 - p2.astype(f32)).astype(bf16)
    return p1, p2, p3


def _even_lane_mask():
    return lax.broadcasted_iota(jnp.int32, (1, LANES), 1) < HEAD_DIM


def _silu(v):
    return v * jax.nn.sigmoid(v)


def _mod_kernel(c_ref, w_ref, b_ref, o_ref):
    s = _silu(c_ref[...])
    o_ref[...] = jnp.dot(s, w_ref[...], precision=lax.Precision.HIGHEST,
                         preferred_element_type=f32) + b_ref[...]


def _modulation(c_all, w_ada, b_ada):
    n = 6 * D_MODEL
    return pl.pallas_call(
        _mod_kernel,
        grid=(n // D_MODEL,),
        in_specs=[pl.BlockSpec((8, D_MODEL), lambda j: (0, 0)),
                  pl.BlockSpec((D_MODEL, D_MODEL), lambda j: (0, j)),
                  pl.BlockSpec((1, D_MODEL), lambda j: (0, j))],
        out_specs=pl.BlockSpec((8, D_MODEL), lambda j: (0, j)),
        out_shape=jax.ShapeDtypeStruct((8, n), f32),
        name="mod",
    )(c_all, w_ada, b_ada.reshape(1, n))


def _head_rmsnorm(v, gain):
    me = _even_lane_mask()
    s = v * v
    se = jnp.sum(jnp.where(me, s, 0.0), axis=-1, keepdims=True)
    so = jnp.sum(jnp.where(me, 0.0, s), axis=-1, keepdims=True)
    ms = jnp.where(me, se, so) * (1.0 / HEAD_DIM)
    return v * lax.rsqrt(ms + EPS) * gain


def _in_kernel(x_ref, mod_ref, g_ref, w_ref, wdt_ref, wvt_ref, gq_ref, gk_ref, dtb_ref, alog_ref,
               z_ref, xbc_ref, dtp_ref, q_ref, k_ref, vt_ref):
    x = x_ref[0]
    ms = jnp.mean(x * x, axis=-1, keepdims=True)
    y = x * lax.rsqrt(ms + EPS) * g_ref[...]
    h = (y * (1.0 + mod_ref[0, 1:2, :]) + mod_ref[0, 0:1, :]).astype(bf16)

    z_ref[0] = _dot(h, w_ref[:, 0:SSD_INNER]).astype(bf16)
    o = SSD_INNER
    xbc_ref[0] = _dot(h, w_ref[:, o:o + CONV_CH]).astype(bf16)
    o += CONV_CH
    for dst_ref, gain_ref in ((q_ref, gq_ref), (k_ref, gk_ref)):
        full = _dot(h, w_ref[:, o:o + NA_INNER])
        for j in range(HEAD_PAIRS):
            sl = slice(j * LANES, (j + 1) * LANES)
            dst_ref[0, :, sl] = _head_rmsnorm(full[:, sl], gain_ref[:, sl]).astype(bf16)
        o += NA_INNER

    vt = _dot_nt(wvt_ref[...], h).astype(bf16)
    for p in range(HEAD_PAIRS):
        for tb in range(TOKEN_TILE // LANES):
            vt_ref[0, p, tb] = vt[p * LANES:(p + 1) * LANES, tb * LANES:(tb + 1) * LANES]

    raw = _dot(h, wdt_ref[...]) + dtb_ref[...]
    dt = jnp.maximum(raw, 0.0) + jnp.log1p(jnp.exp(-jnp.abs(raw)))
    lane = lax.broadcasted_iota(jnp.int32, (1, LANES), 1)
    a_neg = -jnp.exp(alog_ref[...])
    mul = jnp.where(lane < 2 * SSD_HEADS, a_neg,
                    jnp.where((lane >= 64) & (lane < 64 + 2 * SSD_HEADS), 1.0, 0.0))
    dtp_ref[0] = dt * mul


def _in_proj(x, mod, g_mix, w_main, w_dt, w_vt, gq, gk, dt_bias, a_log):
    B, L, _ = x.shape
    T = TOKEN_TILE
    nw = w_main.shape[1]
    const = lambda b, i: (0, 0)
    tok = lambda b, i: (b, i, 0)
    return pl.pallas_call(
        _in_kernel,
        grid=(B, L // T),
        in_specs=[pl.BlockSpec((1, T, D_MODEL), tok),
                  pl.BlockSpec((1, 8, D_MODEL), lambda b, i: (b, 0, 0)),
                  pl.BlockSpec((1, D_MODEL), const),
                  pl.BlockSpec((D_MODEL, nw), const),
                  pl.BlockSpec((D_MODEL, LANES), const),
                  pl.BlockSpec((NA_INNER, D_MODEL), const),
                  pl.BlockSpec((1, NA_INNER), const),
                  pl.BlockSpec((1, NA_INNER), const),
                  pl.BlockSpec((1, LANES), const),
                  pl.BlockSpec((1, LANES), const)],
        out_specs=[pl.BlockSpec((1, T, SSD_INNER), tok),
                   pl.BlockSpec((1, T, CONV_CH), tok),
                   pl.BlockSpec((1, T, LANES), tok),
                   pl.BlockSpec((1, T, NA_INNER), tok),
                   pl.BlockSpec((1, T, NA_INNER), tok),
                   pl.BlockSpec((1, HEAD_PAIRS, T // LANES, LANES, LANES), lambda b, i: (b, 0, i, 0, 0))],
        out_shape=[jax.ShapeDtypeStruct((B, L, SSD_INNER), bf16),
                   jax.ShapeDtypeStruct((B, L, CONV_CH), bf16),
                   jax.ShapeDtypeStruct((B, L, LANES), f32),
                   jax.ShapeDtypeStruct((B, L, NA_INNER), bf16),
                   jax.ShapeDtypeStruct((B, L, NA_INNER), bf16),
                   jax.ShapeDtypeStruct((B, HEAD_PAIRS, L // LANES, LANES, LANES), bf16)],
        compiler_params=pltpu.CompilerParams(dimension_semantics=("arbitrary", "arbitrary"),
                                             vmem_limit_bytes=VMEM_LIMIT),
        name="in_proj",
    )(x, mod, g_mix, w_main, w_dt, w_vt, gq, gk, dt_bias, a_log)


def _head_selector(base):
    r = lax.broadcasted_iota(jnp.int32, (LANES, SSD_INNER), 0)
    c = lax.broadcasted_iota(jnp.int32, (LANES, SSD_INNER), 1)
    return (lax.shift_right_logical(c, 6) == r - base).astype(bf16)


def _expand_heads(v, sel):
    hi = v.astype(bf16)
    lo = (v - hi.astype(f32)).astype(bf16)
    return _dot(hi, sel) + _dot(lo, sel)


def _chunk_decays(p):
    row = lax.broadcasted_iota(jnp.int32, (CHUNK, CHUNK), 0)
    col = lax.broadcasted_iota(jnp.int32, (CHUNK, CHUNK), 1)
    tri = (col <= row).astype(bf16)
    p1, p2, p3 = _split3(p)
    incl = _dot(tri, p1) + _dot(tri, p2) + _dot(tri, p3)
    excl = incl - p
    dts = pltpu.roll(p, 64, 1)
    tot = incl[CHUNK - 1:CHUNK, :]
    is_fwd = lax.broadcasted_iota(jnp.int32, (1, LANES), 1) < SSD_HEADS
    w = jnp.exp(jnp.where(is_fwd, tot - incl, excl)) * dts
    sc = jnp.exp(jnp.where(is_fwd, incl, tot - excl))
    return incl, excl, dts, w, sc, jnp.exp(tot)


def _conv_silu_tile(win_ref, cw_ref, cb_ref, out_ref):
    for c in range(SSD_TILE // CHUNK):
        for cb in range(CONV_CH // 256):
            cols = slice(cb * 256, (cb + 1) * 256)
            acc = jnp.broadcast_to(cb_ref[:, cols], (CHUNK, 256))
            for k in range(CONV_K):
                start = BF16_SUBLANES - CONV_K // 2 + k + c * CHUNK
                acc = acc + win_ref[start:start + CHUNK, cols] * cw_ref[k:k + 1, cols]
            out_ref[0, c * CHUNK:(c + 1) * CHUNK, cols] = _silu(acc).astype(bf16)


def _state_step(xo_ref, rows, st_ref, w, sc, dec, sel):
    w_x = _expand_heads(w, sel)
    sc_x = _expand_heads(sc, sel)
    dec_x = _expand_heads(jnp.broadcast_to(dec, (8, LANES)), sel)[0:1]
    outs = []
    for g in range(SSD_GROUPS):
        gcols = slice(g * GROUP_W, (g + 1) * GROUP_W)
        xs_g = xo_ref[0, rows, g * GROUP_W:(g + 1) * GROUP_W].astype(f32)
        b_g = xo_ref[0, rows, SSD_INNER + g * SSD_STATE:SSD_INNER + (g + 1) * SSD_STATE].astype(f32)
        c_g = xo_ref[0, rows, SSD_INNER + (SSD_GROUPS + g) * SSD_STATE:
                     SSD_INNER + (SSD_GROUPS + g + 1) * SSD_STATE]
        s_g = st_ref[g]
        y_off = _dot(c_g, s_g.astype(bf16)) * sc_x[:, gcols]
        xw = (xs_g * w_x[:, gcols]).astype(bf16)
        st_ref[g] = s_g * dec_x[:, gcols] + _dot(b_g.T.astype(bf16), xw)
        outs.append(y_off)
    return jnp.concatenate(outs, axis=1)


def _ssd_bwd_kernel(xc_ref, xp_ref, xn_ref, dtp_ref, cw_ref, cb_ref, xo_ref, yb_ref, win_ref, st_ref, *, n_tiles):
    i = pl.program_id(1)
    t = n_tiles - 1 - i

    @pl.when(i == 0)
    def _():
        st_ref[...] = jnp.zeros_like(st_ref)

    H = BF16_SUBLANES
    win_ref[0:H, :] = jnp.where(t > 0, xp_ref[0].astype(f32), 0.0)
    win_ref[H:H + SSD_TILE, :] = xc_ref[0].astype(f32)
    win_ref[H + SSD_TILE:2 * H + SSD_TILE, :] = jnp.where(t < n_tiles - 1, xn_ref[0].astype(f32), 0.0)
    _conv_silu_tile(win_ref, cw_ref, cb_ref, xo_ref)
    sel = _head_selector(SSD_HEADS)

    def body(j, carry):
        c = SSD_TILE // CHUNK - 1 - j
        rows = pl.ds(pl.multiple_of(c * CHUNK, CHUNK), CHUNK)
        _, _, _, w, sc, dec = _chunk_decays(dtp_ref[0, rows, :])
        yb_ref[0, rows, :] = _state_step(xo_ref, rows, st_ref, w, sc, dec, sel).astype(bf16)
        return carry

    lax.fori_loop(0, SSD_TILE // CHUNK, body, 0, unroll=SSD_UNROLL)


def _ssd_bwd(xbc, dtp, conv_w, conv_b):
    B, L, _ = xbc.shape
    T = SSD_TILE
    nT = L // T
    hb = T // BF16_SUBLANES
    cur = lambda b, i: (b, nT - 1 - i, 0)
    prev = lambda b, i: (b, jnp.maximum((nT - 1 - i) * hb - 1, 0), 0)
    nxt = lambda b, i: (b, jnp.minimum((nT - i) * hb, L // BF16_SUBLANES - 1), 0)
    const = lambda b, i: (0, 0)
    return pl.pallas_call(
        functools.partial(_ssd_bwd_kernel, n_tiles=nT),
        grid=(B, nT),
        in_specs=[pl.BlockSpec((1, T, CONV_CH), cur),
                  pl.BlockSpec((1, BF16_SUBLANES, CONV_CH), prev),
                  pl.BlockSpec((1, BF16_SUBLANES, CONV_CH), nxt),
                  pl.BlockSpec((1, T, LANES), cur),
                  pl.BlockSpec((CONV_K, CONV_CH), const),
                  pl.BlockSpec((1, CONV_CH), const)],
        out_specs=[pl.BlockSpec((1, T, CONV_CH), cur),
                   pl.BlockSpec((1, T, SSD_INNER), cur)],
        out_shape=[jax.ShapeDtypeStruct((B, L, CONV_CH), bf16),
                   jax.ShapeDtypeStruct((B, L, SSD_INNER), bf16)],
        scratch_shapes=[pltpu.VMEM((T + 2 * BF16_SUBLANES, CONV_CH), f32),
                        pltpu.VMEM((SSD_GROUPS, SSD_STATE, GROUP_W), f32)],
        compiler_params=pltpu.CompilerParams(dimension_semantics=("arbitrary", "arbitrary"),
                                             vmem_limit_bytes=VMEM_LIMIT),
        name="ssd_bwd",
    )(xbc, xbc, xbc, dtp, conv_w, conv_b)


def _ssd_fwd_kernel(xo_ref, dtp_ref, yb_ref, z_ref, dsk_ref, gs_ref, o_ref, st_ref):
    @pl.when(pl.program_id(1) == 0)
    def _():
        st_ref[...] = jnp.zeros_like(st_ref)

    row = lax.broadcasted_iota(jnp.int32, (CHUNK, CHUNK), 0)
    col = lax.broadcasted_iota(jnp.int32, (CHUNK, CHUNK), 1)
    low = col < row
    diag = col == row
    me = _even_lane_mask()
    sel = _head_selector(0)

    def body(c, carry):
        rows = pl.ds(pl.multiple_of(c * CHUNK, CHUNK), CHUNK)
        p = dtp_ref[0, rows, :]
        incl, excl, dts, w, sc, dec = _chunk_decays(p)
        y = _state_step(xo_ref, rows, st_ref, w, sc, dec, sel)

        incl_t = incl.T
        excl_t = excl.T
        dts_t = dts.T
        y_diag = []
        for g in range(SSD_GROUPS):
            b_g = xo_ref[0, rows, SSD_INNER + g * SSD_STATE:SSD_INNER + (g + 1) * SSD_STATE].astype(f32)
            c_g = xo_ref[0, rows, SSD_INNER + (SSD_GROUPS + g) * SSD_STATE:
                         SSD_INNER + (SSD_GROUPS + g + 1) * SSD_STATE]
            cb = _dot(c_g, b_g.T.astype(bf16))
            for pr in range(2):
                xs_pair = xo_ref[0, rows, g * GROUP_W + pr * LANES:g * GROUP_W + (pr + 1) * LANES]
                acc = None
                for e in range(2):
                    hd = 4 * g + 2 * pr + e
                    af_c = jnp.broadcast_to(incl[:, hd:hd + 1], (CHUNK, CHUNK))
                    eb_c = jnp.broadcast_to(excl[:, SSD_HEADS + hd:SSD_HEADS + hd + 1], (CHUNK, CHUNK))
                    af_r = incl_t[hd:hd + 1, :]
                    eb_r = excl_t[SSD_HEADS + hd:SSD_HEADS + hd + 1, :]
                    df_r = dts_t[hd:hd + 1, :]
                    db_r = dts_t[SSD_HEADS + hd:SSD_HEADS + hd + 1, :]
                    arg = jnp.where(low, af_c - af_r, eb_r - eb_c)
                    coef = jnp.where(low, df_r, jnp.where(diag, df_r + db_r, db_r))
                    wm = (cb * jnp.exp(arg) * coef).astype(bf16)
                    xm = jnp.where(me, xs_pair, jnp.zeros_like(xs_pair)) if e == 0 else \
                        jnp.where(me, jnp.zeros_like(xs_pair), xs_pair)
                    d = _dot(wm, xm)
                    acc = d if acc is None else acc + d
                y_diag.append(acc)
        xs = xo_ref[0, rows, 0:SSD_INNER].astype(f32)
        y = y + jnp.concatenate(y_diag, axis=1) + yb_ref[0, rows, :].astype(f32) + xs * dsk_ref[...]
        gated = y * _silu(z_ref[0, rows, :].astype(f32))
        ms = jnp.mean(gated * gated, axis=-1, keepdims=True)
        o_ref[0, rows, :] = (gated * lax.rsqrt(ms + EPS) * gs_ref[...]).astype(bf16)
        return carry

    lax.fori_loop(0, SSD_TILE // CHUNK, body, 0, unroll=SSD_UNROLL)


def _ssd_fwd(xo, dtp, yb, z, d_skip_x, g_ssd):
    B, L, _ = xo.shape
    T = SSD_TILE
    tok = lambda b, i: (b, i, 0)
    const = lambda b, i: (0, 0)
    return pl.pallas_call(
        _ssd_fwd_kernel,
        grid=(B, L // T),
        in_specs=[pl.BlockSpec((1, T, CONV_CH), tok),
                  pl.BlockSpec((1, T, LANES), tok),
                  pl.BlockSpec((1, T, SSD_INNER), tok),
                  pl.BlockSpec((1, T, SSD_INNER), tok),
                  pl.BlockSpec((1, SSD_INNER), const),
                  pl.BlockSpec((1, SSD_INNER), const)],
        out_specs=pl.BlockSpec((1, T, SSD_INNER), tok),
        out_shape=jax.ShapeDtypeStruct((B, L, SSD_INNER), bf16),
        scratch_shapes=[pltpu.VMEM((SSD_GROUPS, SSD_STATE, GROUP_W), f32)],
        compiler_params=pltpu.CompilerParams(dimension_semantics=("arbitrary", "arbitrary"),
                                             vmem_limit_bytes=VMEM_LIMIT),
        name="ssd_fwd",
    )(xo, dtp, yb, z, d_skip_x, g_ssd)


def _na_bias_tables(rpb):
    variants = [(0, 0, 0), (2, 0, 0), (4, 0, 1), (6, 2, 2), (8, 2, 2)]
    kc = np.arange(GRID_W)[:, None]
    w = np.arange(GRID_W)[None, :]
    cs = np.clip(w - NA_COLS // 2, 0, GRID_W - NA_COLS)
    col_ok = (kc >= cs) & (kc < cs + NA_COLS)
    padded = jnp.pad(rpb.astype(f32) * LOG2E, ((0, 0), (0, 0), (GRID_W, GRID_W)))
    lo = GRID_W + NA_COLS - 1
    base = jnp.stack([padded[:, :, lo - q:lo - q + GRID_W] for q in range(GRID_W)], axis=-1)
    base = jnp.where(col_ok, base, NEG_BIG).reshape(HEAD_PAIRS, 2, 2 * NA_ROWS - 1, GRID_W, GRID_W)
    neg = jnp.full((HEAD_PAIRS, GRID_W, GRID_W), NEG_BIG, f32)
    tabs = []
    for roff, st0, st1 in variants:
        pieces = []
        for e in range(2):
            for par, st in ((0, st0), (1, st1)):
                blocks = [base[:, e, kr - (roff + par) + NA_ROWS - 1] if st <= kr < st + NA_ROWS else neg
                          for kr in range(NA_WIN_ROWS)]
                pieces.append(jnp.stack(blocks, axis=1))
        tabs.append(jnp.concatenate(pieces, axis=-1).reshape(HEAD_PAIRS, NA_WIN, 2 * LANES))
    return jnp.stack(tabs)


def _na_kernel(q_ref, k_ref, vt_ref, tab_ref, o_ref, *, grid_rows):
    step = pl.program_id(2)
    me = _even_lane_mask()

    def window(i):
        r = 2 * (step * NA_PAIRS_PER_STEP + i)
        u0 = jnp.clip(r - NA_ROWS // 2, 0, grid_rows - NA_WIN_ROWS)
        var = jnp.where(r < 4, r // 2, jnp.where(r >= grid_rows - 4, 3 + (r - (grid_rows - 4)) // 2, 2))
        return u0, var

    def scores(i):
        u0, var = window(i)
        q2 = q_ref[0, i * LANES:(i + 1) * LANES, :]
        zero = jnp.zeros_like(q2)
        qm = jnp.concatenate([jnp.where(me, q2, zero), jnp.where(me, zero, q2)], axis=0)
        kk = k_ref[0, pl.ds(pl.multiple_of(u0 * GRID_W, LANES), NA_WIN), :]
        return _dot_nt(kk, qm) + tab_ref[var, 0]

    s = scores(0)
    for i in range(NA_PAIRS_PER_STEP):
        s_next = scores(i + 1) if i + 1 < NA_PAIRS_PER_STEP else None
        u0, _ = window(i)
        m = jnp.max(s, axis=0, keepdims=True)
        p = jnp.exp2(s - m)
        l = jnp.sum(p, axis=0, keepdims=True)
        j0 = u0 // 2
        vtw = jnp.concatenate([vt_ref[0, 0, j0 + j] for j in range(NA_WIN // LANES)], axis=1)
        ot = _dot(vtw, p.astype(bf16)) * (1.0 / l)
        oc = jnp.concatenate([ot[0:HEAD_DIM, 0:LANES], ot[HEAD_DIM:LANES, LANES:2 * LANES]], axis=0)
        o_ref[0, i * LANES:(i + 1) * LANES, :] = oc.T.astype(bf16)
        s = s_next


def _na(q, k, vt, tab):
    B, L, _ = q.shape
    grid_rows = L // GRID_W
    QT = NA_PAIRS_PER_STEP * LANES
    nv = tab.shape[0]
    return pl.pallas_call(
        functools.partial(_na_kernel, grid_rows=grid_rows),
        grid=(B, HEAD_PAIRS, L // QT),
        in_specs=[pl.BlockSpec((1, QT, LANES), lambda b, p, s: (b, s, p)),
                  pl.BlockSpec((1, L, LANES), lambda b, p, s: (b, 0, p)),
                  pl.BlockSpec((1, 1, L // LANES, LANES, LANES), lambda b, p, s: (b, p, 0, 0, 0)),
                  pl.BlockSpec((nv, 1, NA_WIN, 2 * LANES), lambda b, p, s: (0, p, 0, 0))],
        out_specs=pl.BlockSpec((1, QT, LANES), lambda b, p, s: (b, s, p)),
        out_shape=jax.ShapeDtypeStruct((B, L, NA_INNER), bf16),
        compiler_params=pltpu.CompilerParams(dimension_semantics=("arbitrary", "arbitrary", "arbitrary"),
                                             vmem_limit_bytes=VMEM_LIMIT),
        name="na",
    )(q, k, vt, tab)


def _out_kernel(x_ref, ys_ref, yn_ref, mod_ref, wo_ref, gf_ref, wg_ref, wu_ref, wd_ref, o_ref, a_ref):
    mix = _dot(ys_ref[0], wo_ref[0:SSD_INNER, :]) + _dot(yn_ref[0], wo_ref[SSD_INNER:, :])
    x1 = x_ref[0] + mod_ref[0, 2:3, :] * mix
    ms = jnp.mean(x1 * x1, axis=-1, keepdims=True)
    h2 = x1 * lax.rsqrt(ms + EPS) * gf_ref[...]
    h2 = (h2 * (1.0 + mod_ref[0, 4:5, :]) + mod_ref[0, 3:4, :]).astype(bf16)
    for c in range(D_FF // FF_BLOCK):
        cols = slice(c * FF_BLOCK, (c + 1) * FF_BLOCK)
        gate = _dot(h2, wg_ref[:, cols])
        up = _dot(h2, wu_ref[:, cols])
        a_ref[:, cols] = (_silu(gate) * up).astype(bf16)
    ff = _dot(a_ref[...], wd_ref[...])
    o_ref[0] = x1 + mod_ref[0, 5:6, :] * ff


def _out_ffn(x, y_ssd, y_na, mod, w_out, g_ffn, w_gate, w_up, w_down):
    B, L, _ = x.shape
    T = TOKEN_TILE
    tok = lambda b, i: (b, i, 0)
    const = lambda b, i: (0, 0)
    single = pl.Buffered(1)
    return pl.pallas_call(
        _out_kernel,
        grid=(B, L // T),
        in_specs=[pl.BlockSpec((1, T, D_MODEL), tok),
                  pl.BlockSpec((1, T, SSD_INNER), tok),
                  pl.BlockSpec((1, T, NA_INNER), tok),
                  pl.BlockSpec((1, 8, D_MODEL), lambda b, i: (b, 0, 0)),
                  pl.BlockSpec((D_MODEL, D_MODEL), const, pipeline_mode=single),
                  pl.BlockSpec((1, D_MODEL), const),
                  pl.BlockSpec((D_MODEL, D_FF), const, pipeline_mode=single),
                  pl.BlockSpec((D_MODEL, D_FF), const, pipeline_mode=single),
                  pl.BlockSpec((D_FF, D_MODEL), const, pipeline_mode=single)],
        out_specs=pl.BlockSpec((1, T, D_MODEL), tok),
        out_shape=jax.ShapeDtypeStruct((B, L, D_MODEL), f32),
        scratch_shapes=[pltpu.VMEM((T, D_FF), bf16)],
        compiler_params=pltpu.CompilerParams(dimension_semantics=("arbitrary", "arbitrary"),
                                             vmem_limit_bytes=VMEM_LIMIT),
        name="out_ffn",
    )(x, y_ssd, y_na, mod, w_out, g_ffn, w_gate, w_up, w_down)


def _pad_lanes(parts):
    row = jnp.zeros((LANES,), f32)
    for off, v in parts:
        row = row.at[off:off + v.shape[0]].set(v.astype(f32))
    return row.reshape(1, LANES)


def kernel(x_prompt, x_sample, c_prompt, c_sample, w_ada, b_ada, g_mix, w_in, conv_w, conv_b, dt_bias_fwd, dt_bias_bwd, a_log_fwd, a_log_bwd, d_skip, g_ssd, g_q, g_k, rpb, w_out, g_ffn, w_gate, w_up, w_down):
    l = 0
    nb_p = x_prompt.shape[0]
    nb_s = x_sample.shape[0]

    c_all = jnp.zeros((8, D_MODEL), f32).at[:nb_p].set(c_prompt).at[nb_p:nb_p + nb_s].set(c_sample)
    mod = _modulation(c_all, w_ada[l], b_ada[l]).reshape(8, 6, D_MODEL)
    mod = jnp.pad(mod, ((0, 0), (0, 2), (0, 0)))

    o1 = SSD_INNER
    o2 = o1 + CONV_CH
    o3 = o2 + 2 * SSD_HEADS
    o4 = o3 + NA_INNER
    o5 = o4 + NA_INNER
    wi = w_in[l]
    w_main = jnp.concatenate([wi[:, :o2], wi[:, o3:o5]], axis=1).astype(bf16)
    w_dt16 = wi[:, o2:o3]
    w_dt = jnp.zeros((D_MODEL, LANES), f32).at[:, 0:16].set(w_dt16).at[:, 64:80].set(w_dt16).astype(bf16)
    w_vt = wi[:, o5:].T.astype(bf16)
    dtb = _pad_lanes([(0, dt_bias_fwd[l]), (8, dt_bias_bwd[l]), (64, dt_bias_fwd[l]), (72, dt_bias_bwd[l])])
    alog = _pad_lanes([(0, a_log_fwd[l]), (8, a_log_bwd[l])])
    gq = jnp.tile(g_q[l].astype(f32) * (HEAD_DIM ** -0.5 * LOG2E), NA_HEADS).reshape(1, NA_INNER)
    gk = jnp.tile(g_k[l].astype(f32), NA_HEADS).reshape(1, NA_INNER)
    dsk = jnp.repeat(d_skip[l].astype(f32), HEAD_DIM).reshape(1, SSD_INNER)
    tab = _na_bias_tables(rpb[l])
    wo = w_out[l].astype(bf16)
    wg = w_gate[l].astype(bf16)
    wu = w_up[l].astype(bf16)
    wd = w_down[l].astype(bf16)
    row = lambda v: v.astype(f32).reshape(1, -1)

    def trunk(x, m):
        z, xbc, dtp, q, k, vt = _in_proj(x, m, row(g_mix[l]), w_main, w_dt, w_vt, gq, gk, dtb, alog)
        xo, yb = _ssd_bwd(xbc, dtp, conv_w[l].astype(f32), row(conv_b[l]))
        y_ssd = _ssd_fwd(xo, dtp, yb, z, dsk, row(g_ssd[l]))
        y_na = _na(q, k, vt, tab)
        return _out_ffn(x, y_ssd, y_na, m, wo, row(g_ffn[l]), wg, wu, wd)

    return (trunk(x_prompt, mod[:nb_p]), trunk(x_sample, mod[nb_p:nb_p + nb_s]))
```

```python
import functools
import math

import numpy as np
import jax
import jax.numpy as jnp
from jax import lax
from jax.experimental import pallas as pl
from jax.experimental.pallas import tpu as pltpu

D_MODEL = 1024
HEAD_DIM = 64
SSD_HEADS = 8
SSD_INNER = SSD_HEADS * HEAD_DIM
SSD_GROUPS = 2
SSD_STATE = 128
CONV_K = 5
CHUNK = 128
NA_HEADS = 8
NA_INNER = NA_HEADS * HEAD_DIM
NA_ROWS = 8
NA_COLS = 16
GRID_W = 64
CONV_CH = SSD_INNER + 2 * SSD_GROUPS * SSD_STATE
D_FF = 2816
EPS = 1e-6

LANES = 128
HALO = 8
CONV_ROWS = 128
CONV_COLS = 256
GROUP_W = (SSD_HEADS // SSD_GROUPS) * HEAD_DIM
HEAD_PAIRS = NA_HEADS // 2
LOG2E = math.log2(math.e)
NEG_BIG = -1e30

TOKEN_TILE = 512
SSD_TILE = 512
SSD_UNROLL = 2
NA_WIN_ROWS = 10
NA_WIN = NA_WIN_ROWS * GRID_W
NA_PAIRS_PER_STEP = 8
NA_UNROLL = 4
FF_BLOCK = 256
VMEM_LIMIT = 56 * 1024 * 1024

f32 = jnp.float32
bf16 = jnp.bfloat16


def _dot(a, b):
    return jnp.dot(a, b, preferred_element_type=f32)


def _dot_nt(a, b):
    return lax.dot_general(a, b, (((1,), (1,)), ((), ())), preferred_element_type=f32)


def _split3(v):
    p1 = v.astype(bf16)
    r1 = v - p1.astype(f32)
    p2 = r1.astype(bf16)
    p3 = (r1 - p2.astype(f32)).astype(bf16)
    return p1, p2, p3


def _even_lane_mask():
    return lax.broadcasted_iota(jnp.int32, (1, LANES), 1) < HEAD_DIM


def _silu(v):
    return v * jax.nn.sigmoid(v)


def _mod_kernel(c_ref, w_ref, b_ref, o_ref):
    s = _silu(c_ref[...])
    o_ref[...] = jnp.dot(s, w_ref[...], precision=lax.Precision.HIGHEST,
                         preferred_element_type=f32) + b_ref[...]


def _modulation(c_all, w_ada, b_ada):
    n = 6 * D_MODEL
    return pl.pallas_call(
        _mod_kernel,
        grid=(n // D_MODEL,),
        in_specs=[pl.BlockSpec((8, D_MODEL), lambda j: (0, 0)),
                  pl.BlockSpec((D_MODEL, D_MODEL), lambda j: (0, j)),
                  pl.BlockSpec((1, D_MODEL), lambda j: (0, j))],
        out_specs=pl.BlockSpec((8, D_MODEL), lambda j: (0, j)),
        out_shape=jax.ShapeDtypeStruct((8, n), f32),
        name="mod",
    )(c_all, w_ada, b_ada.reshape(1, n))


def _head_rmsnorm(v, gain):
    me = _even_lane_mask()
    s = v * v
    se = jnp.sum(jnp.where(me, s, 0.0), axis=-1, keepdims=True)
    so = jnp.sum(jnp.where(me, 0.0, s), axis=-1, keepdims=True)
    ms = jnp.where(me, se, so) * (1.0 / HEAD_DIM)
    return v * lax.rsqrt(ms + EPS) * gain


def _in_kernel(x_ref, xp_ref, xn_ref, mod_ref, g_ref, w_ref, wdt_ref, wvt_ref, gq_ref, gk_ref, dtb_ref,
               alog_ref, cw_ref, cb_ref, z_ref, xbc_ref, dtp_ref, q_ref, k_ref, vt_ref, win_ref):
    i = pl.program_id(1)
    T = TOKEN_TILE

    def norm_mod(x):
        ms = jnp.mean(x * x, axis=-1, keepdims=True)
        y = x * lax.rsqrt(ms + EPS) * g_ref[...]
        return y * (1.0 + mod_ref[0, 1:2, :]) + mod_ref[0, 0:1, :]

    hf = norm_mod(x_ref[0])
    h = hf.astype(bf16)

    o = SSD_INNER
    h_ext = jnp.concatenate([norm_mod(xp_ref[0]), hf, norm_mod(xn_ref[0])], axis=0).astype(bf16)
    win_ref[...] = _dot(h_ext, w_ref[:, o:o + CONV_CH])
    win_ref[0:HALO, :] = jnp.where(i > 0, win_ref[0:HALO, :], 0.0)
    win_ref[T + HALO:T + 2 * HALO, :] = jnp.where(i < pl.num_programs(1) - 1, win_ref[T + HALO:T + 2 * HALO, :], 0.0)
    def conv_rows(c):
        for cb in range(CONV_CH // CONV_COLS):
            cols = slice(cb * CONV_COLS, (cb + 1) * CONV_COLS)
            acc = jnp.broadcast_to(cb_ref[:, cols], (CONV_ROWS, CONV_COLS))
            for k in range(CONV_K):
                start = HALO - CONV_K // 2 + k + c * CONV_ROWS
                acc = acc + win_ref[start:start + CONV_ROWS, cols] * cw_ref[k:k + 1, cols]
            xbc_ref[0, c * CONV_ROWS:(c + 1) * CONV_ROWS, cols] = _silu(acc).astype(bf16)

    def proj_z():
        z_ref[0] = _dot(h, w_ref[:, 0:SSD_INNER]).astype(bf16)

    def proj_head_normed(dst_ref, gain_ref, col0):
        full = _dot(h, w_ref[:, col0:col0 + NA_INNER])
        for j in range(HEAD_PAIRS):
            sl = slice(j * LANES, (j + 1) * LANES)
            dst_ref[0, :, sl] = _head_rmsnorm(full[:, sl], gain_ref[:, sl]).astype(bf16)

    def proj_v_dt():
        vt = _dot_nt(wvt_ref[...], h).astype(bf16)
        for p in range(HEAD_PAIRS):
            for tb in range(TOKEN_TILE // LANES):
                vt_ref[0, p, tb] = vt[p * LANES:(p + 1) * LANES, tb * LANES:(tb + 1) * LANES]
        raw = _dot(h, wdt_ref[...]) + dtb_ref[...]
        dt = jnp.maximum(raw, 0.0) + jnp.log1p(jnp.exp(-jnp.abs(raw)))
        lane = lax.broadcasted_iota(jnp.int32, (1, LANES), 1)
        a_neg = -jnp.exp(alog_ref[...])
        mul = jnp.where(lane < 2 * SSD_HEADS, a_neg,
                        jnp.where((lane >= 64) & (lane < 64 + 2 * SSD_HEADS), 1.0, 0.0))
        dtp_ref[0] = dt * mul

    o += CONV_CH
    stages = [proj_z, functools.partial(proj_head_normed, q_ref, gq_ref, o),
              functools.partial(proj_head_normed, k_ref, gk_ref, o + NA_INNER), proj_v_dt]
    for c in range(T // CONV_ROWS):
        stages[c]()
        conv_rows(c)


def _in_proj(x, mod, g_mix, w_main, w_dt, w_vt, gq, gk, dt_bias, a_log, conv_w, conv_b):
    B, L, _ = x.shape
    T = TOKEN_TILE
    nw = w_main.shape[1]
    hb = T // HALO
    const = lambda b, i: (0, 0)
    tok = lambda b, i: (b, i, 0)
    prev = lambda b, i: (b, jnp.maximum(i * hb - 1, 0), 0)
    nxt = lambda b, i: (b, jnp.minimum((i + 1) * hb, L // HALO - 1), 0)
    return pl.pallas_call(
        _in_kernel,
        grid=(B, L // T),
        in_specs=[pl.BlockSpec((1, T, D_MODEL), tok),
                  pl.BlockSpec((1, HALO, D_MODEL), prev),
                  pl.BlockSpec((1, HALO, D_MODEL), nxt),
                  pl.BlockSpec((1, 8, D_MODEL), lambda b, i: (b, 0, 0)),
                  pl.BlockSpec((1, D_MODEL), const),
                  pl.BlockSpec((D_MODEL, nw), const),
                  pl.BlockSpec((D_MODEL, LANES), const),
                  pl.BlockSpec((NA_INNER, D_MODEL), const),
                  pl.BlockSpec((1, NA_INNER), const),
                  pl.BlockSpec((1, NA_INNER), const),
                  pl.BlockSpec((1, LANES), const),
                  pl.BlockSpec((1, LANES), const),
                  pl.BlockSpec((CONV_K, CONV_CH), const),
                  pl.BlockSpec((1, CONV_CH), const)],
        out_specs=[pl.BlockSpec((1, T, SSD_INNER), tok),
                   pl.BlockSpec((1, T, CONV_CH), tok),
                   pl.BlockSpec((1, T, LANES), tok),
                   pl.BlockSpec((1, T, NA_INNER), tok),
                   pl.BlockSpec((1, T, NA_INNER), tok),
                   pl.BlockSpec((1, HEAD_PAIRS, T // LANES, LANES, LANES), lambda b, i: (b, 0, i, 0, 0))],
        scratch_shapes=[pltpu.VMEM((T + 2 * HALO, CONV_CH), f32)],
        out_shape=[jax.ShapeDtypeStruct((B, L, SSD_INNER), bf16),
                   jax.ShapeDtypeStruct((B, L, CONV_CH), bf16),
                   jax.ShapeDtypeStruct((B, L, LANES), f32),
                   jax.ShapeDtypeStruct((B, L, NA_INNER), bf16),
                   jax.ShapeDtypeStruct((B, L, NA_INNER), bf16),
                   jax.ShapeDtypeStruct((B, HEAD_PAIRS, L // LANES, LANES, LANES), bf16)],
        compiler_params=pltpu.CompilerParams(dimension_semantics=("arbitrary", "arbitrary"),
                                             vmem_limit_bytes=VMEM_LIMIT),
        name="in_proj",
    )(x, x, x, mod, g_mix, w_main, w_dt, w_vt, gq, gk, dt_bias, a_log, conv_w, conv_b)


def _head_selector(base):
    r = lax.broadcasted_iota(jnp.int32, (LANES, SSD_INNER), 0)
    c = lax.broadcasted_iota(jnp.int32, (LANES, SSD_INNER), 1)
    return (lax.shift_right_logical(c, 6) == r - base).astype(bf16)


def _expand_heads(v, sel):
    hi = v.astype(bf16)
    lo = (v - hi.astype(f32)).astype(bf16)
    return _dot(hi, sel) + _dot(lo, sel)


def _chunk_decays(p):
    row = lax.broadcasted_iota(jnp.int32, (CHUNK, CHUNK), 0)
    col = lax.broadcasted_iota(jnp.int32, (CHUNK, CHUNK), 1)
    tri = (col <= row).astype(bf16)
    p1, p2, p3 = _split3(p)
    incl = _dot(tri, p1) + _dot(tri, p2) + _dot(tri, p3)
    excl = incl - p
    dts = pltpu.roll(p, 64, 1)
    tot = incl[CHUNK - 1:CHUNK, :]
    is_fwd = lax.broadcasted_iota(jnp.int32, (1, LANES), 1) < SSD_HEADS
    w = jnp.exp(jnp.where(is_fwd, tot - incl, excl)) * dts
    sc = jnp.exp(jnp.where(is_fwd, incl, tot - excl))
    return incl, excl, dts, w, sc, jnp.exp(tot)


def _state_step(xo_ref, rows, st_ref, w, sc, dec, sel):
    w_x = _expand_heads(w, sel)
    sc_x = _expand_heads(sc, sel)
    dec_x = _expand_heads(jnp.broadcast_to(dec, (8, LANES)), sel)[0:1]
    outs = []
    for g in range(SSD_GROUPS):
        gcols = slice(g * GROUP_W, (g + 1) * GROUP_W)
        xs_g = xo_ref[0, rows, g * GROUP_W:(g + 1) * GROUP_W].astype(f32)
        b_g = xo_ref[0, rows, SSD_INNER + g * SSD_STATE:SSD_INNER + (g + 1) * SSD_STATE].astype(f32)
        c_g = xo_ref[0, rows, SSD_INNER + (SSD_GROUPS + g) * SSD_STATE:
                     SSD_INNER + (SSD_GROUPS + g + 1) * SSD_STATE]
        s_g = st_ref[g]
        y_off = _dot(c_g, s_g.astype(bf16)) * sc_x[:, gcols]
        xw = (xs_g * w_x[:, gcols]).astype(bf16)
        st_ref[g] = s_g * dec_x[:, gcols] + _dot(b_g.T.astype(bf16), xw)
        outs.append(y_off)
    return jnp.concatenate(outs, axis=1)


def _ssd_bwd_kernel(xo_ref, dtp_ref, yb_ref, st_ref):
    @pl.when(pl.program_id(1) == 0)
    def _():
        st_ref[...] = jnp.zeros_like(st_ref)

    sel = _head_selector(SSD_HEADS)

    def body(j, carry):
        c = SSD_TILE // CHUNK - 1 - j
        rows = pl.ds(pl.multiple_of(c * CHUNK, CHUNK), CHUNK)
        _, _, _, w, sc, dec = _chunk_decays(dtp_ref[0, rows, :])
        yb_ref[0, rows, :] = _state_step(xo_ref, rows, st_ref, w, sc, dec, sel).astype(bf16)
        return carry

    lax.fori_loop(0, SSD_TILE // CHUNK, body, 0, unroll=SSD_UNROLL)


def _ssd_bwd(xo, dtp):
    B, L, _ = xo.shape
    T = SSD_TILE
    nT = L // T
    cur = lambda b, i: (b, nT - 1 - i, 0)
    return pl.pallas_call(
        _ssd_bwd_kernel,
        grid=(B, nT),
        in_specs=[pl.BlockSpec((1, T, CONV_CH), cur),
                  pl.BlockSpec((1, T, LANES), cur)],
        out_specs=pl.BlockSpec((1, T, SSD_INNER), cur),
        out_shape=jax.ShapeDtypeStruct((B, L, SSD_INNER), bf16),
        scratch_shapes=[pltpu.VMEM((SSD_GROUPS, SSD_STATE, GROUP_W), f32)],
        compiler_params=pltpu.CompilerParams(dimension_semantics=("arbitrary", "arbitrary"),
                                             vmem_limit_bytes=VMEM_LIMIT),
        name="ssd_bwd",
    )(xo, dtp)


def _ssd_fwd_kernel(xo_ref, dtp_ref, yb_ref, z_ref, dsk_ref, gs_ref, o_ref, st_ref):
    @pl.when(pl.program_id(1) == 0)
    def _():
        st_ref[...] = jnp.zeros_like(st_ref)

    row = lax.broadcasted_iota(jnp.int32, (CHUNK, CHUNK), 0)
    col = lax.broadcasted_iota(jnp.int32, (CHUNK, CHUNK), 1)
    low = col < row
    diag = col == row
    me = _even_lane_mask()
    sel = _head_selector(0)

    def body(c, carry):
        rows = pl.ds(pl.multiple_of(c * CHUNK, CHUNK), CHUNK)
        p = dtp_ref[0, rows, :]
        incl, excl, dts, w, sc, dec = _chunk_decays(p)
        y = _state_step(xo_ref, rows, st_ref, w, sc, dec, sel)

        incl_t = incl.T
        excl_t = excl.T
        dts_t = dts.T
        y_diag = []
        for g in range(SSD_GROUPS):
            b_g = xo_ref[0, rows, SSD_INNER + g * SSD_STATE:SSD_INNER + (g + 1) * SSD_STATE].astype(f32)
            c_g = xo_ref[0, rows, SSD_INNER + (SSD_GROUPS + g) * SSD_STATE:
                         SSD_INNER + (SSD_GROUPS + g + 1) * SSD_STATE]
            cb = _dot(c_g, b_g.T.astype(bf16))
            for pr in range(2):
                xs_pair = xo_ref[0, rows, g * GROUP_W + pr * LANES:g * GROUP_W + (pr + 1) * LANES]
                acc = None
                for e in range(2):
                    hd = 4 * g + 2 * pr + e
                    af_c = jnp.broadcast_to(incl[:, hd:hd + 1], (CHUNK, CHUNK))
                    eb_c = jnp.broadcast_to(excl[:, SSD_HEADS + hd:SSD_HEADS + hd + 1], (CHUNK, CHUNK))
                    af_r = incl_t[hd:hd + 1, :]
                    eb_r = excl_t[SSD_HEADS + hd:SSD_HEADS + hd + 1, :]
                    df_r = dts_t[hd:hd + 1, :]
                    db_r = dts_t[SSD_HEADS + hd:SSD_HEADS + hd + 1, :]
                    arg = jnp.where(low, af_c - af_r, eb_r - eb_c)
                    coef = jnp.where(low, df_r, jnp.where(diag, df_r + db_r, db_r))
                    wm = (cb * jnp.exp(arg) * coef).astype(bf16)
                    xm = jnp.where(me, xs_pair, jnp.zeros_like(xs_pair)) if e == 0 else \
                        jnp.where(me, jnp.zeros_like(xs_pair), xs_pair)
                    d = _dot(wm, xm)
                    acc = d if acc is None else acc + d
                y_diag.append(acc)
        xs = xo_ref[0, rows, 0:SSD_INNER].astype(f32)
        y = y + jnp.concatenate(y_diag, axis=1) + yb_ref[0, rows, :].astype(f32) + xs * dsk_ref[...]
        gated = y * _silu(z_ref[0, rows, :].astype(f32))
        ms = jnp.mean(gated * gated, axis=-1, keepdims=True)
        o_ref[0, rows, :] = (gated * lax.rsqrt(ms + EPS) * gs_ref[...]).astype(bf16)
        return carry

    lax.fori_loop(0, SSD_TILE // CHUNK, body, 0, unroll=SSD_UNROLL)


def _ssd_fwd(xo, dtp, yb, z, d_skip_x, g_ssd):
    B, L, _ = xo.shape
    T = SSD_TILE
    tok = lambda b, i: (b, i, 0)
    const = lambda b, i: (0, 0)
    return pl.pallas_call(
        _ssd_fwd_kernel,
        grid=(B, L // T),
        in_specs=[pl.BlockSpec((1, T, CONV_CH), tok),
                  pl.BlockSpec((1, T, LANES), tok),
                  pl.BlockSpec((1, T, SSD_INNER), tok),
                  pl.BlockSpec((1, T, SSD_INNER), tok),
                  pl.BlockSpec((1, SSD_INNER), const),
                  pl.BlockSpec((1, SSD_INNER), const)],
        out_specs=pl.BlockSpec((1, T, SSD_INNER), tok),
        out_shape=jax.ShapeDtypeStruct((B, L, SSD_INNER), bf16),
        scratch_shapes=[pltpu.VMEM((SSD_GROUPS, SSD_STATE, GROUP_W), f32)],
        compiler_params=pltpu.CompilerParams(dimension_semantics=("arbitrary", "arbitrary"),
                                             vmem_limit_bytes=VMEM_LIMIT),
        name="ssd_fwd",
    )(xo, dtp, yb, z, d_skip_x, g_ssd)


def _na_bias_tables(rpb):
    variants = [(0, 0, 0), (2, 0, 0), (4, 0, 1), (6, 2, 2), (8, 2, 2)]
    kc = np.arange(GRID_W)[:, None]
    w = np.arange(GRID_W)[None, :]
    cs = np.clip(w - NA_COLS // 2, 0, GRID_W - NA_COLS)
    col_ok = (kc >= cs) & (kc < cs + NA_COLS)
    taps = np.arange(2 * NA_COLS - 1)[:, None, None]
    onehot = (((kc - w + NA_COLS - 1)[None] == taps) & col_ok[None]).astype(np.float32)
    base = jnp.einsum('hrk,kcw->hrcw', rpb.astype(f32) * LOG2E, onehot, precision=lax.Precision.HIGHEST)
    base = jnp.where(col_ok, base, NEG_BIG).reshape(HEAD_PAIRS, 2, 2 * NA_ROWS - 1, GRID_W, GRID_W)
    basep = jnp.pad(base, ((0, 0), (0, 0), (2, 2), (0, 0), (0, 0)), constant_values=NEG_BIG)
    n_blk = 2 * NA_ROWS + 2
    blk = jnp.concatenate([basep[:, 0, 1:1 + n_blk], basep[:, 0, 0:n_blk],
                           basep[:, 1, 1:1 + n_blk], basep[:, 1, 0:n_blk]], axis=-1)
    kr = np.arange(NA_WIN)[:, None] // GRID_W
    par = (np.arange(2 * LANES)[None, :] // GRID_W) % 2
    tabs = []
    for roff, st0, st1 in variants:
        st = np.where(par == 0, st0, st1)
        valid = (kr >= st) & (kr < st + NA_ROWS)
        j0 = NA_ROWS - roff
        rows = blk[:, j0:j0 + NA_WIN_ROWS].reshape(HEAD_PAIRS, NA_WIN, 2 * LANES)
        tabs.append(jnp.where(valid, rows, NEG_BIG))
    return jnp.stack(tabs)


def _na_kernel(q_ref, k_ref, vt_ref, tab_ref, o_ref, *, grid_rows):
    step = pl.program_id(2)
    me = _even_lane_mask()

    def window(i):
        r = 2 * (step * NA_PAIRS_PER_STEP + i)
        u0 = jnp.clip(r - NA_ROWS // 2, 0, grid_rows - NA_WIN_ROWS)
        var = jnp.where(r < 4, r // 2, jnp.where(r >= grid_rows - 4, 3 + (r - (grid_rows - 4)) // 2, 2))
        return u0, var

    def scores(i):
        u0, var = window(i)
        q2 = q_ref[0, i * LANES:(i + 1) * LANES, :]
        zero = jnp.zeros_like(q2)
        qm = jnp.concatenate([jnp.where(me, q2, zero), jnp.where(me, zero, q2)], axis=0)
        half = NA_WIN // 2
        parts = []
        for hk in range(2):
            kk = k_ref[0, pl.ds(pl.multiple_of(u0 * GRID_W + hk * half, GRID_W), half), :]
            parts.append(_dot_nt(kk, qm) + tab_ref[var, 0, hk * half:(hk + 1) * half, :])
        return jnp.concatenate(parts, axis=0)

    def finish(i, parts, rl):
        oc = jnp.concatenate([parts[e] * rl[:, e * LANES:(e + 1) * LANES] for e in range(2)], axis=0)
        o_ref[0, i * LANES:(i + 1) * LANES, :] = oc.T.astype(bf16)

    s = scores(0)
    pending = None
    for i in range(NA_PAIRS_PER_STEP):
        s_next = scores(i + 1) if i + 1 < NA_PAIRS_PER_STEP else None
        u0, _ = window(i)
        m = jnp.max(s, axis=0, keepdims=True)
        p = jnp.exp2(s - m)
        l = jnp.sum(p, axis=0, keepdims=True)
        j0 = u0 // 2
        vtw = jnp.concatenate([vt_ref[0, 0, j0 + j] for j in range(NA_WIN // LANES)], axis=1)
        pb = p.astype(bf16)
        parts = [_dot(vtw[e * HEAD_DIM:(e + 1) * HEAD_DIM, :], pb[:, e * LANES:(e + 1) * LANES])
                 for e in range(2)]
        if pending is not None:
            finish(*pending)
        pending = (i, parts, 1.0 / l)
        s = s_next
    finish(*pending)


def _na(q, k, vt, tab):
    B, L, _ = q.shape
    grid_rows = L // GRID_W
    QT = NA_PAIRS_PER_STEP * LANES
    nv = tab.shape[0]
    return pl.pallas_call(
        functools.partial(_na_kernel, grid_rows=grid_rows),
        grid=(B, HEAD_PAIRS, L // QT),
        in_specs=[pl.BlockSpec((1, QT, LANES), lambda b, p, s: (b, s, p)),
                  pl.BlockSpec((1, L, LANES), lambda b, p, s: (b, 0, p)),
                  pl.BlockSpec((1, 1, L // LANES, LANES, LANES), lambda b, p, s: (b, p, 0, 0, 0)),
                  pl.BlockSpec((nv, 1, NA_WIN, 2 * LANES), lambda b, p, s: (0, p, 0, 0))],
        out_specs=pl.BlockSpec((1, QT, LANES), lambda b, p, s: (b, s, p)),
        out_shape=jax.ShapeDtypeStruct((B, L, NA_INNER), bf16),
        compiler_params=pltpu.CompilerParams(dimension_semantics=("arbitrary", "arbitrary", "arbitrary"),
                                             vmem_limit_bytes=VMEM_LIMIT),
        name="na",
    )(q, k, vt, tab)


def _out_kernel(x_ref, ys_ref, yn_ref, mod_ref, wo_ref, gf_ref, wg_ref, wu_ref, wd_ref, o_ref, a_ref):
    mix = _dot(ys_ref[0], wo_ref[0:SSD_INNER, :]) + _dot(yn_ref[0], wo_ref[SSD_INNER:, :])
    x1 = x_ref[0] + mod_ref[0, 2:3, :] * mix
    ms = jnp.mean(x1 * x1, axis=-1, keepdims=True)
    h2 = x1 * lax.rsqrt(ms + EPS) * gf_ref[...]
    h2 = (h2 * (1.0 + mod_ref[0, 4:5, :]) + mod_ref[0, 3:4, :]).astype(bf16)
    for c in range(D_FF // FF_BLOCK):
        cols = slice(c * FF_BLOCK, (c + 1) * FF_BLOCK)
        gate = _dot(h2, wg_ref[:, cols])
        up = _dot(h2, wu_ref[:, cols])
        a_ref[:, cols] = (_silu(gate) * up).astype(bf16)
    ff = _dot(a_ref[...], wd_ref[...])
    o_ref[0] = x1 + mod_ref[0, 5:6, :] * ff


def _out_ffn(x, y_ssd, y_na, mod, w_out, g_ffn, w_gate, w_up, w_down):
    B, L, _ = x.shape
    T = TOKEN_TILE
    tok = lambda b, i: (b, i, 0)
    const = lambda b, i: (0, 0)
    single = pl.Buffered(1)
    return pl.pallas_call(
        _out_kernel,
        grid=(B, L // T),
        in_specs=[pl.BlockSpec((1, T, D_MODEL), tok),
                  pl.BlockSpec((1, T, SSD_INNER), tok),
                  pl.BlockSpec((1, T, NA_INNER), tok),
                  pl.BlockSpec((1, 8, D_MODEL), lambda b, i: (b, 0, 0)),
                  pl.BlockSpec((D_MODEL, D_MODEL), const, pipeline_mode=single),
                  pl.BlockSpec((1, D_MODEL), const),
                  pl.BlockSpec((D_MODEL, D_FF), const, pipeline_mode=single),
                  pl.BlockSpec((D_MODEL, D_FF), const, pipeline_mode=single),
                  pl.BlockSpec((D_FF, D_MODEL), const, pipeline_mode=single)],
        out_specs=pl.BlockSpec((1, T, D_MODEL), tok),
        out_shape=jax.ShapeDtypeStruct((B, L, D_MODEL), f32),
        scratch_shapes=[pltpu.VMEM((T, D_FF), bf16)],
        compiler_params=pltpu.CompilerParams(dimension_semantics=("arbitrary", "arbitrary"),
                                             vmem_limit_bytes=VMEM_LIMIT),
        name="out_ffn",
    )(x, y_ssd, y_na, mod, w_out, g_ffn, w_gate, w_up, w_down)


def _pad_lanes(parts):
    row = jnp.zeros((LANES,), f32)
    for off, v in parts:
        row = row.at[off:off + v.shape[0]].set(v.astype(f32))
    return row.reshape(1, LANES)


def kernel(x_prompt, x_sample, c_prompt, c_sample, w_ada, b_ada, g_mix, w_in, conv_w, conv_b, dt_bias_fwd, dt_bias_bwd, a_log_fwd, a_log_bwd, d_skip, g_ssd, g_q, g_k, rpb, w_out, g_ffn, w_gate, w_up, w_down):
    l = 0
    nb_p = x_prompt.shape[0]
    nb_s = x_sample.shape[0]

    c_all = jnp.zeros((8, D_MODEL), f32).at[:nb_p].set(c_prompt).at[nb_p:nb_p + nb_s].set(c_sample)
    mod = _modulation(c_all, w_ada[l], b_ada[l]).reshape(8, 6, D_MODEL)
    mod = jnp.pad(mod, ((0, 0), (0, 2), (0, 0)))

    o1 = SSD_INNER
    o2 = o1 + CONV_CH
    o3 = o2 + 2 * SSD_HEADS
    o4 = o3 + NA_INNER
    o5 = o4 + NA_INNER
    wi = w_in[l]
    w_main = jnp.concatenate([wi[:, :o2], wi[:, o3:o5]], axis=1).astype(bf16)
    w_dt16 = wi[:, o2:o3]
    w_dt = jnp.zeros((D_MODEL, LANES), f32).at[:, 0:16].set(w_dt16).at[:, 64:80].set(w_dt16).astype(bf16)
    w_vt = wi[:, o5:].T.astype(bf16)
    dtb = _pad_lanes([(0, dt_bias_fwd[l]), (8, dt_bias_bwd[l]), (64, dt_bias_fwd[l]), (72, dt_bias_bwd[l])])
    alog = _pad_lanes([(0, a_log_fwd[l]), (8, a_log_bwd[l])])
    gq = jnp.tile(g_q[l].astype(f32) * (HEAD_DIM ** -0.5 * LOG2E), NA_HEADS).reshape(1, NA_INNER)
    gk = jnp.tile(g_k[l].astype(f32), NA_HEADS).reshape(1, NA_INNER)
    dsk = jnp.repeat(d_skip[l].astype(f32), HEAD_DIM).reshape(1, SSD_INNER)
    tab = _na_bias_tables(rpb[l])
    wo = w_out[l].astype(bf16)
    wg = w_gate[l].astype(bf16)
    wu = w_up[l].astype(bf16)
    wd = w_down[l].astype(bf16)
    row = lambda v: v.astype(f32).reshape(1, -1)

    def trunk(x, m):
        z, xo, dtp, q, k, vt = _in_proj(x, m, row(g_mix[l]), w_main, w_dt, w_vt, gq, gk, dtb, alog,
                                        conv_w[l].astype(f32), row(conv_b[l]))
        yb = _ssd_bwd(xo, dtp)
        y_ssd = _ssd_fwd(xo, dtp, yb, z, dsk, row(g_ssd[l]))
        y_na = _na(q, k, vt, tab)
        return _out_ffn(x, y_ssd, y_na, m, wo, row(g_ffn[l]), wg, wu, wd)

    return (trunk(x_prompt, mod[:nb_p]), trunk(x_sample, mod[nb_p:nb_p + nb_s]))
```

```python
import functools
import math

import numpy as np
import jax
import jax.numpy as jnp
from jax import lax
from jax.experimental import pallas as pl
from jax.experimental.pallas import tpu as pltpu

D_MODEL = 1024
HEAD_DIM = 64
SSD_HEADS = 8
SSD_INNER = SSD_HEADS * HEAD_DIM
SSD_GROUPS = 2
SSD_STATE = 128
CONV_K = 5
CHUNK = 128
NA_HEADS = 8
NA_INNER = NA_HEADS * HEAD_DIM
NA_ROWS = 8
NA_COLS = 16
GRID_W = 64
CONV_CH = SSD_INNER + 2 * SSD_GROUPS * SSD_STATE
D_FF = 2816
EPS = 1e-6

LANES = 128
HALO = 8
CONV_ROWS = 128
CONV_COLS = 256
GROUP_W = (SSD_HEADS // SSD_GROUPS) * HEAD_DIM
HEAD_PAIRS = NA_HEADS // 2
LOG2E = math.log2(math.e)
NEG_BIG = -1e30

TOKEN_TILE = 512
SSD_TILE = 512
SSD_UNROLL = 2
NA_WIN_ROWS = 10
NA_WIN = NA_WIN_ROWS * GRID_W
NA_PAIRS_PER_STEP = 16
FF_BLOCK = 256
VMEM_LIMIT = 56 * 1024 * 1024

f32 = jnp.float32
bf16 = jnp.bfloat16


def _dot(a, b):
    return jnp.dot(a, b, preferred_element_type=f32)


def _dot_nt(a, b):
    return lax.dot_general(a, b, (((1,), (1,)), ((), ())), preferred_element_type=f32)


def _split3(v):
    p1 = v.astype(bf16)
    r1 = v - p1.astype(f32)
    p2 = r1.astype(bf16)
    p3 = (r1 - p2.astype(f32)).astype(bf16)
    return p1, p2, p3


def _even_lane_mask():
    return lax.broadcasted_iota(jnp.int32, (1, LANES), 1) < HEAD_DIM


def _silu(v):
    return v * jax.nn.sigmoid(v)


def _mod_kernel(c_ref, w_ref, b_ref, o_ref):
    s = _silu(c_ref[...])
    o_ref[...] = jnp.dot(s, w_ref[...], precision=lax.Precision.HIGHEST,
                         preferred_element_type=f32) + b_ref[...]


def _modulation(c_all, w_ada, b_ada):
    n = 6 * D_MODEL
    return pl.pallas_call(
        _mod_kernel,
        grid=(n // D_MODEL,),
        in_specs=[pl.BlockSpec((8, D_MODEL), lambda j: (0, 0)),
                  pl.BlockSpec((D_MODEL, D_MODEL), lambda j: (0, j)),
                  pl.BlockSpec((1, D_MODEL), lambda j: (0, j))],
        out_specs=pl.BlockSpec((8, D_MODEL), lambda j: (0, j)),
        out_shape=jax.ShapeDtypeStruct((8, n), f32),
        name="mod",
    )(c_all, w_ada, b_ada.reshape(1, n))


def _head_rmsnorm(v, gain):
    me = _even_lane_mask()
    s = v * v
    se = jnp.sum(jnp.where(me, s, 0.0), axis=-1, keepdims=True)
    so = jnp.sum(jnp.where(me, 0.0, s), axis=-1, keepdims=True)
    ms = jnp.where(me, se, so) * (1.0 / HEAD_DIM)
    return v * lax.rsqrt(ms + EPS) * gain


def _in_kernel(x_ref, xp_ref, xn_ref, mod_ref, g_ref, w_ref, wdt_ref, wvt_ref, gq_ref, gk_ref, dtb_ref,
               alog_ref, cw_ref, cb_ref, z_ref, xbc_ref, dtp_ref, q_ref, k_ref, vt_ref, win_ref):
    i = pl.program_id(1)
    T = TOKEN_TILE

    def norm_mod(x):
        ms = jnp.mean(x * x, axis=-1, keepdims=True)
        y = x * lax.rsqrt(ms + EPS) * g_ref[...]
        return y * (1.0 + mod_ref[0, 1:2, :]) + mod_ref[0, 0:1, :]

    hf = norm_mod(x_ref[0])
    h = hf.astype(bf16)

    o = SSD_INNER
    h_ext = jnp.concatenate([norm_mod(xp_ref[0]), hf, norm_mod(xn_ref[0])], axis=0).astype(bf16)
    win_ref[...] = _dot(h_ext, w_ref[:, o:o + CONV_CH])
    win_ref[0:HALO, :] = jnp.where(i > 0, win_ref[0:HALO, :], 0.0)
    win_ref[T + HALO:T + 2 * HALO, :] = jnp.where(i < pl.num_programs(1) - 1, win_ref[T + HALO:T + 2 * HALO, :], 0.0)
    def conv_rows(c):
        for cb in range(CONV_CH // CONV_COLS):
            cols = slice(cb * CONV_COLS, (cb + 1) * CONV_COLS)
            acc = jnp.broadcast_to(cb_ref[:, cols], (CONV_ROWS, CONV_COLS))
            for k in range(CONV_K):
                start = HALO - CONV_K // 2 + k + c * CONV_ROWS
                acc = acc + win_ref[start:start + CONV_ROWS, cols] * cw_ref[k:k + 1, cols]
            xbc_ref[0, c * CONV_ROWS:(c + 1) * CONV_ROWS, cols] = _silu(acc).astype(bf16)

    def proj_z():
        z_ref[0] = _dot(h, w_ref[:, 0:SSD_INNER]).astype(bf16)

    def proj_head_normed(dst_ref, gain_ref, col0):
        full = _dot(h, w_ref[:, col0:col0 + NA_INNER])
        for j in range(HEAD_PAIRS):
            sl = slice(j * LANES, (j + 1) * LANES)
            dst_ref[0, j] = _head_rmsnorm(full[:, sl], gain_ref[:, sl]).astype(bf16)

    def proj_v_dt():
        vt = _dot_nt(wvt_ref[...], h).astype(bf16)
        for p in range(HEAD_PAIRS):
            for tb in range(TOKEN_TILE // LANES):
                vt_ref[0, p, tb] = vt[p * LANES:(p + 1) * LANES, tb * LANES:(tb + 1) * LANES]
        raw = _dot(h, wdt_ref[...]) + dtb_ref[...]
        dt = jnp.maximum(raw, 0.0) + jnp.log1p(jnp.exp(-jnp.abs(raw)))
        lane = lax.broadcasted_iota(jnp.int32, (1, LANES), 1)
        a_neg = -jnp.exp(alog_ref[...])
        mul = jnp.where(lane < 2 * SSD_HEADS, a_neg,
                        jnp.where((lane >= 64) & (lane < 64 + 2 * SSD_HEADS), 1.0, 0.0))
        dtp_ref[0] = dt * mul

    o += CONV_CH
    stages = [proj_z, functools.partial(proj_head_normed, q_ref, gq_ref, o),
              functools.partial(proj_head_normed, k_ref, gk_ref, o + NA_INNER), proj_v_dt]
    for c in range(T // CONV_ROWS):
        stages[c]()
        conv_rows(c)


def _in_proj(x, mod, g_mix, w_main, w_dt, w_vt, gq, gk, dt_bias, a_log, conv_w, conv_b):
    B, L, _ = x.shape
    T = TOKEN_TILE
    nw = w_main.shape[1]
    hb = T // HALO
    const = lambda b, i: (0, 0)
    tok = lambda b, i: (b, i, 0)
    prev = lambda b, i: (b, jnp.maximum(i * hb - 1, 0), 0)
    nxt = lambda b, i: (b, jnp.minimum((i + 1) * hb, L // HALO - 1), 0)
    return pl.pallas_call(
        _in_kernel,
        grid=(B, L // T),
        in_specs=[pl.BlockSpec((1, T, D_MODEL), tok),
                  pl.BlockSpec((1, HALO, D_MODEL), prev),
                  pl.BlockSpec((1, HALO, D_MODEL), nxt),
                  pl.BlockSpec((1, 8, D_MODEL), lambda b, i: (b, 0, 0)),
                  pl.BlockSpec((1, D_MODEL), const),
                  pl.BlockSpec((D_MODEL, nw), const),
                  pl.BlockSpec((D_MODEL, LANES), const),
                  pl.BlockSpec((NA_INNER, D_MODEL), const),
                  pl.BlockSpec((1, NA_INNER), const),
                  pl.BlockSpec((1, NA_INNER), const),
                  pl.BlockSpec((1, LANES), const),
                  pl.BlockSpec((1, LANES), const),
                  pl.BlockSpec((CONV_K, CONV_CH), const),
                  pl.BlockSpec((1, CONV_CH), const)],
        out_specs=[pl.BlockSpec((1, T, SSD_INNER), tok),
                   pl.BlockSpec((1, T, CONV_CH), tok),
                   pl.BlockSpec((1, T, LANES), tok),
                   pl.BlockSpec((1, HEAD_PAIRS, T, LANES), lambda b, i: (b, 0, i, 0)),
                   pl.BlockSpec((1, HEAD_PAIRS, T, LANES), lambda b, i: (b, 0, i, 0)),
                   pl.BlockSpec((1, HEAD_PAIRS, T // LANES, LANES, LANES), lambda b, i: (b, 0, i, 0, 0))],
        scratch_shapes=[pltpu.VMEM((T + 2 * HALO, CONV_CH), f32)],
        out_shape=[jax.ShapeDtypeStruct((B, L, SSD_INNER), bf16),
                   jax.ShapeDtypeStruct((B, L, CONV_CH), bf16),
                   jax.ShapeDtypeStruct((B, L, LANES), f32),
                   jax.ShapeDtypeStruct((B, HEAD_PAIRS, L, LANES), bf16),
                   jax.ShapeDtypeStruct((B, HEAD_PAIRS, L, LANES), bf16),
                   jax.ShapeDtypeStruct((B, HEAD_PAIRS, L // LANES, LANES, LANES), bf16)],
        compiler_params=pltpu.CompilerParams(dimension_semantics=("arbitrary", "arbitrary"),
                                             vmem_limit_bytes=VMEM_LIMIT),
        name="in_proj",
    )(x, x, x, mod, g_mix, w_main, w_dt, w_vt, gq, gk, dt_bias, a_log, conv_w, conv_b)


def _head_selector(base):
    r = lax.broadcasted_iota(jnp.int32, (LANES, SSD_INNER), 0)
    c = lax.broadcasted_iota(jnp.int32, (LANES, SSD_INNER), 1)
    return (lax.shift_right_logical(c, 6) == r - base).astype(bf16)


def _expand_heads(v, sel):
    hi = v.astype(bf16)
    lo = (v - hi.astype(f32)).astype(bf16)
    return _dot(hi, sel) + _dot(lo, sel)


def _cumsum_matrix():
    row = lax.broadcasted_iota(jnp.int32, (CHUNK, CHUNK), 0)
    col = lax.broadcasted_iota(jnp.int32, (CHUNK, CHUNK), 1)
    return (col <= row).astype(bf16)


def _chunk_decays(p, tri):
    p1, p2, p3 = _split3(p)
    incl = _dot(tri, p1) + _dot(tri, p2) + _dot(tri, p3)
    excl = incl - p
    dts = pltpu.roll(p, 64, 1)
    tot = incl[CHUNK - 1:CHUNK, :]
    is_fwd = lax.broadcasted_iota(jnp.int32, (1, LANES), 1) < SSD_HEADS
    w = jnp.exp(jnp.where(is_fwd, tot - incl, excl)) * dts
    sc = jnp.exp(jnp.where(is_fwd, incl, tot - excl))
    return incl, excl, dts, w, sc, jnp.exp(tot)


def _b_transposed(xo_ref, rows, g):
    b_g = xo_ref[0, rows, SSD_INNER + g * SSD_STATE:SSD_INNER + (g + 1) * SSD_STATE].astype(f32)
    return b_g.T.astype(bf16)


def _state_step(xo_ref, rows, st_ref, w, sc, dec, sel, b_t):
    w_x = _expand_heads(w, sel)
    sc_x = _expand_heads(sc, sel)
    dec_x = _expand_heads(jnp.broadcast_to(dec, (8, LANES)), sel)[0:1]
    outs = []
    for g in range(SSD_GROUPS):
        gcols = slice(g * GROUP_W, (g + 1) * GROUP_W)
        xs_g = xo_ref[0, rows, g * GROUP_W:(g + 1) * GROUP_W].astype(f32)
        c_g = xo_ref[0, rows, SSD_INNER + (SSD_GROUPS + g) * SSD_STATE:
                     SSD_INNER + (SSD_GROUPS + g + 1) * SSD_STATE]
        s_g = st_ref[g]
        y_off = _dot(c_g, s_g.astype(bf16)) * sc_x[:, gcols]
        xw = (xs_g * w_x[:, gcols]).astype(bf16)
        st_ref[g] = s_g * dec_x[:, gcols] + _dot(b_t[g], xw)
        outs.append(y_off)
    return jnp.concatenate(outs, axis=1)


def _ssd_bwd_kernel(xo_ref, dtp_ref, yb_ref, st_ref):
    @pl.when(pl.program_id(1) == 0)
    def _():
        st_ref[...] = jnp.zeros_like(st_ref)

    sel = _head_selector(SSD_HEADS)
    tri = _cumsum_matrix()

    def body(j, carry):
        c = SSD_TILE // CHUNK - 1 - j
        rows = pl.ds(pl.multiple_of(c * CHUNK, CHUNK), CHUNK)
        _, _, _, w, sc, dec = _chunk_decays(dtp_ref[0, rows, :], tri)
        b_t = [_b_transposed(xo_ref, rows, g) for g in range(SSD_GROUPS)]
        yb_ref[0, rows, :] = _state_step(xo_ref, rows, st_ref, w, sc, dec, sel, b_t).astype(bf16)
        return carry

    lax.fori_loop(0, SSD_TILE // CHUNK, body, 0, unroll=SSD_UNROLL)


def _ssd_bwd(xo, dtp):
    B, L, _ = xo.shape
    T = SSD_TILE
    nT = L // T
    cur = lambda b, i: (b, nT - 1 - i, 0)
    return pl.pallas_call(
        _ssd_bwd_kernel,
        grid=(B, nT),
        in_specs=[pl.BlockSpec((1, T, CONV_CH), cur),
                  pl.BlockSpec((1, T, LANES), cur)],
        out_specs=pl.BlockSpec((1, T, SSD_INNER), cur),
        out_shape=jax.ShapeDtypeStruct((B, L, SSD_INNER), bf16),
        scratch_shapes=[pltpu.VMEM((SSD_GROUPS, SSD_STATE, GROUP_W), f32)],
        compiler_params=pltpu.CompilerParams(dimension_semantics=("arbitrary", "arbitrary"),
                                             vmem_limit_bytes=VMEM_LIMIT),
        name="ssd_bwd",
    )(xo, dtp)


def _ssd_fwd_kernel(xo_ref, dtp_ref, yb_ref, dsk_ref, o_ref, st_ref):
    @pl.when(pl.program_id(1) == 0)
    def _():
        st_ref[...] = jnp.zeros_like(st_ref)

    row = lax.broadcasted_iota(jnp.int32, (CHUNK, CHUNK), 0)
    col = lax.broadcasted_iota(jnp.int32, (CHUNK, CHUNK), 1)
    low = col < row
    diag = col == row
    me = _even_lane_mask()
    sel = _head_selector(0)
    tri = _cumsum_matrix()

    def body(c, carry):
        rows = pl.ds(pl.multiple_of(c * CHUNK, CHUNK), CHUNK)
        p = dtp_ref[0, rows, :]
        incl, excl, dts, w, sc, dec = _chunk_decays(p, tri)
        b_t = [_b_transposed(xo_ref, rows, g) for g in range(SSD_GROUPS)]
        y = _state_step(xo_ref, rows, st_ref, w, sc, dec, sel, b_t)

        incl_t = incl.T
        excl_t = excl.T
        dts_t = dts.T
        y_diag = []
        for g in range(SSD_GROUPS):
            c_g = xo_ref[0, rows, SSD_INNER + (SSD_GROUPS + g) * SSD_STATE:
                         SSD_INNER + (SSD_GROUPS + g + 1) * SSD_STATE]
            cb = _dot(c_g, b_t[g])
            for pr in range(2):
                xs_pair = xo_ref[0, rows, g * GROUP_W + pr * LANES:g * GROUP_W + (pr + 1) * LANES]
                acc = None
                for e in range(2):
                    hd = 4 * g + 2 * pr + e
                    af_c = jnp.broadcast_to(incl[:, hd:hd + 1], (CHUNK, CHUNK))
                    eb_c = jnp.broadcast_to(excl[:, SSD_HEADS + hd:SSD_HEADS + hd + 1], (CHUNK, CHUNK))
                    af_r = incl_t[hd:hd + 1, :]
                    eb_r = excl_t[SSD_HEADS + hd:SSD_HEADS + hd + 1, :]
                    df_r = dts_t[hd:hd + 1, :]
                    db_r = dts_t[SSD_HEADS + hd:SSD_HEADS + hd + 1, :]
                    arg = jnp.where(low, af_c - af_r, eb_r - eb_c)
                    coef = jnp.where(low, df_r, jnp.where(diag, df_r + db_r, db_r))
                    wm = (cb * jnp.exp(arg) * coef).astype(bf16)
                    xm = jnp.where(me, xs_pair, jnp.zeros_like(xs_pair)) if e == 0 else \
                        jnp.where(me, jnp.zeros_like(xs_pair), xs_pair)
                    d = _dot(wm, xm)
                    acc = d if acc is None else acc + d
                y_diag.append(acc)
        xs = xo_ref[0, rows, 0:SSD_INNER].astype(f32)
        y = y + jnp.concatenate(y_diag, axis=1) + yb_ref[0, rows, :].astype(f32) + xs * dsk_ref[...]
        o_ref[0, rows, :] = y.astype(bf16)
        return carry

    lax.fori_loop(0, SSD_TILE // CHUNK, body, 0, unroll=SSD_UNROLL)


def _ssd_fwd(xo, dtp, yb, d_skip_x):
    B, L, _ = xo.shape
    T = SSD_TILE
    tok = lambda b, i: (b, i, 0)
    const = lambda b, i: (0, 0)
    return pl.pallas_call(
        _ssd_fwd_kernel,
        grid=(B, L // T),
        in_specs=[pl.BlockSpec((1, T, CONV_CH), tok),
                  pl.BlockSpec((1, T, LANES), tok),
                  pl.BlockSpec((1, T, SSD_INNER), tok),
                  pl.BlockSpec((1, SSD_INNER), const)],
        out_specs=pl.BlockSpec((1, T, SSD_INNER), tok),
        out_shape=jax.ShapeDtypeStruct((B, L, SSD_INNER), bf16),
        scratch_shapes=[pltpu.VMEM((SSD_GROUPS, SSD_STATE, GROUP_W), f32)],
        compiler_params=pltpu.CompilerParams(dimension_semantics=("arbitrary", "arbitrary"),
                                             vmem_limit_bytes=VMEM_LIMIT),
        name="ssd_fwd",
    )(xo, dtp, yb, d_skip_x)


def _na_bias_tables(rpb):
    variants = [(0, 0, 0), (2, 0, 0), (4, 0, 1), (6, 2, 2), (8, 2, 2)]
    kc = np.arange(GRID_W)[:, None]
    w = np.arange(GRID_W)[None, :]
    cs = np.clip(w - NA_COLS // 2, 0, GRID_W - NA_COLS)
    col_ok = (kc >= cs) & (kc < cs + NA_COLS)
    taps = np.arange(2 * NA_COLS - 1)[:, None, None]
    onehot = (((kc - w + NA_COLS - 1)[None] == taps) & col_ok[None]).astype(np.float32)
    base = jnp.einsum('hrk,kcw->hrcw', rpb.astype(f32) * LOG2E, onehot, precision=lax.Precision.HIGHEST)
    base = jnp.where(col_ok, base, NEG_BIG).reshape(HEAD_PAIRS, 2, 2 * NA_ROWS - 1, GRID_W, GRID_W)
    basep = jnp.pad(base, ((0, 0), (0, 0), (2, 2), (0, 0), (0, 0)), constant_values=NEG_BIG)
    n_blk = 2 * NA_ROWS + 2
    blk = jnp.concatenate([basep[:, 0, 1:1 + n_blk], basep[:, 0, 0:n_blk],
                           basep[:, 1, 1:1 + n_blk], basep[:, 1, 0:n_blk]], axis=-1)
    kr = np.arange(NA_WIN)[:, None] // GRID_W
    par = (np.arange(2 * LANES)[None, :] // GRID_W) % 2
    tabs = []
    for roff, st0, st1 in variants:
        st = np.where(par == 0, st0, st1)
        valid = (kr >= st) & (kr < st + NA_ROWS)
        j0 = NA_ROWS - roff
        rows = blk[:, j0:j0 + NA_WIN_ROWS].reshape(HEAD_PAIRS, NA_WIN, 2 * LANES)
        tabs.append(jnp.where(valid, rows, NEG_BIG))
    return jnp.stack(tabs)


def _na_kernel(q_ref, k_ref, vt_ref, tab_ref, o_ref, *, grid_rows):
    step = pl.program_id(2)
    me = _even_lane_mask()

    def window(i):
        r = 2 * (step * NA_PAIRS_PER_STEP + i)
        u0 = jnp.clip(r - NA_ROWS // 2, 0, grid_rows - NA_WIN_ROWS)
        var = jnp.where(r < 4, r // 2, jnp.where(r >= grid_rows - 4, 3 + (r - (grid_rows - 4)) // 2, 2))
        return u0, var

    def scores(i):
        u0, var = window(i)
        q2 = q_ref[0, 0, i * LANES:(i + 1) * LANES, :]
        zero = jnp.zeros_like(q2)
        qm = jnp.concatenate([jnp.where(me, q2, zero), jnp.where(me, zero, q2)], axis=0)
        half = NA_WIN // 2
        parts = []
        for hk in range(2):
            kk = k_ref[0, 0, pl.ds(pl.multiple_of(u0 * GRID_W + hk * half, GRID_W), half), :]
            parts.append(_dot_nt(kk, qm) + tab_ref[var, 0, hk * half:(hk + 1) * half, :])
        return jnp.concatenate(parts, axis=0)

    def finish(i, parts, rl):
        oc = jnp.concatenate([parts[e] * rl[:, e * LANES:(e + 1) * LANES] for e in range(2)], axis=0)
        o_ref[0, i * LANES:(i + 1) * LANES, :] = oc.T.astype(bf16)

    s = scores(0)
    pending = None
    for i in range(NA_PAIRS_PER_STEP):
        s_next = scores(i + 1) if i + 1 < NA_PAIRS_PER_STEP else None
        u0, _ = window(i)
        m = jnp.max(s, axis=0, keepdims=True)
        p = jnp.exp2(s - m)
        l = jnp.sum(p, axis=0, keepdims=True)
        j0 = u0 // 2
        vtw = jnp.concatenate([vt_ref[0, 0, j0 + j] for j in range(NA_WIN // LANES)], axis=1)
        pb = p.astype(bf16)
        parts = [_dot(vtw[e * HEAD_DIM:(e + 1) * HEAD_DIM, :], pb[:, e * LANES:(e + 1) * LANES])
                 for e in range(2)]
        if pending is not None:
            finish(*pending)
        pending = (i, parts, 1.0 / l)
        s = s_next
    finish(*pending)


def _na(q, k, vt, tab):
    B, _, L, _ = q.shape
    grid_rows = L // GRID_W
    QT = NA_PAIRS_PER_STEP * LANES
    nv = tab.shape[0]
    return pl.pallas_call(
        functools.partial(_na_kernel, grid_rows=grid_rows),
        grid=(B, HEAD_PAIRS, L // QT),
        in_specs=[pl.BlockSpec((1, 1, QT, LANES), lambda b, p, s: (b, p, s, 0)),
                  pl.BlockSpec((1, 1, L, LANES), lambda b, p, s: (b, p, 0, 0)),
                  pl.BlockSpec((1, 1, L // LANES, LANES, LANES), lambda b, p, s: (b, p, 0, 0, 0)),
                  pl.BlockSpec((nv, 1, NA_WIN, 2 * LANES), lambda b, p, s: (0, p, 0, 0))],
        out_specs=pl.BlockSpec((1, QT, LANES), lambda b, p, s: (b, s, p)),
        out_shape=jax.ShapeDtypeStruct((B, L, NA_INNER), bf16),
        compiler_params=pltpu.CompilerParams(dimension_semantics=("arbitrary", "arbitrary", "arbitrary"),
                                             vmem_limit_bytes=VMEM_LIMIT),
        name="na",
    )(q, k, vt, tab)


def _out_kernel(x_ref, ys_ref, z_ref, yn_ref, mod_ref, gs_ref, wo_ref, gf_ref, wg_ref, wu_ref, wd_ref, o_ref, a_ref):
    gated = ys_ref[0].astype(f32) * _silu(z_ref[0].astype(f32))
    gms = jnp.mean(gated * gated, axis=-1, keepdims=True)
    y_ssd = (gated * lax.rsqrt(gms + EPS) * gs_ref[...]).astype(bf16)
    mix = _dot(y_ssd, wo_ref[0:SSD_INNER, :]) + _dot(yn_ref[0], wo_ref[SSD_INNER:, :])
    x1 = x_ref[0] + mod_ref[0, 2:3, :] * mix
    ms = jnp.mean(x1 * x1, axis=-1, keepdims=True)
    h2 = x1 * lax.rsqrt(ms + EPS) * gf_ref[...]
    h2 = (h2 * (1.0 + mod_ref[0, 4:5, :]) + mod_ref[0, 3:4, :]).astype(bf16)
    for c in range(D_FF // FF_BLOCK):
        cols = slice(c * FF_BLOCK, (c + 1) * FF_BLOCK)
        gate = _dot(h2, wg_ref[:, cols])
        up = _dot(h2, wu_ref[:, cols])
        a_ref[:, cols] = (_silu(gate) * up).astype(bf16)
    ff = _dot(a_ref[...], wd_ref[...])
    o_ref[0] = x1 + mod_ref[0, 5:6, :] * ff


def _out_ffn(x, y_pre, z, y_na, mod, g_ssd, w_out, g_ffn, w_gate, w_up, w_down):
    B, L, _ = x.shape
    T = TOKEN_TILE
    tok = lambda b, i: (b, i, 0)
    const = lambda b, i: (0, 0)
    single = pl.Buffered(1)
    return pl.pallas_call(
        _out_kernel,
        grid=(B, L // T),
        in_specs=[pl.BlockSpec((1, T, D_MODEL), tok),
                  pl.BlockSpec((1, T, SSD_INNER), tok),
                  pl.BlockSpec((1, T, SSD_INNER), tok),
                  pl.BlockSpec((1, T, NA_INNER), tok),
                  pl.BlockSpec((1, 8, D_MODEL), lambda b, i: (b, 0, 0)),
                  pl.BlockSpec((1, SSD_INNER), const),
                  pl.BlockSpec((D_MODEL, D_MODEL), const, pipeline_mode=single),
                  pl.BlockSpec((1, D_MODEL), const),
                  pl.BlockSpec((D_MODEL, D_FF), const, pipeline_mode=single),
                  pl.BlockSpec((D_MODEL, D_FF), const, pipeline_mode=single),
                  pl.BlockSpec((D_FF, D_MODEL), const, pipeline_mode=single)],
        out_specs=pl.BlockSpec((1, T, D_MODEL), tok),
        out_shape=jax.ShapeDtypeStruct((B, L, D_MODEL), f32),
        scratch_shapes=[pltpu.VMEM((T, D_FF), bf16)],
        compiler_params=pltpu.CompilerParams(dimension_semantics=("arbitrary", "arbitrary"),
                                             vmem_limit_bytes=VMEM_LIMIT),
        name="out_ffn",
    )(x, y_pre, z, y_na, mod, g_ssd, w_out, g_ffn, w_gate, w_up, w_down)


def _pad_lanes(parts):
    row = jnp.zeros((LANES,), f32)
    for off, v in parts:
        row = row.at[off:off + v.shape[0]].set(v.astype(f32))
    return row.reshape(1, LANES)


def kernel(x_prompt, x_sample, c_prompt, c_sample, w_ada, b_ada, g_mix, w_in, conv_w, conv_b, dt_bias_fwd, dt_bias_bwd, a_log_fwd, a_log_bwd, d_skip, g_ssd, g_q, g_k, rpb, w_out, g_ffn, w_gate, w_up, w_down):
    l = 0
    nb_p = x_prompt.shape[0]
    nb_s = x_sample.shape[0]

    c_all = jnp.zeros((8, D_MODEL), f32).at[:nb_p].set(c_prompt).at[nb_p:nb_p + nb_s].set(c_sample)
    mod = _modulation(c_all, w_ada[l], b_ada[l]).reshape(8, 6, D_MODEL)
    mod = jnp.pad(mod, ((0, 0), (0, 2), (0, 0)))

    o1 = SSD_INNER
    o2 = o1 + CONV_CH
    o3 = o2 + 2 * SSD_HEADS
    o4 = o3 + NA_INNER
    o5 = o4 + NA_INNER
    wi = w_in[l]
    w_main = jnp.concatenate([wi[:, :o2], wi[:, o3:o5]], axis=1).astype(bf16)
    w_dt16 = wi[:, o2:o3]
    w_dt = jnp.zeros((D_MODEL, LANES), f32).at[:, 0:16].set(w_dt16).at[:, 64:80].set(w_dt16).astype(bf16)
    w_vt = wi[:, o5:].T.astype(bf16)
    dtb = _pad_lanes([(0, dt_bias_fwd[l]), (8, dt_bias_bwd[l]), (64, dt_bias_fwd[l]), (72, dt_bias_bwd[l])])
    alog = _pad_lanes([(0, a_log_fwd[l]), (8, a_log_bwd[l])])
    gq = jnp.tile(g_q[l].astype(f32) * (HEAD_DIM ** -0.5 * LOG2E), NA_HEADS).reshape(1, NA_INNER)
    gk = jnp.tile(g_k[l].astype(f32), NA_HEADS).reshape(1, NA_INNER)
    dsk = jnp.repeat(d_skip[l].astype(f32), HEAD_DIM).reshape(1, SSD_INNER)
    tab = _na_bias_tables(rpb[l])
    wo = w_out[l].astype(bf16)
    wg = w_gate[l].astype(bf16)
    wu = w_up[l].astype(bf16)
    wd = w_down[l].astype(bf16)
    row = lambda v: v.astype(f32).reshape(1, -1)

    def trunk(x, m):
        z, xo, dtp, q, k, vt = _in_proj(x, m, row(g_mix[l]), w_main, w_dt, w_vt, gq, gk, dtb, alog,
                                        conv_w[l].astype(f32), row(conv_b[l]))
        yb = _ssd_bwd(xo, dtp)
        y_pre = _ssd_fwd(xo, dtp, yb, dsk)
        y_na = _na(q, k, vt, tab)
        return _out_ffn(x, y_pre, z, y_na, m, row(g_ssd[l]), wo, row(g_ffn[l]), wg, wu, wd)

    return (trunk(x_prompt, mod[:nb_p]), trunk(x_sample, mod[nb_p:nb_p + nb_s]))
```

```python
import functools
import math

import numpy as np
import jax
import jax.numpy as jnp
from jax import lax
from jax.experimental import pallas as pl
from jax.experimental.pallas import tpu as pltpu

D_MODEL = 1024
HEAD_DIM = 64
SSD_HEADS = 8
SSD_INNER = SSD_HEADS * HEAD_DIM
SSD_GROUPS = 2
SSD_STATE = 128
CONV_K = 5
CHUNK = 128
NA_HEADS = 8
NA_INNER = NA_HEADS * HEAD_DIM
NA_ROWS = 8
NA_COLS = 16
GRID_W = 64
CONV_CH = SSD_INNER + 2 * SSD_GROUPS * SSD_STATE
D_FF = 2816
EPS = 1e-6

LANES = 128
CONV_HALO = 16
CONV_ROWS = 128
CONV_COLS = 256
GROUP_W = (SSD_HEADS // SSD_GROUPS) * HEAD_DIM
HEAD_PAIRS = NA_HEADS // 2
LOG2E = math.log2(math.e)
NEG_BIG = -1e30

TOKEN_TILE = 512
SSD_TILE = 512
SSD_UNROLL = 2
NA_WIN_ROWS = 10
NA_WIN = NA_WIN_ROWS * GRID_W
NA_PAIRS_PER_STEP = 16
NA_QT = NA_PAIRS_PER_STEP * 2 * GRID_W
NA_HALO = 256
FF_BLOCK = 256
VMEM_LIMIT = 56 * 1024 * 1024

f32 = jnp.float32
bf16 = jnp.bfloat16


def _dot(a, b):
    return jnp.dot(a, b, preferred_element_type=f32)


def _dot_nt(a, b):
    return lax.dot_general(a, b, (((1,), (1,)), ((), ())), preferred_element_type=f32)


def _split3(v):
    p1 = v.astype(bf16)
    r1 = v - p1.astype(f32)
    p2 = r1.astype(bf16)
    p3 = (r1 - p2.astype(f32)).astype(bf16)
    return p1, p2, p3


def _even_lane_mask():
    return lax.broadcasted_iota(jnp.int32, (1, LANES), 1) < HEAD_DIM


def _silu(v):
    return v * jax.nn.sigmoid(v)


def _mod_kernel(c_ref, w_ref, b_ref, o_ref):
    s = _silu(c_ref[...])
    o_ref[...] = jnp.dot(s, w_ref[...], precision=lax.Precision.HIGHEST,
                         preferred_element_type=f32) + b_ref[...]


def _modulation(c_all, w_ada, b_ada):
    n = 6 * D_MODEL
    return pl.pallas_call(
        _mod_kernel,
        grid=(n // D_MODEL,),
        in_specs=[pl.BlockSpec((8, D_MODEL), lambda j: (0, 0)),
                  pl.BlockSpec((D_MODEL, D_MODEL), lambda j: (0, j)),
                  pl.BlockSpec((1, D_MODEL), lambda j: (0, j))],
        out_specs=pl.BlockSpec((8, D_MODEL), lambda j: (0, j)),
        out_shape=jax.ShapeDtypeStruct((8, n), f32),
        name="mod",
    )(c_all, w_ada, b_ada.reshape(1, n))


def _head_rmsnorm(v, gain):
    me = _even_lane_mask()
    s = v * v
    se = jnp.sum(jnp.where(me, s, 0.0), axis=-1, keepdims=True)
    so = jnp.sum(jnp.where(me, 0.0, s), axis=-1, keepdims=True)
    ms = jnp.where(me, se, so) * (1.0 / HEAD_DIM)
    return v * lax.rsqrt(ms + EPS) * gain


def _in_kernel(x_ref, mod_ref, g_ref, w_ref, wdt_ref, wvt_ref, gq_ref, gk_ref, dtb_ref, alog_ref,
               z_ref, xbc_ref, dtp_ref, q_ref, k_ref, vt_ref):
    x = x_ref[0]
    ms = jnp.mean(x * x, axis=-1, keepdims=True)
    y = x * lax.rsqrt(ms + EPS) * g_ref[...]
    h = (y * (1.0 + mod_ref[0, 1:2, :]) + mod_ref[0, 0:1, :]).astype(bf16)

    def proj_xbc():
        xbc_ref[0] = _dot(h, w_ref[:, SSD_INNER:SSD_INNER + CONV_CH]).astype(bf16)

    def proj_z():
        z_ref[0] = _dot(h, w_ref[:, 0:SSD_INNER]).astype(bf16)

    def proj_head_normed(dst_ref, gain_ref, col0):
        full = _dot(h, w_ref[:, col0:col0 + NA_INNER])
        for j in range(HEAD_PAIRS):
            sl = slice(j * LANES, (j + 1) * LANES)
            dst_ref[0, j] = _head_rmsnorm(full[:, sl], gain_ref[:, sl]).astype(bf16)

    def proj_v_dt():
        vt = _dot_nt(wvt_ref[...], h).astype(bf16)
        for p in range(HEAD_PAIRS):
            for tb in range(TOKEN_TILE // LANES):
                vt_ref[0, p, tb] = vt[p * LANES:(p + 1) * LANES, tb * LANES:(tb + 1) * LANES]
        raw = _dot(h, wdt_ref[...]) + dtb_ref[...]
        dt = jnp.maximum(raw, 0.0) + jnp.log1p(jnp.exp(-jnp.abs(raw)))
        lane = lax.broadcasted_iota(jnp.int32, (1, LANES), 1)
        a_neg = -jnp.exp(alog_ref[...])
        mul = jnp.where(lane < 2 * SSD_HEADS, a_neg,
                        jnp.where((lane >= 64) & (lane < 64 + 2 * SSD_HEADS), 1.0, 0.0))
        dtp_ref[0] = dt * mul

    o = SSD_INNER + CONV_CH
    proj_z()
    proj_xbc()
    proj_head_normed(q_ref, gq_ref, o)
    proj_head_normed(k_ref, gk_ref, o + NA_INNER)
    proj_v_dt()


def _in_proj(x, mod, g_mix, w_main, w_dt, w_vt, gq, gk, dt_bias, a_log):
    B, L, _ = x.shape
    T = TOKEN_TILE
    nw = w_main.shape[1]
    const = lambda b, i: (0, 0)
    tok = lambda b, i: (b, i, 0)
    return pl.pallas_call(
        _in_kernel,
        grid=(B, L // T),
        in_specs=[pl.BlockSpec((1, T, D_MODEL), tok),
                  pl.BlockSpec((1, 8, D_MODEL), lambda b, i: (b, 0, 0)),
                  pl.BlockSpec((1, D_MODEL), const),
                  pl.BlockSpec((D_MODEL, nw), const),
                  pl.BlockSpec((D_MODEL, LANES), const),
                  pl.BlockSpec((NA_INNER, D_MODEL), const),
                  pl.BlockSpec((1, NA_INNER), const),
                  pl.BlockSpec((1, NA_INNER), const),
                  pl.BlockSpec((1, LANES), const),
                  pl.BlockSpec((1, LANES), const)],
        out_specs=[pl.BlockSpec((1, T, SSD_INNER), tok),
                   pl.BlockSpec((1, T, CONV_CH), tok),
                   pl.BlockSpec((1, T, LANES), tok),
                   pl.BlockSpec((1, HEAD_PAIRS, T, LANES), lambda b, i: (b, 0, i, 0)),
                   pl.BlockSpec((1, HEAD_PAIRS, T, LANES), lambda b, i: (b, 0, i, 0)),
                   pl.BlockSpec((1, HEAD_PAIRS, T // LANES, LANES, LANES), lambda b, i: (b, 0, i, 0, 0))],
        out_shape=[jax.ShapeDtypeStruct((B, L, SSD_INNER), bf16),
                   jax.ShapeDtypeStruct((B, L, CONV_CH), bf16),
                   jax.ShapeDtypeStruct((B, L, LANES), f32),
                   jax.ShapeDtypeStruct((B, HEAD_PAIRS, L, LANES), bf16),
                   jax.ShapeDtypeStruct((B, HEAD_PAIRS, L, LANES), bf16),
                   jax.ShapeDtypeStruct((B, HEAD_PAIRS, L // LANES, LANES, LANES), bf16)],
        compiler_params=pltpu.CompilerParams(dimension_semantics=("arbitrary", "arbitrary"),
                                             vmem_limit_bytes=VMEM_LIMIT),
        name="in_proj",
    )(x, mod, g_mix, w_main, w_dt, w_vt, gq, gk, dt_bias, a_log)


def _head_selector(base):
    r = lax.broadcasted_iota(jnp.int32, (LANES, SSD_INNER), 0)
    c = lax.broadcasted_iota(jnp.int32, (LANES, SSD_INNER), 1)
    return (lax.shift_right_logical(c, 6) == r - base).astype(bf16)


def _expand_heads(v, sel):
    hi = v.astype(bf16)
    lo = (v - hi.astype(f32)).astype(bf16)
    return _dot(hi, sel) + _dot(lo, sel)


def _cumsum_matrix():
    row = lax.broadcasted_iota(jnp.int32, (CHUNK, CHUNK), 0)
    col = lax.broadcasted_iota(jnp.int32, (CHUNK, CHUNK), 1)
    return (col <= row).astype(bf16)


def _chunk_decays(p, tri):
    p1, p2, p3 = _split3(p)
    incl = _dot(tri, p1) + _dot(tri, p2) + _dot(tri, p3)
    excl = incl - p
    dts = pltpu.roll(p, 64, 1)
    tot = incl[CHUNK - 1:CHUNK, :]
    is_fwd = lax.broadcasted_iota(jnp.int32, (1, LANES), 1) < SSD_HEADS
    w = jnp.exp(jnp.where(is_fwd, tot - incl, excl)) * dts
    sc = jnp.exp(jnp.where(is_fwd, incl, tot - excl))
    return incl, excl, dts, w, sc, jnp.exp(tot)


def _b_transposed(xo_ref, rows, g):
    b_g = xo_ref[0, rows, SSD_INNER + g * SSD_STATE:SSD_INNER + (g + 1) * SSD_STATE].astype(f32)
    return b_g.T.astype(bf16)


def _state_step(xo_ref, rows, st_ref, w, sc, dec, sel, b_t):
    w_x = _expand_heads(w, sel)
    sc_x = _expand_heads(sc, sel)
    dec_x = _expand_heads(jnp.broadcast_to(dec, (8, LANES)), sel)[0:1]
    outs = []
    for g in range(SSD_GROUPS):
        gcols = slice(g * GROUP_W, (g + 1) * GROUP_W)
        xs_g = xo_ref[0, rows, g * GROUP_W:(g + 1) * GROUP_W].astype(f32)
        c_g = xo_ref[0, rows, SSD_INNER + (SSD_GROUPS + g) * SSD_STATE:
                     SSD_INNER + (SSD_GROUPS + g + 1) * SSD_STATE]
        s_g = st_ref[g]
        y_off = _dot(c_g, s_g.astype(bf16)) * sc_x[:, gcols]
        xw = (xs_g * w_x[:, gcols]).astype(bf16)
        st_ref[g] = s_g * dec_x[:, gcols] + _dot(b_t[g], xw)
        outs.append(y_off)
    return jnp.concatenate(outs, axis=1)


def _ssd_bwd_kernel(xc_ref, xp_ref, xn_ref, dtp_ref, cw_ref, cb_ref, xo_ref, yb_ref, win_ref, st_ref, *, n_tiles):
    i = pl.program_id(1)
    t = n_tiles - 1 - i

    @pl.when(i == 0)
    def _():
        st_ref[...] = jnp.zeros_like(st_ref)

    H = CONV_HALO
    win_ref[0:H, :] = jnp.where(t > 0, xp_ref[0], jnp.zeros_like(xp_ref[0]))
    win_ref[H:H + SSD_TILE, :] = xc_ref[0]
    win_ref[H + SSD_TILE:, :] = jnp.where(t < n_tiles - 1, xn_ref[0], jnp.zeros_like(xn_ref[0]))
    r_i = lax.broadcasted_iota(jnp.int32, (CONV_ROWS, CONV_ROWS + 2 * H), 0)
    c_i = lax.broadcasted_iota(jnp.int32, (CONV_ROWS, CONV_ROWS + 2 * H), 1)
    mid = CONV_K // 2
    shift = {k: (c_i == r_i + H + k - mid).astype(bf16) for k in range(CONV_K) if k != mid}
    for c in range(SSD_TILE // CONV_ROWS):
        for cb in range(CONV_CH // CONV_COLS):
            cols = slice(cb * CONV_COLS, (cb + 1) * CONV_COLS)
            xw = win_ref[c * CONV_ROWS:(c + 1) * CONV_ROWS + 2 * H, cols]
            centre = win_ref[H + c * CONV_ROWS:H + (c + 1) * CONV_ROWS, cols].astype(f32)
            acc = cb_ref[:, cols] + centre * cw_ref[mid:mid + 1, cols]
            for k in shift:
                acc = acc + _dot(shift[k], xw) * cw_ref[k:k + 1, cols]
            xo_ref[0, c * CONV_ROWS:(c + 1) * CONV_ROWS, cols] = _silu(acc).astype(bf16)

    sel = _head_selector(SSD_HEADS)
    tri = _cumsum_matrix()

    def body(j, carry):
        c = SSD_TILE // CHUNK - 1 - j
        rows = pl.ds(pl.multiple_of(c * CHUNK, CHUNK), CHUNK)
        _, _, _, w, sc, dec = _chunk_decays(dtp_ref[0, rows, :], tri)
        b_t = [_b_transposed(xo_ref, rows, g) for g in range(SSD_GROUPS)]
        yb_ref[0, rows, :] = _state_step(xo_ref, rows, st_ref, w, sc, dec, sel, b_t).astype(bf16)
        return carry

    lax.fori_loop(0, SSD_TILE // CHUNK, body, 0, unroll=SSD_UNROLL)


def _ssd_bwd(xbc, dtp, conv_w, conv_b):
    B, L, _ = xbc.shape
    T = SSD_TILE
    nT = L // T
    hb = T // CONV_HALO
    cur = lambda b, i: (b, nT - 1 - i, 0)
    prev = lambda b, i: (b, jnp.maximum((nT - 1 - i) * hb - 1, 0), 0)
    nxt = lambda b, i: (b, jnp.minimum((nT - i) * hb, L // CONV_HALO - 1), 0)
    const = lambda b, i: (0, 0)
    return pl.pallas_call(
        functools.partial(_ssd_bwd_kernel, n_tiles=nT),
        grid=(B, nT),
        in_specs=[pl.BlockSpec((1, T, CONV_CH), cur),
                  pl.BlockSpec((1, CONV_HALO, CONV_CH), prev),
                  pl.BlockSpec((1, CONV_HALO, CONV_CH), nxt),
                  pl.BlockSpec((1, T, LANES), cur),
                  pl.BlockSpec((CONV_K, CONV_CH), const),
                  pl.BlockSpec((1, CONV_CH), const)],
        out_specs=[pl.BlockSpec((1, T, CONV_CH), cur),
                   pl.BlockSpec((1, T, SSD_INNER), cur)],
        out_shape=[jax.ShapeDtypeStruct((B, L, CONV_CH), bf16),
                   jax.ShapeDtypeStruct((B, L, SSD_INNER), bf16)],
        scratch_shapes=[pltpu.VMEM((T + 2 * CONV_HALO, CONV_CH), bf16),
                        pltpu.VMEM((SSD_GROUPS, SSD_STATE, GROUP_W), f32)],
        compiler_params=pltpu.CompilerParams(dimension_semantics=("arbitrary", "arbitrary"),
                                             vmem_limit_bytes=VMEM_LIMIT),
        name="ssd_bwd",
    )(xbc, xbc, xbc, dtp, conv_w, conv_b)


def _ssd_fwd_kernel(xo_ref, dtp_ref, yb_ref, dsk_ref, o_ref, st_ref):
    @pl.when(pl.program_id(1) == 0)
    def _():
        st_ref[...] = jnp.zeros_like(st_ref)

    row = lax.broadcasted_iota(jnp.int32, (CHUNK, CHUNK), 0)
    col = lax.broadcasted_iota(jnp.int32, (CHUNK, CHUNK), 1)
    low = col < row
    diag = col == row
    me = _even_lane_mask()
    sel = _head_selector(0)
    tri = _cumsum_matrix()

    def body(c, carry):
        rows = pl.ds(pl.multiple_of(c * CHUNK, CHUNK), CHUNK)
        p = dtp_ref[0, rows, :]
        incl, excl, dts, w, sc, dec = _chunk_decays(p, tri)
        b_t = [_b_transposed(xo_ref, rows, g) for g in range(SSD_GROUPS)]
        y = _state_step(xo_ref, rows, st_ref, w, sc, dec, sel, b_t)

        incl_t = incl.T
        excl_t = excl.T
        dts_t = dts.T
        y_diag = []
        for g in range(SSD_GROUPS):
            c_g = xo_ref[0, rows, SSD_INNER + (SSD_GROUPS + g) * SSD_STATE:
                         SSD_INNER + (SSD_GROUPS + g + 1) * SSD_STATE]
            cb = _dot(c_g, b_t[g])
            for pr in range(2):
                xs_pair = xo_ref[0, rows, g * GROUP_W + pr * LANES:g * GROUP_W + (pr + 1) * LANES]
                acc = None
                for e in range(2):
                    hd = 4 * g + 2 * pr + e
                    af_c = jnp.broadcast_to(incl[:, hd:hd + 1], (CHUNK, CHUNK))
                    eb_c = jnp.broadcast_to(excl[:, SSD_HEADS + hd:SSD_HEADS + hd + 1], (CHUNK, CHUNK))
                    af_r = incl_t[hd:hd + 1, :]
                    eb_r = excl_t[SSD_HEADS + hd:SSD_HEADS + hd + 1, :]
                    df_r = dts_t[hd:hd + 1, :]
                    db_r = dts_t[SSD_HEADS + hd:SSD_HEADS + hd + 1, :]
                    arg = jnp.where(low, af_c - af_r, eb_r - eb_c)
                    coef = jnp.where(low, df_r, jnp.where(diag, df_r + db_r, db_r))
                    wm = (cb * jnp.exp(arg) * coef).astype(bf16)
                    xm = jnp.where(me, xs_pair, jnp.zeros_like(xs_pair)) if e == 0 else \
                        jnp.where(me, jnp.zeros_like(xs_pair), xs_pair)
                    d = _dot(wm, xm)
                    acc = d if acc is None else acc + d
                y_diag.append(acc)
        xs = xo_ref[0, rows, 0:SSD_INNER].astype(f32)
        y = y + jnp.concatenate(y_diag, axis=1) + yb_ref[0, rows, :].astype(f32) + xs * dsk_ref[...]
        o_ref[0, rows, :] = y.astype(bf16)
        return carry

    lax.fori_loop(0, SSD_TILE // CHUNK, body, 0, unroll=SSD_UNROLL)


def _ssd_fwd(xo, dtp, yb, d_skip_x):
    B, L, _ = xo.shape
    T = SSD_TILE
    tok = lambda b, i: (b, i, 0)
    const = lambda b, i: (0, 0)
    return pl.pallas_call(
        _ssd_fwd_kernel,
        grid=(B, L // T),
        in_specs=[pl.BlockSpec((1, T, CONV_CH), tok),
                  pl.BlockSpec((1, T, LANES), tok),
                  pl.BlockSpec((1, T, SSD_INNER), tok),
                  pl.BlockSpec((1, SSD_INNER), const)],
        out_specs=pl.BlockSpec((1, T, SSD_INNER), tok),
        out_shape=jax.ShapeDtypeStruct((B, L, SSD_INNER), bf16),
        scratch_shapes=[pltpu.VMEM((SSD_GROUPS, SSD_STATE, GROUP_W), f32)],
        compiler_params=pltpu.CompilerParams(dimension_semantics=("arbitrary", "arbitrary"),
                                             vmem_limit_bytes=VMEM_LIMIT),
        name="ssd_fwd",
    )(xo, dtp, yb, d_skip_x)


def _na_bias_tables(rpb):
    variants = [(0, 0, 0), (2, 0, 0), (4, 0, 1), (6, 2, 2), (8, 2, 2)]
    kc = np.arange(GRID_W)[:, None]
    w = np.arange(GRID_W)[None, :]
    cs = np.clip(w - NA_COLS // 2, 0, GRID_W - NA_COLS)
    col_ok = (kc >= cs) & (kc < cs + NA_COLS)
    taps = np.arange(2 * NA_COLS - 1)[:, None, None]
    onehot = (((kc - w + NA_COLS - 1)[None] == taps) & col_ok[None]).astype(np.float32)
    base = jnp.einsum('hrk,kcw->hrcw', rpb.astype(f32) * LOG2E, onehot, precision=lax.Precision.HIGHEST)
    base = jnp.where(col_ok, base, NEG_BIG).reshape(HEAD_PAIRS, 2, 2 * NA_ROWS - 1, GRID_W, GRID_W)
    basep = jnp.pad(base, ((0, 0), (0, 0), (2, 2), (0, 0), (0, 0)), constant_values=NEG_BIG)
    n_blk = 2 * NA_ROWS + 2
    blk = jnp.concatenate([basep[:, 0, 1:1 + n_blk], basep[:, 0, 0:n_blk],
                           basep[:, 1, 1:1 + n_blk], basep[:, 1, 0:n_blk]], axis=-1)
    kr = np.arange(NA_WIN)[:, None] // GRID_W
    par = (np.arange(2 * LANES)[None, :] // GRID_W) % 2
    tabs = []
    for roff, st0, st1 in variants:
        st = np.where(par == 0, st0, st1)
        valid = (kr >= st) & (kr < st + NA_ROWS)
        j0 = NA_ROWS - roff
        rows = blk[:, j0:j0 + NA_WIN_ROWS].reshape(HEAD_PAIRS, NA_WIN, 2 * LANES)
        tabs.append(jnp.where(valid, rows, NEG_BIG))
    return jnp.stack(tabs)


def _na_kernel(q_ref, kp_ref, kc_ref, kn_ref, vp_ref, vc_ref, vn_ref, tab_ref, o_ref, kwin_ref, vwin_ref,
               *, grid_rows):
    step = pl.program_id(2)
    me = _even_lane_mask()

    kwin_ref[0:NA_HALO, :] = kp_ref[0, 0]
    kwin_ref[NA_HALO:NA_HALO + NA_QT, :] = kc_ref[0, 0]
    kwin_ref[NA_HALO + NA_QT:, :] = kn_ref[0, 0]
    ht = NA_HALO // LANES
    vwin_ref[0:ht] = vp_ref[0, 0]
    vwin_ref[ht:ht + NA_QT // LANES] = vc_ref[0, 0]
    vwin_ref[ht + NA_QT // LANES:] = vn_ref[0, 0]
    tok0 = step * NA_QT - NA_HALO

    def window(i):
        r = 2 * (step * NA_PAIRS_PER_STEP + i)
        u0 = jnp.clip(r - NA_ROWS // 2, 0, grid_rows - NA_WIN_ROWS)
        var = jnp.where(r < 4, r // 2, jnp.where(r >= grid_rows - 4, 3 + (r - (grid_rows - 4)) // 2, 2))
        return u0, var

    def scores(i):
        u0, var = window(i)
        q2 = q_ref[0, 0, i * LANES:(i + 1) * LANES, :]
        zero = jnp.zeros_like(q2)
        qm = jnp.concatenate([jnp.where(me, q2, zero), jnp.where(me, zero, q2)], axis=0)
        half = NA_WIN // 2
        parts = []
        for hk in range(2):
            kk = kwin_ref[pl.ds(pl.multiple_of(u0 * GRID_W - tok0 + hk * half, GRID_W), half), :]
            parts.append(_dot_nt(kk, qm) + tab_ref[var, 0, hk * half:(hk + 1) * half, :])
        return jnp.concatenate(parts, axis=0)

    def finish(i, parts, rl):
        oc = jnp.concatenate([parts[e] * rl[:, e * LANES:(e + 1) * LANES] for e in range(2)], axis=0)
        o_ref[0, i * LANES:(i + 1) * LANES, :] = oc.T.astype(bf16)

    s = scores(0)
    pending = None
    for i in range(NA_PAIRS_PER_STEP):
        s_next = scores(i + 1) if i + 1 < NA_PAIRS_PER_STEP else None
        u0, _ = window(i)
        m = jnp.max(s, axis=0, keepdims=True)
        p = jnp.exp2(s - m)
        l = jnp.sum(p, axis=0, keepdims=True)
        j0 = lax.shift_right_logical(u0 * GRID_W - tok0, 7)
        vtw = jnp.concatenate([vwin_ref[j0 + j] for j in range(NA_WIN // LANES)], axis=1)
        pb = p.astype(bf16)
        parts = [_dot(vtw[e * HEAD_DIM:(e + 1) * HEAD_DIM, :], pb[:, e * LANES:(e + 1) * LANES])
                 for e in range(2)]
        if pending is not None:
            finish(*pending)
        pending = (i, parts, 1.0 / l)
        s = s_next
    finish(*pending)


def _na(q, k, vt, tab):
    B, _, L, _ = q.shape
    grid_rows = L // GRID_W
    QT, H = NA_QT, NA_HALO
    r = QT // H
    nh = L // H
    nv = tab.shape[0]
    cur = lambda p, b, s: (b, p, s, 0)
    prev = lambda p, b, s: (b, p, jnp.maximum(s * r - 1, 0), 0)
    nxt = lambda p, b, s: (b, p, jnp.minimum((s + 1) * r, nh - 1), 0)
    tile = lambda f: (lambda p, b, s: f(p, b, s) + (0,))
    ht = H // LANES
    return pl.pallas_call(
        functools.partial(_na_kernel, grid_rows=grid_rows),
        grid=(HEAD_PAIRS, B, L // QT),
        in_specs=[pl.BlockSpec((1, 1, QT, LANES), cur),
                  pl.BlockSpec((1, 1, H, LANES), prev),
                  pl.BlockSpec((1, 1, QT, LANES), cur),
                  pl.BlockSpec((1, 1, H, LANES), nxt),
                  pl.BlockSpec((1, 1, ht, LANES, LANES), tile(prev)),
                  pl.BlockSpec((1, 1, QT // LANES, LANES, LANES), tile(cur)),
                  pl.BlockSpec((1, 1, ht, LANES, LANES), tile(nxt)),
                  pl.BlockSpec((nv, 1, NA_WIN, 2 * LANES), lambda p, b, s: (0, p, 0, 0))],
        out_specs=pl.BlockSpec((1, QT, LANES), lambda p, b, s: (b, s, p)),
        out_shape=jax.ShapeDtypeStruct((B, L, NA_INNER), bf16),
        scratch_shapes=[pltpu.VMEM((QT + 2 * H, LANES), bf16),
                        pltpu.VMEM((QT // LANES + 2 * ht, LANES, LANES), bf16)],
        compiler_params=pltpu.CompilerParams(dimension_semantics=("arbitrary", "arbitrary", "arbitrary"),
                                             vmem_limit_bytes=VMEM_LIMIT),
        name="na",
    )(q, k, k, k, vt, vt, vt, tab)


def _out_kernel(x_ref, ys_ref, z_ref, yn_ref, mod_ref, gs_ref, wo_ref, gf_ref, wg_ref, wu_ref, wd_ref, o_ref, a_ref):
    gated = ys_ref[0].astype(f32) * _silu(z_ref[0].astype(f32))
    gms = jnp.mean(gated * gated, axis=-1, keepdims=True)
    y_ssd = (gated * lax.rsqrt(gms + EPS) * gs_ref[...]).astype(bf16)
    mix = _dot(y_ssd, wo_ref[0:SSD_INNER, :]) + _dot(yn_ref[0], wo_ref[SSD_INNER:, :])
    x1 = x_ref[0] + mod_ref[0, 2:3, :] * mix
    ms = jnp.mean(x1 * x1, axis=-1, keepdims=True)
    h2 = x1 * lax.rsqrt(ms + EPS) * gf_ref[...]
    h2 = (h2 * (1.0 + mod_ref[0, 4:5, :]) + mod_ref[0, 3:4, :]).astype(bf16)
    for c in range(D_FF // FF_BLOCK):
        cols = slice(c * FF_BLOCK, (c + 1) * FF_BLOCK)
        gate = _dot(h2, wg_ref[:, cols])
        up = _dot(h2, wu_ref[:, cols])
        a_ref[:, cols] = (_silu(gate) * up).astype(bf16)
    ff = _dot(a_ref[...], wd_ref[...])
    o_ref[0] = x1 + mod_ref[0, 5:6, :] * ff


def _out_ffn(x, y_pre, z, y_na, mod, g_ssd, w_out, g_ffn, w_gate, w_up, w_down):
    B, L, _ = x.shape
    T = TOKEN_TILE
    tok = lambda b, i: (b, i, 0)
    const = lambda b, i: (0, 0)
    single = pl.Buffered(1)
    return pl.pallas_call(
        _out_kernel,
        grid=(B, L // T),
        in_specs=[pl.BlockSpec((1, T, D_MODEL), tok),
                  pl.BlockSpec((1, T, SSD_INNER), tok),
                  pl.BlockSpec((1, T, SSD_INNER), tok),
                  pl.BlockSpec((1, T, NA_INNER), tok),
                  pl.BlockSpec((1, 8, D_MODEL), lambda b, i: (b, 0, 0)),
                  pl.BlockSpec((1, SSD_INNER), const),
                  pl.BlockSpec((D_MODEL, D_MODEL), const, pipeline_mode=single),
                  pl.BlockSpec((1, D_MODEL), const),
                  pl.BlockSpec((D_MODEL, D_FF), const, pipeline_mode=single),
                  pl.BlockSpec((D_MODEL, D_FF), const, pipeline_mode=single),
                  pl.BlockSpec((D_FF, D_MODEL), const, pipeline_mode=single)],
        out_specs=pl.BlockSpec((1, T, D_MODEL), tok),
        out_shape=jax.ShapeDtypeStruct((B, L, D_MODEL), f32),
        scratch_shapes=[pltpu.VMEM((T, D_FF), bf16)],
        compiler_params=pltpu.CompilerParams(dimension_semantics=("arbitrary", "arbitrary"),
                                             vmem_limit_bytes=VMEM_LIMIT),
        name="out_ffn",
    )(x, y_pre, z, y_na, mod, g_ssd, w_out, g_ffn, w_gate, w_up, w_down)


def _pad_lanes(parts):
    row = jnp.zeros((LANES,), f32)
    for off, v in parts:
        row = row.at[off:off + v.shape[0]].set(v.astype(f32))
    return row.reshape(1, LANES)


def kernel(x_prompt, x_sample, c_prompt, c_sample, w_ada, b_ada, g_mix, w_in, conv_w, conv_b, dt_bias_fwd, dt_bias_bwd, a_log_fwd, a_log_bwd, d_skip, g_ssd, g_q, g_k, rpb, w_out, g_ffn, w_gate, w_up, w_down):
    l = 0
    nb_p = x_prompt.shape[0]
    nb_s = x_sample.shape[0]

    c_all = jnp.zeros((8, D_MODEL), f32).at[:nb_p].set(c_prompt).at[nb_p:nb_p + nb_s].set(c_sample)
    mod = _modulation(c_all, w_ada[l], b_ada[l]).reshape(8, 6, D_MODEL)
    mod = jnp.pad(mod, ((0, 0), (0, 2), (0, 0)))

    o1 = SSD_INNER
    o2 = o1 + CONV_CH
    o3 = o2 + 2 * SSD_HEADS
    o4 = o3 + NA_INNER
    o5 = o4 + NA_INNER
    wi = w_in[l]
    w_main = jnp.concatenate([wi[:, :o2], wi[:, o3:o5]], axis=1).astype(bf16)
    w_dt16 = wi[:, o2:o3]
    w_dt = jnp.zeros((D_MODEL, LANES), f32).at[:, 0:16].set(w_dt16).at[:, 64:80].set(w_dt16).astype(bf16)
    w_vt = wi[:, o5:].T.astype(bf16)
    dtb = _pad_lanes([(0, dt_bias_fwd[l]), (8, dt_bias_bwd[l]), (64, dt_bias_fwd[l]), (72, dt_bias_bwd[l])])
    alog = _pad_lanes([(0, a_log_fwd[l]), (8, a_log_bwd[l])])
    gq = jnp.tile(g_q[l].astype(f32) * (HEAD_DIM ** -0.5 * LOG2E), NA_HEADS).reshape(1, NA_INNER)
    gk = jnp.tile(g_k[l].astype(f32), NA_HEADS).reshape(1, NA_INNER)
    dsk = jnp.repeat(d_skip[l].astype(f32), HEAD_DIM).reshape(1, SSD_INNER)
    tab = _na_bias_tables(rpb[l])
    wo = w_out[l].astype(bf16)
    wg = w_gate[l].astype(bf16)
    wu = w_up[l].astype(bf16)
    wd = w_down[l].astype(bf16)
    row = lambda v: v.astype(f32).reshape(1, -1)

    def trunk(x, m):
        z, xbc, dtp, q, k, vt = _in_proj(x, m, row(g_mix[l]), w_main, w_dt, w_vt, gq, gk, dtb, alog)
        xo, yb = _ssd_bwd(xbc, dtp, conv_w[l].astype(f32), row(conv_b[l]))
        y_pre = _ssd_fwd(xo, dtp, yb, dsk)
        y_na = _na(q, k, vt, tab)
        return _out_ffn(x, y_pre, z, y_na, m, row(g_ssd[l]), wo, row(g_ffn[l]), wg, wu, wd)

    return (trunk(x_prompt, mod[:nb_p]), trunk(x_sample, mod[nb_p:nb_p + nb_s]))
```

```python
import functools
import math

import numpy as np
import jax
import jax.numpy as jnp
from jax import lax
from jax.experimental import pallas as pl
from jax.experimental.pallas import tpu as pltpu

D_MODEL = 1024
HEAD_DIM = 64
SSD_HEADS = 8
SSD_INNER = SSD_HEADS * HEAD_DIM
SSD_GROUPS = 2
SSD_STATE = 128
CONV_K = 5
CHUNK = 128
NA_HEADS = 8
NA_INNER = NA_HEADS * HEAD_DIM
NA_ROWS = 8
NA_COLS = 16
GRID_W = 64
CONV_CH = SSD_INNER + 2 * SSD_GROUPS * SSD_STATE
D_FF = 2816
EPS = 1e-6

LANES = 128
CONV_HALO = 16
CONV_ROWS = 128
CONV_COLS = 256
GROUP_W = (SSD_HEADS // SSD_GROUPS) * HEAD_DIM
HEAD_PAIRS = NA_HEADS // 2
LOG2E = math.log2(math.e)
NEG_BIG = -1e30

TOKEN_TILE = 512
SSD_TILE = 512
NA_WIN_ROWS = 10
NA_WIN = NA_WIN_ROWS * GRID_W
NA_PAIRS_PER_STEP = 16
NA_QT = NA_PAIRS_PER_STEP * 2 * GRID_W
NA_HALO = 256
FF_BLOCK = 256
VMEM_LIMIT = 56 * 1024 * 1024

f32 = jnp.float32
bf16 = jnp.bfloat16


def _dot(a, b):
    return jnp.dot(a, b, preferred_element_type=f32)


def _dot_nt(a, b):
    return lax.dot_general(a, b, (((1,), (1,)), ((), ())), preferred_element_type=f32)


def _split3(v):
    p1 = v.astype(bf16)
    r1 = v - p1.astype(f32)
    p2 = r1.astype(bf16)
    p3 = (r1 - p2.astype(f32)).astype(bf16)
    return p1, p2, p3


def _even_lane_mask():
    return lax.broadcasted_iota(jnp.int32, (1, LANES), 1) < HEAD_DIM


def _silu(v):
    return v * jax.nn.sigmoid(v)


def _mod_kernel(c_ref, w_ref, b_ref, o_ref):
    s = _silu(c_ref[...])
    o_ref[...] = jnp.dot(s, w_ref[...], precision=lax.Precision.HIGHEST,
                         preferred_element_type=f32) + b_ref[...]


def _modulation(c_all, w_ada, b_ada):
    n = 6 * D_MODEL
    return pl.pallas_call(
        _mod_kernel,
        grid=(n // D_MODEL,),
        in_specs=[pl.BlockSpec((8, D_MODEL), lambda j: (0, 0)),
                  pl.BlockSpec((D_MODEL, D_MODEL), lambda j: (0, j)),
                  pl.BlockSpec((1, D_MODEL), lambda j: (0, j))],
        out_specs=pl.BlockSpec((8, D_MODEL), lambda j: (0, j)),
        out_shape=jax.ShapeDtypeStruct((8, n), f32),
        name="mod",
    )(c_all, w_ada, b_ada.reshape(1, n))


def _head_rmsnorm(v, gain):
    me = _even_lane_mask()
    s = v * v
    se = jnp.sum(jnp.where(me, s, 0.0), axis=-1, keepdims=True)
    so = jnp.sum(jnp.where(me, 0.0, s), axis=-1, keepdims=True)
    ms = jnp.where(me, se, so) * (1.0 / HEAD_DIM)
    return v * lax.rsqrt(ms + EPS) * gain


def _in_kernel(x_ref, mod_ref, g_ref, w_ref, wdt_ref, wvt_ref, gq_ref, gk_ref, dtb_ref, alog_ref,
               z_ref, xbc_ref, dtp_ref, q_ref, k_ref, vt_ref):
    half = TOKEN_TILE // 2

    def norm_mod(rows):
        x = x_ref[0, rows, :]
        ms = jnp.mean(x * x, axis=-1, keepdims=True)
        y = x * lax.rsqrt(ms + EPS) * g_ref[...]
        return (y * (1.0 + mod_ref[0, 1:2, :]) + mod_ref[0, 0:1, :]).astype(bf16)

    def proj_xbc(rows, h):
        xbc_ref[0, rows, :] = _dot(h, w_ref[:, SSD_INNER:SSD_INNER + CONV_CH]).astype(bf16)

    def proj_z(rows, h):
        z_ref[0, rows, :] = _dot(h, w_ref[:, 0:SSD_INNER]).astype(bf16)

    def proj_head_normed(rows, h, dst_ref, gain_ref, col0):
        full = _dot(h, w_ref[:, col0:col0 + NA_INNER])
        for j in range(HEAD_PAIRS):
            sl = slice(j * LANES, (j + 1) * LANES)
            dst_ref[0, j, rows, :] = _head_rmsnorm(full[:, sl], gain_ref[:, sl]).astype(bf16)

    def proj_v_dt(rows, h):
        vt = _dot_nt(wvt_ref[...], h).astype(bf16)
        tb0 = rows.start // LANES
        for p in range(HEAD_PAIRS):
            for tb in range(half // LANES):
                vt_ref[0, p, tb0 + tb] = vt[p * LANES:(p + 1) * LANES, tb * LANES:(tb + 1) * LANES]
        raw = _dot(h, wdt_ref[...]) + dtb_ref[...]
        dt = jnp.maximum(raw, 0.0) + jnp.log1p(jnp.exp(-jnp.abs(raw)))
        lane = lax.broadcasted_iota(jnp.int32, (1, LANES), 1)
        a_neg = -jnp.exp(alog_ref[...])
        mul = jnp.where(lane < 2 * SSD_HEADS, a_neg,
                        jnp.where((lane >= 64) & (lane < 64 + 2 * SSD_HEADS), 1.0, 0.0))
        dtp_ref[0, rows, :] = dt * mul

    o = SSD_INNER + CONV_CH
    ra, rb = slice(0, half), slice(half, 2 * half)
    ha = norm_mod(ra)
    proj_xbc(ra, ha)
    hb = norm_mod(rb)
    for rows, h in ((ra, ha), (rb, hb)):
        if rows is rb:
            proj_xbc(rows, h)
        proj_z(rows, h)
        proj_head_normed(rows, h, q_ref, gq_ref, o)
        proj_head_normed(rows, h, k_ref, gk_ref, o + NA_INNER)
        proj_v_dt(rows, h)


def _in_proj(x, mod, g_mix, w_main, w_dt, w_vt, gq, gk, dt_bias, a_log):
    B, L, _ = x.shape
    T = TOKEN_TILE
    nw = w_main.shape[1]
    const = lambda b, i: (0, 0)
    tok = lambda b, i: (b, i, 0)
    return pl.pallas_call(
        _in_kernel,
        grid=(B, L // T),
        in_specs=[pl.BlockSpec((1, T, D_MODEL), tok),
                  pl.BlockSpec((1, 6, D_MODEL), lambda b, i: (b, 0, 0)),
                  pl.BlockSpec((1, D_MODEL), const),
                  pl.BlockSpec((D_MODEL, nw), const),
                  pl.BlockSpec((D_MODEL, LANES), const),
                  pl.BlockSpec((NA_INNER, D_MODEL), const),
                  pl.BlockSpec((1, NA_INNER), const),
                  pl.BlockSpec((1, NA_INNER), const),
                  pl.BlockSpec((1, LANES), const),
                  pl.BlockSpec((1, LANES), const)],
        out_specs=[pl.BlockSpec((1, T, SSD_INNER), tok),
                   pl.BlockSpec((1, T, CONV_CH), tok),
                   pl.BlockSpec((1, T, LANES), tok),
                   pl.BlockSpec((1, HEAD_PAIRS, T, LANES), lambda b, i: (b, 0, i, 0)),
                   pl.BlockSpec((1, HEAD_PAIRS, T, LANES), lambda b, i: (b, 0, i, 0)),
                   pl.BlockSpec((1, HEAD_PAIRS, T // LANES, LANES, LANES), lambda b, i: (b, 0, i, 0, 0))],
        out_shape=[jax.ShapeDtypeStruct((B, L, SSD_INNER), bf16),
                   jax.ShapeDtypeStruct((B, L, CONV_CH), bf16),
                   jax.ShapeDtypeStruct((B, L, LANES), f32),
                   jax.ShapeDtypeStruct((B, HEAD_PAIRS, L, LANES), bf16),
                   jax.ShapeDtypeStruct((B, HEAD_PAIRS, L, LANES), bf16),
                   jax.ShapeDtypeStruct((B, HEAD_PAIRS, L // LANES, LANES, LANES), bf16)],
        compiler_params=pltpu.CompilerParams(dimension_semantics=("arbitrary", "arbitrary"),
                                             vmem_limit_bytes=VMEM_LIMIT),
        name="in_proj",
    )(x, mod, g_mix, w_main, w_dt, w_vt, gq, gk, dt_bias, a_log)


def _head_selector(base):
    r = lax.broadcasted_iota(jnp.int32, (LANES, SSD_INNER), 0)
    c = lax.broadcasted_iota(jnp.int32, (LANES, SSD_INNER), 1)
    return (lax.shift_right_logical(c, 6) == r - base).astype(bf16)


def _expand_heads(v, sel):
    hi = v.astype(bf16)
    lo = (v - hi.astype(f32)).astype(bf16)
    return _dot(hi, sel) + _dot(lo, sel)


def _cumsum_matrix():
    row = lax.broadcasted_iota(jnp.int32, (CHUNK, CHUNK), 0)
    col = lax.broadcasted_iota(jnp.int32, (CHUNK, CHUNK), 1)
    return (col <= row).astype(bf16)


def _chunk_decays(p, tri):
    p1, p2, p3 = _split3(p)
    incl = _dot(tri, p1) + _dot(tri, p2) + _dot(tri, p3)
    excl = incl - p
    dts = pltpu.roll(p, 64, 1)
    tot = incl[CHUNK - 1:CHUNK, :]
    is_fwd = lax.broadcasted_iota(jnp.int32, (1, LANES), 1) < SSD_HEADS
    w = jnp.exp(jnp.where(is_fwd, tot - incl, excl)) * dts
    sc = jnp.exp(jnp.where(is_fwd, incl, tot - excl))
    return incl, excl, dts, w, sc, jnp.exp(tot)


def _b_transposed(xo_ref, rows, g):
    b_g = xo_ref[0, rows, SSD_INNER + g * SSD_STATE:SSD_INNER + (g + 1) * SSD_STATE].astype(f32)
    return b_g.T.astype(bf16)


def _expand_decays(w, sc, dec, sel):
    return (_expand_heads(w, sel), _expand_heads(sc, sel),
            _expand_heads(jnp.broadcast_to(dec, (8, LANES)), sel)[0:1])


def _state_step(xo_ref, rows, st_ref, expanded, b_t):
    w_x, sc_x, dec_x = expanded
    outs = []
    for g in range(SSD_GROUPS):
        gcols = slice(g * GROUP_W, (g + 1) * GROUP_W)
        xs_g = xo_ref[0, rows, g * GROUP_W:(g + 1) * GROUP_W].astype(f32)
        c_g = xo_ref[0, rows, SSD_INNER + (SSD_GROUPS + g) * SSD_STATE:
                     SSD_INNER + (SSD_GROUPS + g + 1) * SSD_STATE]
        s_g = st_ref[g]
        y_off = _dot(c_g, s_g.astype(bf16)) * sc_x[:, gcols]
        xw = (xs_g * w_x[:, gcols]).astype(bf16)
        st_ref[g] = s_g * dec_x[:, gcols] + _dot(b_t[g], xw)
        outs.append(y_off)
    return jnp.concatenate(outs, axis=1)


def _ssd_bwd_kernel(xc_ref, xp_ref, xn_ref, dtp_ref, cw_ref, cb_ref, xo_ref, yb_ref, win_ref, st_ref, *, n_tiles):
    i = pl.program_id(1)
    t = n_tiles - 1 - i

    @pl.when(i == 0)
    def _():
        st_ref[...] = jnp.zeros_like(st_ref)

    H = CONV_HALO
    win_ref[0:H, :] = jnp.where(t > 0, xp_ref[0], jnp.zeros_like(xp_ref[0]))
    win_ref[H:H + SSD_TILE, :] = xc_ref[0]
    win_ref[H + SSD_TILE:, :] = jnp.where(t < n_tiles - 1, xn_ref[0], jnp.zeros_like(xn_ref[0]))
    r_i = lax.broadcasted_iota(jnp.int32, (CONV_ROWS, CONV_ROWS + 2 * H), 0)
    c_i = lax.broadcasted_iota(jnp.int32, (CONV_ROWS, CONV_ROWS + 2 * H), 1)
    mid = CONV_K // 2
    shift = {k: (c_i == r_i + H + k - mid).astype(bf16) for k in range(CONV_K) if k != mid}
    for c in range(SSD_TILE // CONV_ROWS):
        for cb in range(CONV_CH // CONV_COLS):
            cols = slice(cb * CONV_COLS, (cb + 1) * CONV_COLS)
            xw = win_ref[c * CONV_ROWS:(c + 1) * CONV_ROWS + 2 * H, cols]
            centre = win_ref[H + c * CONV_ROWS:H + (c + 1) * CONV_ROWS, cols].astype(f32)
            acc = cb_ref[:, cols] + centre * cw_ref[mid:mid + 1, cols]
            for k in shift:
                acc = acc + _dot(shift[k], xw) * cw_ref[k:k + 1, cols]
            xo_ref[0, c * CONV_ROWS:(c + 1) * CONV_ROWS, cols] = _silu(acc).astype(bf16)

    sel = _head_selector(SSD_HEADS)
    tri = _cumsum_matrix()

    def body(j, carry):
        c = SSD_TILE // CHUNK - 1 - j
        rows = pl.ds(pl.multiple_of(c * CHUNK, CHUNK), CHUNK)
        _, _, _, w, sc, dec = _chunk_decays(dtp_ref[0, rows, :], tri)
        b_t = [_b_transposed(xo_ref, rows, g) for g in range(SSD_GROUPS)]
        yb_ref[0, rows, :] = _state_step(xo_ref, rows, st_ref, _expand_decays(w, sc, dec, sel), b_t).astype(bf16)
        return carry

    lax.fori_loop(0, SSD_TILE // CHUNK, body, 0, unroll=True)


def _ssd_bwd(xbc, dtp, conv_w, conv_b):
    B, L, _ = xbc.shape
    T = SSD_TILE
    nT = L // T
    hb = T // CONV_HALO
    cur = lambda b, i: (b, nT - 1 - i, 0)
    prev = lambda b, i: (b, jnp.maximum((nT - 1 - i) * hb - 1, 0), 0)
    nxt = lambda b, i: (b, jnp.minimum((nT - i) * hb, L // CONV_HALO - 1), 0)
    const = lambda b, i: (0, 0)
    return pl.pallas_call(
        functools.partial(_ssd_bwd_kernel, n_tiles=nT),
        grid=(B, nT),
        in_specs=[pl.BlockSpec((1, T, CONV_CH), cur),
                  pl.BlockSpec((1, CONV_HALO, CONV_CH), prev),
                  pl.BlockSpec((1, CONV_HALO, CONV_CH), nxt),
                  pl.BlockSpec((1, T, LANES), cur),
                  pl.BlockSpec((CONV_K, CONV_CH), const),
                  pl.BlockSpec((1, CONV_CH), const)],
        out_specs=[pl.BlockSpec((1, T, CONV_CH), cur),
                   pl.BlockSpec((1, T, SSD_INNER), cur)],
        out_shape=[jax.ShapeDtypeStruct((B, L, CONV_CH), bf16),
                   jax.ShapeDtypeStruct((B, L, SSD_INNER), bf16)],
        scratch_shapes=[pltpu.VMEM((T + 2 * CONV_HALO, CONV_CH), bf16),
                        pltpu.VMEM((SSD_GROUPS, SSD_STATE, GROUP_W), f32)],
        compiler_params=pltpu.CompilerParams(dimension_semantics=("arbitrary", "arbitrary"),
                                             vmem_limit_bytes=VMEM_LIMIT),
        name="ssd_bwd",
    )(xbc, xbc, xbc, dtp, conv_w, conv_b)


def _ssd_fwd_kernel(xo_ref, dtp_ref, yb_ref, dsk_ref, o_ref, st_ref):
    @pl.when(pl.program_id(1) == 0)
    def _():
        st_ref[...] = jnp.zeros_like(st_ref)

    row = lax.broadcasted_iota(jnp.int32, (CHUNK, CHUNK), 0)
    col = lax.broadcasted_iota(jnp.int32, (CHUNK, CHUNK), 1)
    low = col < row
    diag = col == row
    me = _even_lane_mask()
    sel = _head_selector(0)
    tri = _cumsum_matrix()

    def body(c, carry):
        rows = pl.ds(pl.multiple_of(c * CHUNK, CHUNK), CHUNK)
        incl, excl, dts, w, sc, dec = _chunk_decays(dtp_ref[0, rows, :], tri)
        b_t = [_b_transposed(xo_ref, rows, g) for g in range(SSD_GROUPS)]
        y = _state_step(xo_ref, rows, st_ref, _expand_decays(w, sc, dec, sel), b_t)

        incl_t = incl.T
        excl_t = excl.T
        dts_t = dts.T
        y_diag = []
        for g in range(SSD_GROUPS):
            c_g = xo_ref[0, rows, SSD_INNER + (SSD_GROUPS + g) * SSD_STATE:
                         SSD_INNER + (SSD_GROUPS + g + 1) * SSD_STATE]
            cb = _dot(c_g, b_t[g])
            for pr in range(2):
                xs_pair = xo_ref[0, rows, g * GROUP_W + pr * LANES:g * GROUP_W + (pr + 1) * LANES]
                acc = None
                for e in range(2):
                    hd = 4 * g + 2 * pr + e
                    af_c = jnp.broadcast_to(incl[:, hd:hd + 1], (CHUNK, CHUNK))
                    eb_c = jnp.broadcast_to(excl[:, SSD_HEADS + hd:SSD_HEADS + hd + 1], (CHUNK, CHUNK))
                    af_r = incl_t[hd:hd + 1, :]
                    eb_r = excl_t[SSD_HEADS + hd:SSD_HEADS + hd + 1, :]
                    df_r = dts_t[hd:hd + 1, :]
                    db_r = dts_t[SSD_HEADS + hd:SSD_HEADS + hd + 1, :]
                    arg = jnp.where(low, af_c - af_r, eb_r - eb_c)
                    coef = jnp.where(low, df_r, jnp.where(diag, df_r + db_r, db_r))
                    wm = (cb * jnp.exp(arg) * coef).astype(bf16)
                    xm = jnp.where(me, xs_pair, jnp.zeros_like(xs_pair)) if e == 0 else \
                        jnp.where(me, jnp.zeros_like(xs_pair), xs_pair)
                    d = _dot(wm, xm)
                    acc = d if acc is None else acc + d
                y_diag.append(acc)
        xs = xo_ref[0, rows, 0:SSD_INNER].astype(f32)
        y = y + jnp.concatenate(y_diag, axis=1) + yb_ref[0, rows, :].astype(f32) + xs * dsk_ref[...]
        o_ref[0, rows, :] = y.astype(bf16)
        return carry

    lax.fori_loop(0, SSD_TILE // CHUNK, body, 0, unroll=True)


def _ssd_fwd(xo, dtp, yb, d_skip_x):
    B, L, _ = xo.shape
    T = SSD_TILE
    tok = lambda b, i: (b, i, 0)
    const = lambda b, i: (0, 0)
    return pl.pallas_call(
        _ssd_fwd_kernel,
        grid=(B, L // T),
        in_specs=[pl.BlockSpec((1, T, CONV_CH), tok),
                  pl.BlockSpec((1, T, LANES), tok),
                  pl.BlockSpec((1, T, SSD_INNER), tok),
                  pl.BlockSpec((1, SSD_INNER), const)],
        out_specs=pl.BlockSpec((1, T, SSD_INNER), tok),
        out_shape=jax.ShapeDtypeStruct((B, L, SSD_INNER), bf16),
        scratch_shapes=[pltpu.VMEM((SSD_GROUPS, SSD_STATE, GROUP_W), f32)],
        compiler_params=pltpu.CompilerParams(dimension_semantics=("arbitrary", "arbitrary"),
                                             vmem_limit_bytes=VMEM_LIMIT),
        name="ssd_fwd",
    )(xo, dtp, yb, d_skip_x)


def _na_bias_tables(rpb):
    variants = [(0, 0, 0), (2, 0, 0), (4, 0, 1), (6, 2, 2), (8, 2, 2)]
    kc = np.arange(GRID_W)[:, None]
    w = np.arange(GRID_W)[None, :]
    cs = np.clip(w - NA_COLS // 2, 0, GRID_W - NA_COLS)
    col_ok = (kc >= cs) & (kc < cs + NA_COLS)
    taps = np.arange(2 * NA_COLS - 1)[:, None, None]
    onehot = (((kc - w + NA_COLS - 1)[None] == taps) & col_ok[None]).astype(np.float32)
    base = jnp.einsum('hrk,kcw->hrcw', rpb.astype(f32) * LOG2E, onehot, precision=lax.Precision.HIGHEST)
    base = jnp.where(col_ok, base, NEG_BIG).reshape(HEAD_PAIRS, 2, 2 * NA_ROWS - 1, GRID_W, GRID_W)
    basep = jnp.pad(base, ((0, 0), (0, 0), (2, 2), (0, 0), (0, 0)), constant_values=NEG_BIG)
    n_blk = 2 * NA_ROWS + 2
    blk = jnp.concatenate([basep[:, 0, 1:1 + n_blk], basep[:, 0, 0:n_blk],
                           basep[:, 1, 1:1 + n_blk], basep[:, 1, 0:n_blk]], axis=-1)
    kr = np.arange(NA_WIN)[:, None] // GRID_W
    par = (np.arange(2 * LANES)[None, :] // GRID_W) % 2
    tabs = []
    for roff, st0, st1 in variants:
        st = np.where(par == 0, st0, st1)
        valid = (kr >= st) & (kr < st + NA_ROWS)
        j0 = NA_ROWS - roff
        rows = blk[:, j0:j0 + NA_WIN_ROWS].reshape(HEAD_PAIRS, NA_WIN, 2 * LANES)
        tabs.append(jnp.where(valid, rows, NEG_BIG))
    return jnp.stack(tabs)


def _na_kernel(q_ref, kp_ref, kc_ref, kn_ref, vp_ref, vc_ref, vn_ref, tab_ref, o_ref, kwin_ref, vwin_ref,
               *, grid_rows):
    step = pl.program_id(2)
    me = _even_lane_mask()

    kwin_ref[0:NA_HALO, :] = kp_ref[0, 0]
    kwin_ref[NA_HALO:NA_HALO + NA_QT, :] = kc_ref[0, 0]
    kwin_ref[NA_HALO + NA_QT:, :] = kn_ref[0, 0]
    ht = NA_HALO // LANES
    vwin_ref[0:ht] = vp_ref[0, 0]
    vwin_ref[ht:ht + NA_QT // LANES] = vc_ref[0, 0]
    vwin_ref[ht + NA_QT // LANES:] = vn_ref[0, 0]
    tok0 = step * NA_QT - NA_HALO

    def window(i):
        r = 2 * (step * NA_PAIRS_PER_STEP + i)
        u0 = jnp.clip(r - NA_ROWS // 2, 0, grid_rows - NA_WIN_ROWS)
        var = jnp.where(r < 4, r // 2, jnp.where(r >= grid_rows - 4, 3 + (r - (grid_rows - 4)) // 2, 2))
        return u0, var

    def scores(i):
        u0, var = window(i)
        q2 = q_ref[0, 0, i * LANES:(i + 1) * LANES, :]
        zero = jnp.zeros_like(q2)
        qm = jnp.concatenate([jnp.where(me, q2, zero), jnp.where(me, zero, q2)], axis=0)
        half = NA_WIN // 2
        parts = []
        for hk in range(2):
            kk = kwin_ref[pl.ds(pl.multiple_of(u0 * GRID_W - tok0 + hk * half, GRID_W), half), :]
            parts.append(_dot_nt(kk, qm) + tab_ref[var, 0, hk * half:(hk + 1) * half, :])
        return jnp.concatenate(parts, axis=0)

    def finish(i, parts, rl):
        oc = jnp.concatenate([parts[e] * rl[:, e * LANES:(e + 1) * LANES] for e in range(2)], axis=0)
        o_ref[0, i * LANES:(i + 1) * LANES, :] = oc.T.astype(bf16)

    s = scores(0)
    pending = None
    for i in range(NA_PAIRS_PER_STEP):
        s_next = scores(i + 1) if i + 1 < NA_PAIRS_PER_STEP else None
        u0, _ = window(i)
        m = jnp.max(s, axis=0, keepdims=True)
        p = jnp.exp2(s - m)
        l = jnp.sum(p, axis=0, keepdims=True)
        j0 = lax.shift_right_logical(u0 * GRID_W - tok0, 7)
        vtw = jnp.concatenate([vwin_ref[j0 + j] for j in range(NA_WIN // LANES)], axis=1)
        pb = p.astype(bf16)
        parts = [_dot(vtw[e * HEAD_DIM:(e + 1) * HEAD_DIM, :], pb[:, e * LANES:(e + 1) * LANES])
                 for e in range(2)]
        if pending is not None:
            finish(*pending)
        pending = (i, parts, 1.0 / l)
        s = s_next
    finish(*pending)


def _na(q, k, vt, tab):
    B, _, L, _ = q.shape
    grid_rows = L // GRID_W
    QT, H = NA_QT, NA_HALO
    r = QT // H
    nh = L // H
    nv = tab.shape[0]
    cur = lambda p, b, s: (b, p, s, 0)
    prev = lambda p, b, s: (b, p, jnp.maximum(s * r - 1, 0), 0)
    nxt = lambda p, b, s: (b, p, jnp.minimum((s + 1) * r, nh - 1), 0)
    tile = lambda f: (lambda p, b, s: f(p, b, s) + (0,))
    ht = H // LANES
    return pl.pallas_call(
        functools.partial(_na_kernel, grid_rows=grid_rows),
        grid=(HEAD_PAIRS, B, L // QT),
        in_specs=[pl.BlockSpec((1, 1, QT, LANES), cur),
                  pl.BlockSpec((1, 1, H, LANES), prev),
                  pl.BlockSpec((1, 1, QT, LANES), cur),
                  pl.BlockSpec((1, 1, H, LANES), nxt),
                  pl.BlockSpec((1, 1, ht, LANES, LANES), tile(prev)),
                  pl.BlockSpec((1, 1, QT // LANES, LANES, LANES), tile(cur)),
                  pl.BlockSpec((1, 1, ht, LANES, LANES), tile(nxt)),
                  pl.BlockSpec((nv, 1, NA_WIN, 2 * LANES), lambda p, b, s: (0, p, 0, 0))],
        out_specs=pl.BlockSpec((1, QT, LANES), lambda p, b, s: (b, s, p)),
        out_shape=jax.ShapeDtypeStruct((B, L, NA_INNER), bf16),
        scratch_shapes=[pltpu.VMEM((QT + 2 * H, LANES), bf16),
                        pltpu.VMEM((QT // LANES + 2 * ht, LANES, LANES), bf16)],
        compiler_params=pltpu.CompilerParams(dimension_semantics=("arbitrary", "arbitrary", "arbitrary"),
                                             vmem_limit_bytes=VMEM_LIMIT),
        name="na",
    )(q, k, k, k, vt, vt, vt, tab)


def _out_kernel(x_ref, ys_ref, z_ref, yn_ref, mod_ref, gs_ref, wo_ref, gf_ref, wg_ref, wu_ref, wd_ref, o_ref, a_ref):
    half = TOKEN_TILE // 2

    def gate_norm(rows):
        gated = ys_ref[0, rows, :].astype(f32) * _silu(z_ref[0, rows, :].astype(f32))
        gms = jnp.mean(gated * gated, axis=-1, keepdims=True)
        return (gated * lax.rsqrt(gms + EPS) * gs_ref[...]).astype(bf16)

    def out_proj(rows, y_ssd):
        mix = _dot(y_ssd, wo_ref[0:SSD_INNER, :]) + _dot(yn_ref[0, rows, :], wo_ref[SSD_INNER:, :])
        return x_ref[0, rows, :] + mod_ref[0, 2:3, :] * mix

    def ffn_norm(x1):
        ms = jnp.mean(x1 * x1, axis=-1, keepdims=True)
        h2 = x1 * lax.rsqrt(ms + EPS) * gf_ref[...]
        return (h2 * (1.0 + mod_ref[0, 4:5, :]) + mod_ref[0, 3:4, :]).astype(bf16)

    def ffn_block(rows, h2, c):
        cols = slice(c * FF_BLOCK, (c + 1) * FF_BLOCK)
        gate = _dot(h2, wg_ref[:, cols])
        up = _dot(h2, wu_ref[:, cols])
        a_ref[rows, cols] = (_silu(gate) * up).astype(bf16)

    def ffn_down(rows, x1):
        ff = _dot(a_ref[rows, :], wd_ref[...])
        o_ref[0, rows, :] = x1 + mod_ref[0, 5:6, :] * ff

    ra, rb = slice(0, half), slice(half, 2 * half)
    n_blocks = D_FF // FF_BLOCK
    x1a = out_proj(ra, gate_norm(ra))
    h2a = ffn_norm(x1a)
    ffn_block(ra, h2a, 0)
    ysb = gate_norm(rb)
    ffn_block(ra, h2a, 1)
    x1b = out_proj(rb, ysb)
    ffn_block(ra, h2a, 2)
    h2b = ffn_norm(x1b)
    for c in range(3, n_blocks):
        ffn_block(ra, h2a, c)
    ffn_down(ra, x1a)
    for c in range(n_blocks):
        ffn_block(rb, h2b, c)
    ffn_down(rb, x1b)


def _out_ffn(x, y_pre, z, y_na, mod, g_ssd, w_out, g_ffn, w_gate, w_up, w_down):
    B, L, _ = x.shape
    T = TOKEN_TILE
    tok = lambda b, i: (b, i, 0)
    const = lambda b, i: (0, 0)
    single = pl.Buffered(1)
    return pl.pallas_call(
        _out_kernel,
        grid=(B, L // T),
        in_specs=[pl.BlockSpec((1, T, D_MODEL), tok),
                  pl.BlockSpec((1, T, SSD_INNER), tok),
                  pl.BlockSpec((1, T, SSD_INNER), tok),
                  pl.BlockSpec((1, T, NA_INNER), tok),
                  pl.BlockSpec((1, 6, D_MODEL), lambda b, i: (b, 0, 0)),
                  pl.BlockSpec((1, SSD_INNER), const),
                  pl.BlockSpec((D_MODEL, D_MODEL), const, pipeline_mode=single),
                  pl.BlockSpec((1, D_MODEL), const),
                  pl.BlockSpec((D_MODEL, D_FF), const, pipeline_mode=single),
                  pl.BlockSpec((D_MODEL, D_FF), const, pipeline_mode=single),
                  pl.BlockSpec((D_FF, D_MODEL), const, pipeline_mode=single)],
        out_specs=pl.BlockSpec((1, T, D_MODEL), tok),
        out_shape=jax.ShapeDtypeStruct((B, L, D_MODEL), f32),
        scratch_shapes=[pltpu.VMEM((T, D_FF), bf16)],
        compiler_params=pltpu.CompilerParams(dimension_semantics=("arbitrary", "arbitrary"),
                                             vmem_limit_bytes=VMEM_LIMIT),
        name="out_ffn",
    )(x, y_pre, z, y_na, mod, g_ssd, w_out, g_ffn, w_gate, w_up, w_down)


def _pad_lanes(parts):
    pieces, pos = [], 0
    for off, v in parts:
        pieces += [jnp.zeros((off - pos,), f32), v.astype(f32)]
        pos = off + v.shape[0]
    pieces.append(jnp.zeros((LANES - pos,), f32))
    return jnp.concatenate(pieces).reshape(1, LANES)


def kernel(x_prompt, x_sample, c_prompt, c_sample, w_ada, b_ada, g_mix, w_in, conv_w, conv_b, dt_bias_fwd, dt_bias_bwd, a_log_fwd, a_log_bwd, d_skip, g_ssd, g_q, g_k, rpb, w_out, g_ffn, w_gate, w_up, w_down):
    l = 0
    nb_p = x_prompt.shape[0]
    nb_s = x_sample.shape[0]

    c_all = jnp.concatenate([c_prompt, c_sample, jnp.zeros((8 - nb_p - nb_s, D_MODEL), f32)], axis=0)
    mod = _modulation(c_all, w_ada[l], b_ada[l]).reshape(8, 6, D_MODEL)

    o1 = SSD_INNER
    o2 = o1 + CONV_CH
    o3 = o2 + 2 * SSD_HEADS
    o4 = o3 + NA_INNER
    o5 = o4 + NA_INNER
    wi = w_in[l]
    w_main = jnp.concatenate([wi[:, :o2], wi[:, o3:o5]], axis=1).astype(bf16)
    w_dt16 = wi[:, o2:o3]
    gap = jnp.zeros((D_MODEL, LANES // 2 - 2 * SSD_HEADS), f32)
    w_dt = jnp.concatenate([w_dt16, gap, w_dt16, gap], axis=1).astype(bf16)
    w_vt = wi[:, o5:].T.astype(bf16)
    dtb = _pad_lanes([(0, dt_bias_fwd[l]), (8, dt_bias_bwd[l]), (64, dt_bias_fwd[l]), (72, dt_bias_bwd[l])])
    alog = _pad_lanes([(0, a_log_fwd[l]), (8, a_log_bwd[l])])
    gq = jnp.tile(g_q[l].astype(f32) * (HEAD_DIM ** -0.5 * LOG2E), NA_HEADS).reshape(1, NA_INNER)
    gk = jnp.tile(g_k[l].astype(f32), NA_HEADS).reshape(1, NA_INNER)
    dsk = jnp.repeat(d_skip[l].astype(f32), HEAD_DIM).reshape(1, SSD_INNER)
    tab = _na_bias_tables(rpb[l])
    wo = w_out[l].astype(bf16)
    wg = w_gate[l].astype(bf16)
    wu = w_up[l].astype(bf16)
    wd = w_down[l].astype(bf16)
    row = lambda v: v.astype(f32).reshape(1, -1)

    def trunk(x, m):
        z, xbc, dtp, q, k, vt = _in_proj(x, m, row(g_mix[l]), w_main, w_dt, w_vt, gq, gk, dtb, alog)
        xo, yb = _ssd_bwd(xbc, dtp, conv_w[l].astype(f32), row(conv_b[l]))
        y_pre = _ssd_fwd(xo, dtp, yb, dsk)
        y_na = _na(q, k, vt, tab)
        return _out_ffn(x, y_pre, z, y_na, m, row(g_ssd[l]), wo, row(g_ffn[l]), wg, wu, wd)

    return (trunk(x_prompt, mod[:nb_p]), trunk(x_sample, mod[nb_p:nb_p + nb_s]))
```

```python
import functools
import math

import numpy as np
import jax
import jax.numpy as jnp
from jax import lax
from jax.experimental import pallas as pl
from jax.experimental.pallas import tpu as pltpu

D_MODEL = 1024
HEAD_DIM = 64
SSD_HEADS = 8
SSD_INNER = SSD_HEADS * HEAD_DIM
SSD_GROUPS = 2
SSD_STATE = 128
CONV_K = 5
CHUNK = 128
NA_HEADS = 8
NA_INNER = NA_HEADS * HEAD_DIM
NA_ROWS = 8
NA_COLS = 16
GRID_W = 64
CONV_CH = SSD_INNER + 2 * SSD_GROUPS * SSD_STATE
D_FF = 2816
EPS = 1e-6

LANES = 128
CONV_HALO = 16
CONV_ROWS = 128
CONV_COLS = 256
GROUP_W = (SSD_HEADS // SSD_GROUPS) * HEAD_DIM
HEAD_PAIRS = NA_HEADS // 2
LOG2E = math.log2(math.e)
NEG_BIG = -1e30

TOKEN_TILE = 1024
TOKEN_SPLIT = 1
SSD_TILE = 512
NA_WIN_ROWS = 10
NA_WIN = NA_WIN_ROWS * GRID_W
NA_PAIRS_PER_STEP = 16
NA_QT = NA_PAIRS_PER_STEP * 2 * GRID_W
NA_HALO = 256
FF_BLOCK = 256
VMEM_LIMIT = 56 * 1024 * 1024

f32 = jnp.float32
bf16 = jnp.bfloat16


def _dot(a, b):
    return jnp.dot(a, b, preferred_element_type=f32)


def _dot_nt(a, b):
    return lax.dot_general(a, b, (((1,), (1,)), ((), ())), preferred_element_type=f32)


def _split3(v):
    p1 = v.astype(bf16)
    r1 = v - p1.astype(f32)
    p2 = r1.astype(bf16)
    p3 = (r1 - p2.astype(f32)).astype(bf16)
    return p1, p2, p3


def _even_lane_mask():
    return lax.broadcasted_iota(jnp.int32, (1, LANES), 1) < HEAD_DIM


def _silu(v):
    return v * jax.nn.sigmoid(v)


def _mod_kernel(c_ref, w_ref, b_ref, o_ref):
    s = _silu(c_ref[...])
    o_ref[...] = jnp.dot(s, w_ref[...], precision=lax.Precision.HIGHEST,
                         preferred_element_type=f32) + b_ref[...]


def _modulation(c_all, w_ada, b_ada):
    n = 6 * D_MODEL
    return pl.pallas_call(
        _mod_kernel,
        grid=(n // D_MODEL,),
        in_specs=[pl.BlockSpec((8, D_MODEL), lambda j: (0, 0)),
                  pl.BlockSpec((D_MODEL, D_MODEL), lambda j: (0, j)),
                  pl.BlockSpec((1, D_MODEL), lambda j: (0, j))],
        out_specs=pl.BlockSpec((8, D_MODEL), lambda j: (0, j)),
        out_shape=jax.ShapeDtypeStruct((8, n), f32),
        name="mod",
    )(c_all, w_ada, b_ada.reshape(1, n))


def _head_rmsnorm(v, gain):
    me = _even_lane_mask()
    s = v * v
    se = jnp.sum(jnp.where(me, s, 0.0), axis=-1, keepdims=True)
    so = jnp.sum(jnp.where(me, 0.0, s), axis=-1, keepdims=True)
    ms = jnp.where(me, se, so) * (1.0 / HEAD_DIM)
    return v * lax.rsqrt(ms + EPS) * gain


def _in_kernel(x_ref, mod_ref, g_ref, w_ref, wdt_ref, wvt_ref, gq_ref, gk_ref, dtb_ref, alog_ref,
               z_ref, xbc_ref, dtp_ref, q_ref, k_ref, vt_ref):
    half = TOKEN_TILE // TOKEN_SPLIT

    def norm_mod(rows):
        x = x_ref[0, rows, :]
        ms = jnp.mean(x * x, axis=-1, keepdims=True)
        y = x * lax.rsqrt(ms + EPS) * g_ref[...]
        return (y * (1.0 + mod_ref[0, 1:2, :]) + mod_ref[0, 0:1, :]).astype(bf16)

    def proj_xbc(rows, h):
        xbc_ref[0, rows, :] = _dot(h, w_ref[:, SSD_INNER:SSD_INNER + CONV_CH]).astype(bf16)

    def proj_z(rows, h):
        z_ref[0, rows, :] = _dot(h, w_ref[:, 0:SSD_INNER]).astype(bf16)

    def proj_head_normed(rows, h, dst_ref, gain_ref, col0):
        full = _dot(h, w_ref[:, col0:col0 + NA_INNER])
        for j in range(HEAD_PAIRS):
            sl = slice(j * LANES, (j + 1) * LANES)
            dst_ref[0, j, rows, :] = _head_rmsnorm(full[:, sl], gain_ref[:, sl]).astype(bf16)

    def proj_v_dt(rows, h):
        vt = _dot_nt(wvt_ref[...], h).astype(bf16)
        tb0 = rows.start // LANES
        for p in range(HEAD_PAIRS):
            for tb in range(half // LANES):
                vt_ref[0, p, tb0 + tb] = vt[p * LANES:(p + 1) * LANES, tb * LANES:(tb + 1) * LANES]
        raw = _dot(h, wdt_ref[...]) + dtb_ref[...]
        dt = jnp.maximum(raw, 0.0) + jnp.log1p(jnp.exp(-jnp.abs(raw)))
        lane = lax.broadcasted_iota(jnp.int32, (1, LANES), 1)
        a_neg = -jnp.exp(alog_ref[...])
        mul = jnp.where(lane < 2 * SSD_HEADS, a_neg,
                        jnp.where((lane >= 64) & (lane < 64 + 2 * SSD_HEADS), 1.0, 0.0))
        dtp_ref[0, rows, :] = dt * mul

    o = SSD_INNER + CONV_CH
    for s in range(TOKEN_SPLIT):
        rows = slice(s * half, (s + 1) * half)
        h = norm_mod(rows)
        proj_xbc(rows, h)
        proj_z(rows, h)
        proj_head_normed(rows, h, q_ref, gq_ref, o)
        proj_head_normed(rows, h, k_ref, gk_ref, o + NA_INNER)
        proj_v_dt(rows, h)


def _in_proj(x, mod, g_mix, w_main, w_dt, w_vt, gq, gk, dt_bias, a_log):
    B, L, _ = x.shape
    T = TOKEN_TILE
    nw = w_main.shape[1]
    const = lambda b, i: (0, 0)
    tok = lambda b, i: (b, i, 0)
    return pl.pallas_call(
        _in_kernel,
        grid=(B, L // T),
        in_specs=[pl.BlockSpec((1, T, D_MODEL), tok),
                  pl.BlockSpec((1, 6, D_MODEL), lambda b, i: (b, 0, 0)),
                  pl.BlockSpec((1, D_MODEL), const),
                  pl.BlockSpec((D_MODEL, nw), const),
                  pl.BlockSpec((D_MODEL, LANES), const),
                  pl.BlockSpec((NA_INNER, D_MODEL), const),
                  pl.BlockSpec((1, NA_INNER), const),
                  pl.BlockSpec((1, NA_INNER), const),
                  pl.BlockSpec((1, LANES), const),
                  pl.BlockSpec((1, LANES), const)],
        out_specs=[pl.BlockSpec((1, T, SSD_INNER), tok),
                   pl.BlockSpec((1, T, CONV_CH), tok),
                   pl.BlockSpec((1, T, LANES), tok),
                   pl.BlockSpec((1, HEAD_PAIRS, T, LANES), lambda b, i: (b, 0, i, 0)),
                   pl.BlockSpec((1, HEAD_PAIRS, T, LANES), lambda b, i: (b, 0, i, 0)),
                   pl.BlockSpec((1, HEAD_PAIRS, T // LANES, LANES, LANES), lambda b, i: (b, 0, i, 0, 0))],
        out_shape=[jax.ShapeDtypeStruct((B, L, SSD_INNER), bf16),
                   jax.ShapeDtypeStruct((B, L, CONV_CH), bf16),
                   jax.ShapeDtypeStruct((B, L, LANES), f32),
                   jax.ShapeDtypeStruct((B, HEAD_PAIRS, L, LANES), bf16),
                   jax.ShapeDtypeStruct((B, HEAD_PAIRS, L, LANES), bf16),
                   jax.ShapeDtypeStruct((B, HEAD_PAIRS, L // LANES, LANES, LANES), bf16)],
        compiler_params=pltpu.CompilerParams(dimension_semantics=("arbitrary", "arbitrary"),
                                             vmem_limit_bytes=VMEM_LIMIT),
        name="in_proj",
    )(x, mod, g_mix, w_main, w_dt, w_vt, gq, gk, dt_bias, a_log)


def _head_selector(base):
    r = lax.broadcasted_iota(jnp.int32, (LANES, SSD_INNER), 0)
    c = lax.broadcasted_iota(jnp.int32, (LANES, SSD_INNER), 1)
    return (lax.shift_right_logical(c, 6) == r - base).astype(bf16)


def _expand_heads(v, sel):
    hi = v.astype(bf16)
    lo = (v - hi.astype(f32)).astype(bf16)
    return _dot(hi, sel) + _dot(lo, sel)


def _cumsum_matrix():
    row = lax.broadcasted_iota(jnp.int32, (CHUNK, CHUNK), 0)
    col = lax.broadcasted_iota(jnp.int32, (CHUNK, CHUNK), 1)
    return (col <= row).astype(bf16)


def _chunk_decays(p, tri):
    p1, p2, p3 = _split3(p)
    incl = _dot(tri, p1) + _dot(tri, p2) + _dot(tri, p3)
    excl = incl - p
    dts = pltpu.roll(p, 64, 1)
    tot = incl[CHUNK - 1:CHUNK, :]
    is_fwd = lax.broadcasted_iota(jnp.int32, (1, LANES), 1) < SSD_HEADS
    w = jnp.exp(jnp.where(is_fwd, tot - incl, excl)) * dts
    sc = jnp.exp(jnp.where(is_fwd, incl, tot - excl))
    return incl, excl, dts, w, sc, jnp.exp(tot)


def _b_transposed(xo_ref, rows, g):
    b_g = xo_ref[0, rows, SSD_INNER + g * SSD_STATE:SSD_INNER + (g + 1) * SSD_STATE].astype(f32)
    return b_g.T.astype(bf16)


def _expand_decays(w, sc, dec, sel):
    return (_expand_heads(w, sel), _expand_heads(sc, sel),
            _expand_heads(jnp.broadcast_to(dec, (8, LANES)), sel)[0:1])


def _state_step(xo_ref, rows, st_ref, expanded, b_t):
    w_x, sc_x, dec_x = expanded
    outs = []
    for g in range(SSD_GROUPS):
        gcols = slice(g * GROUP_W, (g + 1) * GROUP_W)
        xs_g = xo_ref[0, rows, g * GROUP_W:(g + 1) * GROUP_W].astype(f32)
        c_g = xo_ref[0, rows, SSD_INNER + (SSD_GROUPS + g) * SSD_STATE:
                     SSD_INNER + (SSD_GROUPS + g + 1) * SSD_STATE]
        s_g = st_ref[g]
        y_off = _dot(c_g, s_g.astype(bf16)) * sc_x[:, gcols]
        xw = (xs_g * w_x[:, gcols]).astype(bf16)
        st_ref[g] = s_g * dec_x[:, gcols] + _dot(b_t[g], xw)
        outs.append(y_off)
    return jnp.concatenate(outs, axis=1)


def _ssd_bwd_kernel(xc_ref, xp_ref, xn_ref, dtp_ref, cw_ref, cb_ref, xo_ref, yb_ref, win_ref, st_ref, *, n_tiles):
    i = pl.program_id(1)
    t = n_tiles - 1 - i

    @pl.when(i == 0)
    def _():
        st_ref[...] = jnp.zeros_like(st_ref)

    H = CONV_HALO
    win_ref[0:H, :] = jnp.where(t > 0, xp_ref[0], jnp.zeros_like(xp_ref[0]))
    win_ref[H:H + SSD_TILE, :] = xc_ref[0]
    win_ref[H + SSD_TILE:, :] = jnp.where(t < n_tiles - 1, xn_ref[0], jnp.zeros_like(xn_ref[0]))
    r_i = lax.broadcasted_iota(jnp.int32, (CONV_ROWS, CONV_ROWS + 2 * H), 0)
    c_i = lax.broadcasted_iota(jnp.int32, (CONV_ROWS, CONV_ROWS + 2 * H), 1)
    mid = CONV_K // 2
    shift = {k: (c_i == r_i + H + k - mid).astype(bf16) for k in range(CONV_K) if k != mid}
    for c in range(SSD_TILE // CONV_ROWS):
        for cb in range(CONV_CH // CONV_COLS):
            cols = slice(cb * CONV_COLS, (cb + 1) * CONV_COLS)
            xw = win_ref[c * CONV_ROWS:(c + 1) * CONV_ROWS + 2 * H, cols]
            centre = win_ref[H + c * CONV_ROWS:H + (c + 1) * CONV_ROWS, cols].astype(f32)
            acc = cb_ref[:, cols] + centre * cw_ref[mid:mid + 1, cols]
            for k in shift:
                acc = acc + _dot(shift[k], xw) * cw_ref[k:k + 1, cols]
            xo_ref[0, c * CONV_ROWS:(c + 1) * CONV_ROWS, cols] = _silu(acc).astype(bf16)

    sel = _head_selector(SSD_HEADS)
    tri = _cumsum_matrix()

    def body(j, carry):
        c = SSD_TILE // CHUNK - 1 - j
        rows = pl.ds(pl.multiple_of(c * CHUNK, CHUNK), CHUNK)
        _, _, _, w, sc, dec = _chunk_decays(dtp_ref[0, rows, :], tri)
        b_t = [_b_transposed(xo_ref, rows, g) for g in range(SSD_GROUPS)]
        yb_ref[0, rows, :] = _state_step(xo_ref, rows, st_ref, _expand_decays(w, sc, dec, sel), b_t).astype(bf16)
        return carry

    lax.fori_loop(0, SSD_TILE // CHUNK, body, 0, unroll=True)


def _ssd_bwd(xbc, dtp, conv_w, conv_b):
    B, L, _ = xbc.shape
    T = SSD_TILE
    nT = L // T
    hb = T // CONV_HALO
    cur = lambda b, i: (b, nT - 1 - i, 0)
    prev = lambda b, i: (b, jnp.maximum((nT - 1 - i) * hb - 1, 0), 0)
    nxt = lambda b, i: (b, jnp.minimum((nT - i) * hb, L // CONV_HALO - 1), 0)
    const = lambda b, i: (0, 0)
    return pl.pallas_call(
        functools.partial(_ssd_bwd_kernel, n_tiles=nT),
        grid=(B, nT),
        in_specs=[pl.BlockSpec((1, T, CONV_CH), cur),
                  pl.BlockSpec((1, CONV_HALO, CONV_CH), prev),
                  pl.BlockSpec((1, CONV_HALO, CONV_CH), nxt),
                  pl.BlockSpec((1, T, LANES), cur),
                  pl.BlockSpec((CONV_K, CONV_CH), const),
                  pl.BlockSpec((1, CONV_CH), const)],
        out_specs=[pl.BlockSpec((1, T, CONV_CH), cur),
                   pl.BlockSpec((1, T, SSD_INNER), cur)],
        out_shape=[jax.ShapeDtypeStruct((B, L, CONV_CH), bf16),
                   jax.ShapeDtypeStruct((B, L, SSD_INNER), bf16)],
        scratch_shapes=[pltpu.VMEM((T + 2 * CONV_HALO, CONV_CH), bf16),
                        pltpu.VMEM((SSD_GROUPS, SSD_STATE, GROUP_W), f32)],
        compiler_params=pltpu.CompilerParams(dimension_semantics=("arbitrary", "arbitrary"),
                                             vmem_limit_bytes=VMEM_LIMIT),
        name="ssd_bwd",
    )(xbc, xbc, xbc, dtp, conv_w, conv_b)


def _ssd_fwd_kernel(xo_ref, dtp_ref, yb_ref, dsk_ref, o_ref, st_ref):
    @pl.when(pl.program_id(1) == 0)
    def _():
        st_ref[...] = jnp.zeros_like(st_ref)

    row = lax.broadcasted_iota(jnp.int32, (CHUNK, CHUNK), 0)
    col = lax.broadcasted_iota(jnp.int32, (CHUNK, CHUNK), 1)
    low = col < row
    diag = col == row
    me = _even_lane_mask()
    sel = _head_selector(0)
    tri = _cumsum_matrix()

    def body(c, carry):
        rows = pl.ds(pl.multiple_of(c * CHUNK, CHUNK), CHUNK)
        incl, excl, dts, w, sc, dec = _chunk_decays(dtp_ref[0, rows, :], tri)
        b_t = [_b_transposed(xo_ref, rows, g) for g in range(SSD_GROUPS)]
        y = _state_step(xo_ref, rows, st_ref, _expand_decays(w, sc, dec, sel), b_t)

        incl_t = incl.T
        excl_t = excl.T
        dts_t = dts.T
        y_diag = []
        for g in range(SSD_GROUPS):
            c_g = xo_ref[0, rows, SSD_INNER + (SSD_GROUPS + g) * SSD_STATE:
                         SSD_INNER + (SSD_GROUPS + g + 1) * SSD_STATE]
            cb = _dot(c_g, b_t[g])
            for pr in range(2):
                xs_pair = xo_ref[0, rows, g * GROUP_W + pr * LANES:g * GROUP_W + (pr + 1) * LANES]
                acc = None
                for e in range(2):
                    hd = 4 * g + 2 * pr + e
                    af_c = jnp.broadcast_to(incl[:, hd:hd + 1], (CHUNK, CHUNK))
                    eb_c = jnp.broadcast_to(excl[:, SSD_HEADS + hd:SSD_HEADS + hd + 1], (CHUNK, CHUNK))
                    af_r = incl_t[hd:hd + 1, :]
                    eb_r = excl_t[SSD_HEADS + hd:SSD_HEADS + hd + 1, :]
                    df_r = dts_t[hd:hd + 1, :]
                    db_r = dts_t[SSD_HEADS + hd:SSD_HEADS + hd + 1, :]
                    arg = jnp.where(low, af_c - af_r, eb_r - eb_c)
                    coef = jnp.where(low, df_r, jnp.where(diag, df_r + db_r, db_r))
                    wm = (cb * jnp.exp(arg) * coef).astype(bf16)
                    xm = jnp.where(me, xs_pair, jnp.zeros_like(xs_pair)) if e == 0 else \
                        jnp.where(me, jnp.zeros_like(xs_pair), xs_pair)
                    d = _dot(wm, xm)
                    acc = d if acc is None else acc + d
                y_diag.append(acc)
        xs = xo_ref[0, rows, 0:SSD_INNER].astype(f32)
        y = y + jnp.concatenate(y_diag, axis=1) + yb_ref[0, rows, :].astype(f32) + xs * dsk_ref[...]
        o_ref[0, rows, :] = y.astype(bf16)
        return carry

    lax.fori_loop(0, SSD_TILE // CHUNK, body, 0, unroll=True)


def _ssd_fwd(xo, dtp, yb, d_skip_x):
    B, L, _ = xo.shape
    T = SSD_TILE
    tok = lambda b, i: (b, i, 0)
    const = lambda b, i: (0, 0)
    return pl.pallas_call(
        _ssd_fwd_kernel,
        grid=(B, L // T),
        in_specs=[pl.BlockSpec((1, T, CONV_CH), tok),
                  pl.BlockSpec((1, T, LANES), tok),
                  pl.BlockSpec((1, T, SSD_INNER), tok),
                  pl.BlockSpec((1, SSD_INNER), const)],
        out_specs=pl.BlockSpec((1, T, SSD_INNER), tok),
        out_shape=jax.ShapeDtypeStruct((B, L, SSD_INNER), bf16),
        scratch_shapes=[pltpu.VMEM((SSD_GROUPS, SSD_STATE, GROUP_W), f32)],
        compiler_params=pltpu.CompilerParams(dimension_semantics=("arbitrary", "arbitrary"),
                                             vmem_limit_bytes=VMEM_LIMIT),
        name="ssd_fwd",
    )(xo, dtp, yb, d_skip_x)


def _na_bias_tables(rpb):
    variants = [(0, 0, 0), (2, 0, 0), (4, 0, 1), (6, 2, 2), (8, 2, 2)]
    kc = np.arange(GRID_W)[:, None]
    w = np.arange(GRID_W)[None, :]
    cs = np.clip(w - NA_COLS // 2, 0, GRID_W - NA_COLS)
    col_ok = (kc >= cs) & (kc < cs + NA_COLS)
    taps = np.arange(2 * NA_COLS - 1)[:, None, None]
    onehot = (((kc - w + NA_COLS - 1)[None] == taps) & col_ok[None]).astype(np.float32)
    base = jnp.einsum('hrk,kcw->hrcw', rpb.astype(f32) * LOG2E, onehot, precision=lax.Precision.HIGHEST)
    base = jnp.where(col_ok, base, NEG_BIG).reshape(HEAD_PAIRS, 2, 2 * NA_ROWS - 1, GRID_W, GRID_W)
    basep = jnp.pad(base, ((0, 0), (0, 0), (2, 2), (0, 0), (0, 0)), constant_values=NEG_BIG)
    n_blk = 2 * NA_ROWS + 2
    blk = jnp.concatenate([basep[:, 0, 1:1 + n_blk], basep[:, 0, 0:n_blk],
                           basep[:, 1, 1:1 + n_blk], basep[:, 1, 0:n_blk]], axis=-1)
    kr = np.arange(NA_WIN)[:, None] // GRID_W
    par = (np.arange(2 * LANES)[None, :] // GRID_W) % 2
    tabs = []
    for roff, st0, st1 in variants:
        st = np.where(par == 0, st0, st1)
        valid = (kr >= st) & (kr < st + NA_ROWS)
        j0 = NA_ROWS - roff
        rows = blk[:, j0:j0 + NA_WIN_ROWS].reshape(HEAD_PAIRS, NA_WIN, 2 * LANES)
        tabs.append(jnp.where(valid, rows, NEG_BIG))
    return jnp.stack(tabs)


def _na_kernel(q_ref, kp_ref, kc_ref, kn_ref, vp_ref, vc_ref, vn_ref, tab_ref, o_ref, kwin_ref, vwin_ref,
               *, grid_rows):
    step = pl.program_id(2)
    me = _even_lane_mask()

    kwin_ref[0:NA_HALO, :] = kp_ref[0, 0]
    kwin_ref[NA_HALO:NA_HALO + NA_QT, :] = kc_ref[0, 0]
    kwin_ref[NA_HALO + NA_QT:, :] = kn_ref[0, 0]
    ht = NA_HALO // LANES
    vwin_ref[0:ht] = vp_ref[0, 0]
    vwin_ref[ht:ht + NA_QT // LANES] = vc_ref[0, 0]
    vwin_ref[ht + NA_QT // LANES:] = vn_ref[0, 0]
    tok0 = step * NA_QT - NA_HALO

    def window(i):
        r = 2 * (step * NA_PAIRS_PER_STEP + i)
        u0 = jnp.clip(r - NA_ROWS // 2, 0, grid_rows - NA_WIN_ROWS)
        var = jnp.where(r < 4, r // 2, jnp.where(r >= grid_rows - 4, 3 + (r - (grid_rows - 4)) // 2, 2))
        return u0, var

    def scores(i):
        u0, var = window(i)
        q2 = q_ref[0, 0, i * LANES:(i + 1) * LANES, :]
        zero = jnp.zeros_like(q2)
        qm = jnp.concatenate([jnp.where(me, q2, zero), jnp.where(me, zero, q2)], axis=0)
        half = NA_WIN // 2
        parts = []
        for hk in range(2):
            kk = kwin_ref[pl.ds(pl.multiple_of(u0 * GRID_W - tok0 + hk * half, GRID_W), half), :]
            parts.append(_dot_nt(kk, qm) + tab_ref[var, 0, hk * half:(hk + 1) * half, :])
        return jnp.concatenate(parts, axis=0)

    def finish(i, parts, rl):
        oc = jnp.concatenate([parts[e] * rl[:, e * LANES:(e + 1) * LANES] for e in range(2)], axis=0)
        o_ref[0, i * LANES:(i + 1) * LANES, :] = oc.T.astype(bf16)

    s = scores(0)
    pending = None
    for i in range(NA_PAIRS_PER_STEP):
        s_next = scores(i + 1) if i + 1 < NA_PAIRS_PER_STEP else None
        u0, _ = window(i)
        m = jnp.max(s, axis=0, keepdims=True)
        p = jnp.exp2(s - m)
        l = jnp.sum(p, axis=0, keepdims=True)
        j0 = lax.shift_right_logical(u0 * GRID_W - tok0, 7)
        vtw = jnp.concatenate([vwin_ref[j0 + j] for j in range(NA_WIN // LANES)], axis=1)
        pb = p.astype(bf16)
        parts = [_dot(vtw[e * HEAD_DIM:(e + 1) * HEAD_DIM, :], pb[:, e * LANES:(e + 1) * LANES])
                 for e in range(2)]
        if pending is not None:
            finish(*pending)
        pending = (i, parts, 1.0 / l)
        s = s_next
    finish(*pending)


def _na(q, k, vt, tab):
    B, _, L, _ = q.shape
    grid_rows = L // GRID_W
    QT, H = NA_QT, NA_HALO
    r = QT // H
    nh = L // H
    nv = tab.shape[0]
    cur = lambda p, b, s: (b, p, s, 0)
    prev = lambda p, b, s: (b, p, jnp.maximum(s * r - 1, 0), 0)
    nxt = lambda p, b, s: (b, p, jnp.minimum((s + 1) * r, nh - 1), 0)
    tile = lambda f: (lambda p, b, s: f(p, b, s) + (0,))
    ht = H // LANES
    return pl.pallas_call(
        functools.partial(_na_kernel, grid_rows=grid_rows),
        grid=(HEAD_PAIRS, B, L // QT),
        in_specs=[pl.BlockSpec((1, 1, QT, LANES), cur),
                  pl.BlockSpec((1, 1, H, LANES), prev),
                  pl.BlockSpec((1, 1, QT, LANES), cur),
                  pl.BlockSpec((1, 1, H, LANES), nxt),
                  pl.BlockSpec((1, 1, ht, LANES, LANES), tile(prev)),
                  pl.BlockSpec((1, 1, QT // LANES, LANES, LANES), tile(cur)),
                  pl.BlockSpec((1, 1, ht, LANES, LANES), tile(nxt)),
                  pl.BlockSpec((nv, 1, NA_WIN, 2 * LANES), lambda p, b, s: (0, p, 0, 0))],
        out_specs=pl.BlockSpec((1, QT, LANES), lambda p, b, s: (b, s, p)),
        out_shape=jax.ShapeDtypeStruct((B, L, NA_INNER), bf16),
        scratch_shapes=[pltpu.VMEM((QT + 2 * H, LANES), bf16),
                        pltpu.VMEM((QT // LANES + 2 * ht, LANES, LANES), bf16)],
        compiler_params=pltpu.CompilerParams(dimension_semantics=("arbitrary", "arbitrary", "arbitrary"),
                                             vmem_limit_bytes=VMEM_LIMIT),
        name="na",
    )(q, k, k, k, vt, vt, vt, tab)


def _out_kernel(x_ref, ys_ref, z_ref, yn_ref, mod_ref, gs_ref, wo_ref, gf_ref, wg_ref, wu_ref, wd_ref, o_ref, a_ref):
    half = TOKEN_TILE // TOKEN_SPLIT

    def gate_norm(rows):
        gated = ys_ref[0, rows, :].astype(f32) * _silu(z_ref[0, rows, :].astype(f32))
        gms = jnp.mean(gated * gated, axis=-1, keepdims=True)
        return (gated * lax.rsqrt(gms + EPS) * gs_ref[...]).astype(bf16)

    def out_proj(rows, y_ssd):
        mix = _dot(y_ssd, wo_ref[0:SSD_INNER, :]) + _dot(yn_ref[0, rows, :], wo_ref[SSD_INNER:, :])
        return x_ref[0, rows, :] + mod_ref[0, 2:3, :] * mix

    def ffn_norm(x1):
        ms = jnp.mean(x1 * x1, axis=-1, keepdims=True)
        h2 = x1 * lax.rsqrt(ms + EPS) * gf_ref[...]
        return (h2 * (1.0 + mod_ref[0, 4:5, :]) + mod_ref[0, 3:4, :]).astype(bf16)

    def ffn_block(rows, h2, c):
        cols = slice(c * FF_BLOCK, (c + 1) * FF_BLOCK)
        gate = _dot(h2, wg_ref[:, cols])
        up = _dot(h2, wu_ref[:, cols])
        a_ref[rows, cols] = (_silu(gate) * up).astype(bf16)

    def ffn_down(rows, x1):
        ff = _dot(a_ref[rows, :], wd_ref[...])
        o_ref[0, rows, :] = x1 + mod_ref[0, 5:6, :] * ff

    for s in range(TOKEN_SPLIT):
        rows = slice(s * half, (s + 1) * half)
        x1 = out_proj(rows, gate_norm(rows))
        h2 = ffn_norm(x1)
        for c in range(D_FF // FF_BLOCK):
            ffn_block(rows, h2, c)
        ffn_down(rows, x1)


def _out_ffn(x, y_pre, z, y_na, mod, g_ssd, w_out, g_ffn, w_gate, w_up, w_down):
    B, L, _ = x.shape
    T = TOKEN_TILE
    tok = lambda b, i: (b, i, 0)
    const = lambda b, i: (0, 0)
    single = pl.Buffered(1)
    return pl.pallas_call(
        _out_kernel,
        grid=(B, L // T),
        in_specs=[pl.BlockSpec((1, T, D_MODEL), tok),
                  pl.BlockSpec((1, T, SSD_INNER), tok),
                  pl.BlockSpec((1, T, SSD_INNER), tok),
                  pl.BlockSpec((1, T, NA_INNER), tok),
                  pl.BlockSpec((1, 6, D_MODEL), lambda b, i: (b, 0, 0)),
                  pl.BlockSpec((1, SSD_INNER), const),
                  pl.BlockSpec((D_MODEL, D_MODEL), const, pipeline_mode=single),
                  pl.BlockSpec((1, D_MODEL), const),
                  pl.BlockSpec((D_MODEL, D_FF), const, pipeline_mode=single),
                  pl.BlockSpec((D_MODEL, D_FF), const, pipeline_mode=single),
                  pl.BlockSpec((D_FF, D_MODEL), const, pipeline_mode=single)],
        out_specs=pl.BlockSpec((1, T, D_MODEL), tok),
        out_shape=jax.ShapeDtypeStruct((B, L, D_MODEL), f32),
        scratch_shapes=[pltpu.VMEM((T, D_FF), bf16)],
        compiler_params=pltpu.CompilerParams(dimension_semantics=("arbitrary", "arbitrary"),
                                             vmem_limit_bytes=VMEM_LIMIT),
        name="out_ffn",
    )(x, y_pre, z, y_na, mod, g_ssd, w_out, g_ffn, w_gate, w_up, w_down)


def _pad_lanes(parts):
    pieces, pos = [], 0
    for off, v in parts:
        pieces += [jnp.zeros((off - pos,), f32), v.astype(f32)]
        pos = off + v.shape[0]
    pieces.append(jnp.zeros((LANES - pos,), f32))
    return jnp.concatenate(pieces).reshape(1, LANES)


def kernel(x_prompt, x_sample, c_prompt, c_sample, w_ada, b_ada, g_mix, w_in, conv_w, conv_b, dt_bias_fwd, dt_bias_bwd, a_log_fwd, a_log_bwd, d_skip, g_ssd, g_q, g_k, rpb, w_out, g_ffn, w_gate, w_up, w_down):
    l = 0
    nb_p = x_prompt.shape[0]
    nb_s = x_sample.shape[0]

    c_all = jnp.concatenate([c_prompt, c_sample, jnp.zeros((8 - nb_p - nb_s, D_MODEL), f32)], axis=0)
    mod = _modulation(c_all, w_ada[l], b_ada[l]).reshape(8, 6, D_MODEL)

    o1 = SSD_INNER
    o2 = o1 + CONV_CH
    o3 = o2 + 2 * SSD_HEADS
    o4 = o3 + NA_INNER
    o5 = o4 + NA_INNER
    wi = w_in[l]
    w_main = jnp.concatenate([wi[:, :o2], wi[:, o3:o5]], axis=1).astype(bf16)
    w_dt16 = wi[:, o2:o3]
    gap = jnp.zeros((D_MODEL, LANES // 2 - 2 * SSD_HEADS), f32)
    w_dt = jnp.concatenate([w_dt16, gap, w_dt16, gap], axis=1).astype(bf16)
    w_vt = wi[:, o5:].T.astype(bf16)
    dtb = _pad_lanes([(0, dt_bias_fwd[l]), (8, dt_bias_bwd[l]), (64, dt_bias_fwd[l]), (72, dt_bias_bwd[l])])
    alog = _pad_lanes([(0, a_log_fwd[l]), (8, a_log_bwd[l])])
    gq = jnp.tile(g_q[l].astype(f32) * (HEAD_DIM ** -0.5 * LOG2E), NA_HEADS).reshape(1, NA_INNER)
    gk = jnp.tile(g_k[l].astype(f32), NA_HEADS).reshape(1, NA_INNER)
    dsk = jnp.repeat(d_skip[l].astype(f32), HEAD_DIM).reshape(1, SSD_INNER)
    tab = _na_bias_tables(rpb[l])
    wo = w_out[l].astype(bf16)
    wg = w_gate[l].astype(bf16)
    wu = w_up[l].astype(bf16)
    wd = w_down[l].astype(bf16)
    row = lambda v: v.astype(f32).reshape(1, -1)

    def trunk(x, m):
        z, xbc, dtp, q, k, vt = _in_proj(x, m, row(g_mix[l]), w_main, w_dt, w_vt, gq, gk, dtb, alog)
        xo, yb = _ssd_bwd(xbc, dtp, conv_w[l].astype(f32), row(conv_b[l]))
        y_pre = _ssd_fwd(xo, dtp, yb, dsk)
        y_na = _na(q, k, vt, tab)
        return _out_ffn(x, y_pre, z, y_na, m, row(g_ssd[l]), wo, row(g_ffn[l]), wg, wu, wd)

    return (trunk(x_prompt, mod[:nb_p]), trunk(x_sample, mod[nb_p:nb_p + nb_s]))
```

```python
import functools
import math

import numpy as np
import jax
import jax.numpy as jnp
from jax import lax
from jax.experimental import pallas as pl
from jax.experimental.pallas import tpu as pltpu

D_MODEL = 1024
HEAD_DIM = 64
SSD_HEADS = 8
SSD_INNER = SSD_HEADS * HEAD_DIM
SSD_GROUPS = 2
SSD_STATE = 128
CONV_K = 5
CHUNK = 128
NA_HEADS = 8
NA_INNER = NA_HEADS * HEAD_DIM
NA_ROWS = 8
NA_COLS = 16
GRID_W = 64
CONV_CH = SSD_INNER + 2 * SSD_GROUPS * SSD_STATE
D_FF = 2816
EPS = 1e-6

LANES = 128
BF16_ROWS = 16
CONV_HALO = BF16_ROWS
CONV_ROWS = 128
CONV_COLS = 256
GROUP_W = (SSD_HEADS // SSD_GROUPS) * HEAD_DIM
HEAD_PAIRS = NA_HEADS // 2
LOG2E = math.log2(math.e)
NEG_BIG = -1e30

TOKEN_TILE = 1024
TOKEN_SPLIT = 1
SSD_TILE = 512
NA_WIN_ROWS = 10
NA_WIN = NA_WIN_ROWS * GRID_W
NA_PAIRS_PER_STEP = 16
NA_QT = NA_PAIRS_PER_STEP * 2 * GRID_W
NA_HALO = 256
FF_BLOCK = 256
VMEM_LIMIT = 56 * 1024 * 1024

f32 = jnp.float32
bf16 = jnp.bfloat16


def _dot(a, b):
    return jnp.dot(a, b, preferred_element_type=f32)


def _dot_nt(a, b):
    return lax.dot_general(a, b, (((1,), (1,)), ((), ())), preferred_element_type=f32)


def _split3(v):
    p1 = v.astype(bf16)
    r1 = v - p1.astype(f32)
    p2 = r1.astype(bf16)
    p3 = (r1 - p2.astype(f32)).astype(bf16)
    return p1, p2, p3


def _even_lane_mask():
    return lax.broadcasted_iota(jnp.int32, (1, LANES), 1) < HEAD_DIM


def _silu(v):
    return v * jax.nn.sigmoid(v)


def _mod_kernel(c_ref, w_ref, b_ref, o_ref):
    s = _silu(c_ref[...])
    o_ref[...] = jnp.dot(s, w_ref[...], precision=lax.Precision.HIGHEST,
                         preferred_element_type=f32) + b_ref[...]


def _modulation(c_all, w_ada, b_ada):
    n = 6 * D_MODEL
    return pl.pallas_call(
        _mod_kernel,
        grid=(n // D_MODEL,),
        in_specs=[pl.BlockSpec((8, D_MODEL), lambda j: (0, 0)),
                  pl.BlockSpec((D_MODEL, D_MODEL), lambda j: (0, j)),
                  pl.BlockSpec((1, D_MODEL), lambda j: (0, j))],
        out_specs=pl.BlockSpec((8, D_MODEL), lambda j: (0, j)),
        out_shape=jax.ShapeDtypeStruct((8, n), f32),
        name="mod",
    )(c_all, w_ada, b_ada.reshape(1, n))


def _head_rmsnorm(v, gain):
    me = _even_lane_mask()
    s = v * v
    se = jnp.sum(jnp.where(me, s, 0.0), axis=-1, keepdims=True)
    so = jnp.sum(jnp.where(me, 0.0, s), axis=-1, keepdims=True)
    ms = jnp.where(me, se, so) * (1.0 / HEAD_DIM)
    return v * lax.rsqrt(ms + EPS) * gain


def _in_kernel(x_ref, mod_ref, g_ref, w_ref, wdt_ref, wvt_ref, gq_ref, gk_ref, dtb_ref, alog_ref,
               z_ref, xbc_ref, dtp_ref, q_ref, k_ref, vt_ref):
    half = TOKEN_TILE // TOKEN_SPLIT

    def norm_mod(rows):
        x = x_ref[0, rows, :]
        ms = jnp.mean(x * x, axis=-1, keepdims=True)
        y = x * lax.rsqrt(ms + EPS) * g_ref[...]
        return (y * (1.0 + mod_ref[0, 1:2, :]) + mod_ref[0, 0:1, :]).astype(bf16)

    def proj_xbc(rows, h):
        xbc_ref[0, rows, :] = _dot(h, w_ref[:, SSD_INNER:SSD_INNER + CONV_CH]).astype(bf16)

    def proj_z(rows, h):
        z_ref[0, rows, :] = _dot(h, w_ref[:, 0:SSD_INNER]).astype(bf16)

    def proj_head_normed(rows, h, dst_ref, gain_ref, col0):
        full = _dot(h, w_ref[:, col0:col0 + NA_INNER])
        for j in range(HEAD_PAIRS):
            sl = slice(j * LANES, (j + 1) * LANES)
            dst_ref[0, j, rows, :] = _head_rmsnorm(full[:, sl], gain_ref[:, sl]).astype(bf16)

    def proj_v_dt(rows, h):
        vt = _dot_nt(wvt_ref[...], h).astype(bf16)
        tb0 = rows.start // LANES
        for p in range(HEAD_PAIRS):
            for tb in range(half // LANES):
                vt_ref[0, p, tb0 + tb] = vt[p * LANES:(p + 1) * LANES, tb * LANES:(tb + 1) * LANES]
        raw = _dot(h, wdt_ref[...]) + dtb_ref[...]
        dt = jnp.maximum(raw, 0.0) + jnp.log1p(jnp.exp(-jnp.abs(raw)))
        lane = lax.broadcasted_iota(jnp.int32, (1, LANES), 1)
        a_neg = -jnp.exp(alog_ref[...])
        mul = jnp.where(lane < 2 * SSD_HEADS, a_neg,
                        jnp.where((lane >= 64) & (lane < 64 + 2 * SSD_HEADS), 1.0, 0.0))
        dtp_ref[0, rows, :] = dt * mul

    o = SSD_INNER + CONV_CH
    for s in range(TOKEN_SPLIT):
        rows = slice(s * half, (s + 1) * half)
        h = norm_mod(rows)
        proj_xbc(rows, h)
        proj_z(rows, h)
        proj_head_normed(rows, h, q_ref, gq_ref, o)
        proj_head_normed(rows, h, k_ref, gk_ref, o + NA_INNER)
        proj_v_dt(rows, h)


def _in_proj(x, mod, g_mix, w_main, w_dt, w_vt, gq, gk, dt_bias, a_log):
    B, L, _ = x.shape
    T = TOKEN_TILE
    nw = w_main.shape[1]
    const = lambda b, i: (0, 0)
    tok = lambda b, i: (b, i, 0)
    return pl.pallas_call(
        _in_kernel,
        grid=(B, L // T),
        in_specs=[pl.BlockSpec((1, T, D_MODEL), tok),
                  pl.BlockSpec((1, 6, D_MODEL), lambda b, i: (b, 0, 0)),
                  pl.BlockSpec((1, D_MODEL), const),
                  pl.BlockSpec((D_MODEL, nw), const),
                  pl.BlockSpec((D_MODEL, LANES), const),
                  pl.BlockSpec((NA_INNER, D_MODEL), const),
                  pl.BlockSpec((1, NA_INNER), const),
                  pl.BlockSpec((1, NA_INNER), const),
                  pl.BlockSpec((1, LANES), const),
                  pl.BlockSpec((1, LANES), const)],
        out_specs=[pl.BlockSpec((1, T, SSD_INNER), tok),
                   pl.BlockSpec((1, T, CONV_CH), tok),
                   pl.BlockSpec((1, T, LANES), tok),
                   pl.BlockSpec((1, HEAD_PAIRS, T, LANES), lambda b, i: (b, 0, i, 0)),
                   pl.BlockSpec((1, HEAD_PAIRS, T, LANES), lambda b, i: (b, 0, i, 0)),
                   pl.BlockSpec((1, HEAD_PAIRS, T // LANES, LANES, LANES), lambda b, i: (b, 0, i, 0, 0))],
        out_shape=[jax.ShapeDtypeStruct((B, L, SSD_INNER), bf16),
                   jax.ShapeDtypeStruct((B, L, CONV_CH), bf16),
                   jax.ShapeDtypeStruct((B, L, LANES), f32),
                   jax.ShapeDtypeStruct((B, HEAD_PAIRS, L, LANES), bf16),
                   jax.ShapeDtypeStruct((B, HEAD_PAIRS, L, LANES), bf16),
                   jax.ShapeDtypeStruct((B, HEAD_PAIRS, L // LANES, LANES, LANES), bf16)],
        compiler_params=pltpu.CompilerParams(dimension_semantics=("arbitrary", "arbitrary"),
                                             vmem_limit_bytes=VMEM_LIMIT),
        name="in_proj",
    )(x, mod, g_mix, w_main, w_dt, w_vt, gq, gk, dt_bias, a_log)


def _head_selector(base):
    r = lax.broadcasted_iota(jnp.int32, (LANES, SSD_INNER), 0)
    c = lax.broadcasted_iota(jnp.int32, (LANES, SSD_INNER), 1)
    return (lax.shift_right_logical(c, 6) == r - base).astype(bf16)


def _expand_heads(v, sel):
    hi = v.astype(bf16)
    lo = (v - hi.astype(f32)).astype(bf16)
    return _dot(hi, sel) + _dot(lo, sel)


def _cumsum_matrix():
    row = lax.broadcasted_iota(jnp.int32, (CHUNK, CHUNK), 0)
    col = lax.broadcasted_iota(jnp.int32, (CHUNK, CHUNK), 1)
    return (col <= row).astype(bf16)


def _chunk_decays(p, tri):
    p1, p2, p3 = _split3(p)
    incl = _dot(tri, p1) + _dot(tri, p2) + _dot(tri, p3)
    excl = incl - p
    dts = pltpu.roll(p, 64, 1)
    tot = incl[CHUNK - 1:CHUNK, :]
    is_fwd = lax.broadcasted_iota(jnp.int32, (1, LANES), 1) < SSD_HEADS
    w = jnp.exp(jnp.where(is_fwd, tot - incl, excl)) * dts
    sc = jnp.exp(jnp.where(is_fwd, incl, tot - excl))
    return incl, excl, dts, w, sc, jnp.exp(tot)


def _b_transposed(xo_ref, rows, g):
    b_g = xo_ref[0, rows, SSD_INNER + g * SSD_STATE:SSD_INNER + (g + 1) * SSD_STATE].astype(f32)
    return b_g.T.astype(bf16)


def _expand_decays(w, sc, dec, sel):
    return (_expand_heads(w, sel), _expand_heads(sc, sel),
            _expand_heads(jnp.broadcast_to(dec, (8, LANES)), sel)[0:1])


def _state_step(xo_ref, rows, st_ref, expanded, b_t):
    w_x, sc_x, dec_x = expanded
    outs = []
    for g in range(SSD_GROUPS):
        gcols = slice(g * GROUP_W, (g + 1) * GROUP_W)
        xs_g = xo_ref[0, rows, g * GROUP_W:(g + 1) * GROUP_W].astype(f32)
        c_g = xo_ref[0, rows, SSD_INNER + (SSD_GROUPS + g) * SSD_STATE:
                     SSD_INNER + (SSD_GROUPS + g + 1) * SSD_STATE]
        s_g = st_ref[g]
        y_off = _dot(c_g, s_g.astype(bf16)) * sc_x[:, gcols]
        xw = (xs_g * w_x[:, gcols]).astype(bf16)
        st_ref[g] = s_g * dec_x[:, gcols] + _dot(b_t[g], xw)
        outs.append(y_off)
    return jnp.concatenate(outs, axis=1)


def _ssd_bwd_kernel(xc_ref, xp_ref, xn_ref, dtp_ref, cw_ref, cb_ref, xo_ref, yb_ref, win_ref, st_ref, *, n_tiles):
    i = pl.program_id(1)
    t = n_tiles - 1 - i

    @pl.when(i == 0)
    def _():
        st_ref[...] = jnp.zeros_like(st_ref)

    H = CONV_HALO
    win_ref[0:H, :] = jnp.where(t > 0, xp_ref[0], jnp.zeros_like(xp_ref[0]))
    win_ref[H:H + SSD_TILE, :] = xc_ref[0]
    win_ref[H + SSD_TILE:, :] = jnp.where(t < n_tiles - 1, xn_ref[0], jnp.zeros_like(xn_ref[0]))
    r_i = lax.broadcasted_iota(jnp.int32, (CONV_ROWS, CONV_ROWS + 2 * H), 0)
    c_i = lax.broadcasted_iota(jnp.int32, (CONV_ROWS, CONV_ROWS + 2 * H), 1)
    mid = CONV_K // 2
    shift = {k: (c_i == r_i + H + k - mid).astype(bf16) for k in range(CONV_K) if k != mid}
    for c in range(SSD_TILE // CONV_ROWS):
        for cb in range(CONV_CH // CONV_COLS):
            cols = slice(cb * CONV_COLS, (cb + 1) * CONV_COLS)
            xw = win_ref[c * CONV_ROWS:(c + 1) * CONV_ROWS + 2 * H, cols]
            centre = win_ref[H + c * CONV_ROWS:H + (c + 1) * CONV_ROWS, cols].astype(f32)
            acc = cb_ref[:, cols] + centre * cw_ref[mid:mid + 1, cols]
            for k in shift:
                acc = acc + _dot(shift[k], xw) * cw_ref[k:k + 1, cols]
            xo_ref[0, c * CONV_ROWS:(c + 1) * CONV_ROWS, cols] = _silu(acc).astype(bf16)

    sel = _head_selector(SSD_HEADS)
    tri = _cumsum_matrix()

    def body(j, carry):
        c = SSD_TILE // CHUNK - 1 - j
        rows = pl.ds(pl.multiple_of(c * CHUNK, CHUNK), CHUNK)
        _, _, _, w, sc, dec = _chunk_decays(dtp_ref[0, rows, :], tri)
        b_t = [_b_transposed(xo_ref, rows, g) for g in range(SSD_GROUPS)]
        yb_ref[0, rows, :] = _state_step(xo_ref, rows, st_ref, _expand_decays(w, sc, dec, sel), b_t).astype(bf16)
        return carry

    lax.fori_loop(0, SSD_TILE // CHUNK, body, 0, unroll=True)


def _ssd_bwd(xbc, dtp, conv_w, conv_b):
    B, L, _ = xbc.shape
    T = SSD_TILE
    nT = L // T
    hb = T // CONV_HALO
    cur = lambda b, i: (b, nT - 1 - i, 0)
    prev = lambda b, i: (b, jnp.maximum((nT - 1 - i) * hb - 1, 0), 0)
    nxt = lambda b, i: (b, jnp.minimum((nT - i) * hb, L // CONV_HALO - 1), 0)
    const = lambda b, i: (0, 0)
    return pl.pallas_call(
        functools.partial(_ssd_bwd_kernel, n_tiles=nT),
        grid=(B, nT),
        in_specs=[pl.BlockSpec((1, T, CONV_CH), cur),
                  pl.BlockSpec((1, CONV_HALO, CONV_CH), prev),
                  pl.BlockSpec((1, CONV_HALO, CONV_CH), nxt),
                  pl.BlockSpec((1, T, LANES), cur),
                  pl.BlockSpec((CONV_K, CONV_CH), const),
                  pl.BlockSpec((1, CONV_CH), const)],
        out_specs=[pl.BlockSpec((1, T, CONV_CH), cur),
                   pl.BlockSpec((1, T, SSD_INNER), cur)],
        out_shape=[jax.ShapeDtypeStruct((B, L, CONV_CH), bf16),
                   jax.ShapeDtypeStruct((B, L, SSD_INNER), bf16)],
        scratch_shapes=[pltpu.VMEM((T + 2 * CONV_HALO, CONV_CH), bf16),
                        pltpu.VMEM((SSD_GROUPS, SSD_STATE, GROUP_W), f32)],
        compiler_params=pltpu.CompilerParams(dimension_semantics=("arbitrary", "arbitrary"),
                                             vmem_limit_bytes=VMEM_LIMIT),
        name="ssd_bwd",
    )(xbc, xbc, xbc, dtp, conv_w, conv_b)


def _ssd_fwd_kernel(xo_ref, dtp_ref, yb_ref, dsk_ref, o_ref, st_ref):
    @pl.when(pl.program_id(1) == 0)
    def _():
        st_ref[...] = jnp.zeros_like(st_ref)

    row = lax.broadcasted_iota(jnp.int32, (CHUNK, CHUNK), 0)
    col = lax.broadcasted_iota(jnp.int32, (CHUNK, CHUNK), 1)
    low = col < row
    diag = col == row
    me = _even_lane_mask()
    sel = _head_selector(0)
    tri = _cumsum_matrix()

    def body(c, carry):
        rows = pl.ds(pl.multiple_of(c * CHUNK, CHUNK), CHUNK)
        incl, excl, dts, w, sc, dec = _chunk_decays(dtp_ref[0, rows, :], tri)
        b_t = [_b_transposed(xo_ref, rows, g) for g in range(SSD_GROUPS)]
        y = _state_step(xo_ref, rows, st_ref, _expand_decays(w, sc, dec, sel), b_t)

        incl_t = incl.T
        excl_t = excl.T
        dts_t = dts.T
        y_diag = []
        for g in range(SSD_GROUPS):
            c_g = xo_ref[0, rows, SSD_INNER + (SSD_GROUPS + g) * SSD_STATE:
                         SSD_INNER + (SSD_GROUPS + g + 1) * SSD_STATE]
            cb = _dot(c_g, b_t[g])
            for pr in range(2):
                xs_pair = xo_ref[0, rows, g * GROUP_W + pr * LANES:g * GROUP_W + (pr + 1) * LANES]
                acc = None
                for e in range(2):
                    hd = 4 * g + 2 * pr + e
                    af_c = jnp.broadcast_to(incl[:, hd:hd + 1], (CHUNK, CHUNK))
                    eb_c = jnp.broadcast_to(excl[:, SSD_HEADS + hd:SSD_HEADS + hd + 1], (CHUNK, CHUNK))
                    af_r = incl_t[hd:hd + 1, :]
                    eb_r = excl_t[SSD_HEADS + hd:SSD_HEADS + hd + 1, :]
                    df_r = dts_t[hd:hd + 1, :]
                    db_r = dts_t[SSD_HEADS + hd:SSD_HEADS + hd + 1, :]
                    arg = jnp.where(low, af_c - af_r, eb_r - eb_c)
                    coef = jnp.where(low, df_r, jnp.where(diag, df_r + db_r, db_r))
                    wm = (cb * jnp.exp(arg) * coef).astype(bf16)
                    xm = jnp.where(me, xs_pair, jnp.zeros_like(xs_pair)) if e == 0 else \
                        jnp.where(me, jnp.zeros_like(xs_pair), xs_pair)
                    d = _dot(wm, xm)
                    acc = d if acc is None else acc + d
                y_diag.append(acc)
        xs = xo_ref[0, rows, 0:SSD_INNER].astype(f32)
        y = y + jnp.concatenate(y_diag, axis=1) + yb_ref[0, rows, :].astype(f32) + xs * dsk_ref[...]
        o_ref[0, rows, :] = y.astype(bf16)
        return carry

    lax.fori_loop(0, SSD_TILE // CHUNK, body, 0, unroll=True)


def _ssd_fwd(xo, dtp, yb, d_skip_x):
    B, L, _ = xo.shape
    T = SSD_TILE
    tok = lambda b, i: (b, i, 0)
    const = lambda b, i: (0, 0)
    return pl.pallas_call(
        _ssd_fwd_kernel,
        grid=(B, L // T),
        in_specs=[pl.BlockSpec((1, T, CONV_CH), tok),
                  pl.BlockSpec((1, T, LANES), tok),
                  pl.BlockSpec((1, T, SSD_INNER), tok),
                  pl.BlockSpec((1, SSD_INNER), const)],
        out_specs=pl.BlockSpec((1, T, SSD_INNER), tok),
        out_shape=jax.ShapeDtypeStruct((B, L, SSD_INNER), bf16),
        scratch_shapes=[pltpu.VMEM((SSD_GROUPS, SSD_STATE, GROUP_W), f32)],
        compiler_params=pltpu.CompilerParams(dimension_semantics=("arbitrary", "arbitrary"),
                                             vmem_limit_bytes=VMEM_LIMIT),
        name="ssd_fwd",
    )(xo, dtp, yb, d_skip_x)


def _na_bias_tables(rpb):
    variants = [(0, 0, 0), (2, 0, 0), (4, 0, 1), (6, 2, 2), (8, 2, 2)]
    kc = np.arange(GRID_W)[:, None]
    w = np.arange(GRID_W)[None, :]
    cs = np.clip(w - NA_COLS // 2, 0, GRID_W - NA_COLS)
    col_ok = (kc >= cs) & (kc < cs + NA_COLS)
    taps = np.arange(2 * NA_COLS - 1)[:, None, None]
    onehot = (((kc - w + NA_COLS - 1)[None] == taps) & col_ok[None]).astype(np.float32)
    base = jnp.einsum('hrk,kcw->hrcw', rpb.astype(f32) * LOG2E, onehot, precision=lax.Precision.HIGHEST)
    base = jnp.where(col_ok, base, NEG_BIG).reshape(HEAD_PAIRS, 2, 2 * NA_ROWS - 1, GRID_W, GRID_W)
    basep = jnp.pad(base, ((0, 0), (0, 0), (2, 2), (0, 0), (0, 0)), constant_values=NEG_BIG)
    n_blk = 2 * NA_ROWS + 2
    blk = jnp.concatenate([basep[:, 0, 1:1 + n_blk], basep[:, 0, 0:n_blk],
                           basep[:, 1, 1:1 + n_blk], basep[:, 1, 0:n_blk]], axis=-1)
    kr = np.arange(NA_WIN)[:, None] // GRID_W
    par = (np.arange(2 * LANES)[None, :] // GRID_W) % 2
    tabs = []
    for roff, st0, st1 in variants:
        st = np.where(par == 0, st0, st1)
        valid = (kr >= st) & (kr < st + NA_ROWS)
        j0 = NA_ROWS - roff
        rows = blk[:, j0:j0 + NA_WIN_ROWS].reshape(HEAD_PAIRS, NA_WIN, 2 * LANES)
        tabs.append(jnp.where(valid, rows, NEG_BIG))
    return jnp.stack(tabs)


def _na_kernel(q_ref, kp_ref, kc_ref, kn_ref, vp_ref, vc_ref, vn_ref, tab_ref, o_ref, kwin_ref, vwin_ref,
               *, grid_rows):
    step = pl.program_id(2)
    me = _even_lane_mask()

    kwin_ref[0:NA_HALO, :] = kp_ref[0, 0]
    kwin_ref[NA_HALO:NA_HALO + NA_QT, :] = kc_ref[0, 0]
    kwin_ref[NA_HALO + NA_QT:, :] = kn_ref[0, 0]
    ht = NA_HALO // LANES
    vwin_ref[0:ht] = vp_ref[0, 0]
    vwin_ref[ht:ht + NA_QT // LANES] = vc_ref[0, 0]
    vwin_ref[ht + NA_QT // LANES:] = vn_ref[0, 0]
    tok0 = step * NA_QT - NA_HALO

    def window(i):
        r = 2 * (step * NA_PAIRS_PER_STEP + i)
        u0 = jnp.clip(r - NA_ROWS // 2, 0, grid_rows - NA_WIN_ROWS)
        var = jnp.where(r < 4, r // 2, jnp.where(r >= grid_rows - 4, 3 + (r - (grid_rows - 4)) // 2, 2))
        return u0, var

    def run(win_rows):
        n_keys = win_rows * GRID_W
        half = n_keys // 2

        def scores(i):
            u0, var = window(i)
            q2 = q_ref[0, 0, i * LANES:(i + 1) * LANES, :]
            zero = jnp.zeros_like(q2)
            qm = jnp.concatenate([jnp.where(me, q2, zero), jnp.where(me, zero, q2)], axis=0)
            parts = []
            for hk in range(2):
                start = pl.multiple_of(u0 * GRID_W - tok0 + hk * half, BF16_ROWS)
                kk = kwin_ref[pl.ds(start, half), :]
                parts.append(_dot_nt(kk, qm) + tab_ref[var, 0, hk * half:(hk + 1) * half, :])
            return jnp.concatenate(parts, axis=0)

        def finish(i, parts, rl):
            oc = jnp.concatenate([parts[e] * rl[:, e * LANES:(e + 1) * LANES] for e in range(2)], axis=0)
            o_ref[0, 0, i * LANES:(i + 1) * LANES, :] = oc.T.astype(bf16)

        s = scores(0)
        pending = None
        for i in range(NA_PAIRS_PER_STEP):
            s_next = scores(i + 1) if i + 1 < NA_PAIRS_PER_STEP else None
            u0, _ = window(i)
            m = jnp.max(s, axis=0, keepdims=True)
            p = jnp.exp2(s - m)
            l = jnp.sum(p, axis=0, keepdims=True)
            j0 = lax.shift_right_logical(u0 * GRID_W - tok0, 7)
            vtw = jnp.concatenate([vwin_ref[j0 + j] for j in range(NA_WIN // LANES)], axis=1)
            pb = p.astype(bf16)
            if n_keys < NA_WIN:
                pb = jnp.concatenate([pb, jnp.zeros((NA_WIN - n_keys, 2 * LANES), bf16)], axis=0)
            parts = [_dot(vtw[e * HEAD_DIM:(e + 1) * HEAD_DIM, :], pb[:, e * LANES:(e + 1) * LANES])
                     for e in range(2)]
            if pending is not None:
                finish(*pending)
            pending = (i, parts, 1.0 / l)
            s = s_next
        finish(*pending)

    last = pl.num_programs(2) - 1
    pl.when(step != last)(functools.partial(run, NA_WIN_ROWS - 1))
    pl.when(step == last)(functools.partial(run, NA_WIN_ROWS))


def _na(q, k, vt, tab):
    B, _, L, _ = q.shape
    grid_rows = L // GRID_W
    QT, H = NA_QT, NA_HALO
    r = QT // H
    nh = L // H
    nv = tab.shape[0]
    cur = lambda p, b, s: (b, p, s, 0)
    prev = lambda p, b, s: (b, p, jnp.maximum(s * r - 1, 0), 0)
    nxt = lambda p, b, s: (b, p, jnp.minimum((s + 1) * r, nh - 1), 0)
    tile = lambda f: (lambda p, b, s: f(p, b, s) + (0,))
    ht = H // LANES
    return pl.pallas_call(
        functools.partial(_na_kernel, grid_rows=grid_rows),
        grid=(HEAD_PAIRS, B, L // QT),
        in_specs=[pl.BlockSpec((1, 1, QT, LANES), cur),
                  pl.BlockSpec((1, 1, H, LANES), prev),
                  pl.BlockSpec((1, 1, QT, LANES), cur),
                  pl.BlockSpec((1, 1, H, LANES), nxt),
                  pl.BlockSpec((1, 1, ht, LANES, LANES), tile(prev)),
                  pl.BlockSpec((1, 1, QT // LANES, LANES, LANES), tile(cur)),
                  pl.BlockSpec((1, 1, ht, LANES, LANES), tile(nxt)),
                  pl.BlockSpec((nv, 1, NA_WIN, 2 * LANES), lambda p, b, s: (0, p, 0, 0))],
        out_specs=pl.BlockSpec((1, 1, QT, LANES), cur),
        out_shape=jax.ShapeDtypeStruct((B, HEAD_PAIRS, L, LANES), bf16),
        scratch_shapes=[pltpu.VMEM((QT + 2 * H, LANES), bf16),
                        pltpu.VMEM((QT // LANES + 2 * ht, LANES, LANES), bf16)],
        compiler_params=pltpu.CompilerParams(dimension_semantics=("arbitrary", "arbitrary", "arbitrary"),
                                             vmem_limit_bytes=VMEM_LIMIT),
        name="na",
    )(q, k, k, k, vt, vt, vt, tab)


def _out_kernel(x_ref, ys_ref, z_ref, yn_ref, mod_ref, gs_ref, wo_ref, gf_ref, wg_ref, wu_ref, wd_ref, o_ref, a_ref):
    half = TOKEN_TILE // TOKEN_SPLIT

    def gate_norm(rows):
        gated = ys_ref[0, rows, :].astype(f32) * _silu(z_ref[0, rows, :].astype(f32))
        gms = jnp.mean(gated * gated, axis=-1, keepdims=True)
        return (gated * lax.rsqrt(gms + EPS) * gs_ref[...]).astype(bf16)

    def out_proj(rows, y_ssd):
        y_na = jnp.concatenate([yn_ref[0, p, rows, :] for p in range(HEAD_PAIRS)], axis=1)
        mix = _dot(y_ssd, wo_ref[0:SSD_INNER, :]) + _dot(y_na, wo_ref[SSD_INNER:, :])
        return x_ref[0, rows, :] + mod_ref[0, 2:3, :] * mix

    def ffn_norm(x1):
        ms = jnp.mean(x1 * x1, axis=-1, keepdims=True)
        h2 = x1 * lax.rsqrt(ms + EPS) * gf_ref[...]
        return (h2 * (1.0 + mod_ref[0, 4:5, :]) + mod_ref[0, 3:4, :]).astype(bf16)

    def ffn_block(rows, h2, c):
        cols = slice(c * FF_BLOCK, (c + 1) * FF_BLOCK)
        gate = _dot(h2, wg_ref[:, cols])
        up = _dot(h2, wu_ref[:, cols])
        a_ref[rows, cols] = (_silu(gate) * up).astype(bf16)

    def ffn_down(rows, x1):
        ff = _dot(a_ref[rows, :], wd_ref[...])
        o_ref[0, rows, :] = x1 + mod_ref[0, 5:6, :] * ff

    for s in range(TOKEN_SPLIT):
        rows = slice(s * half, (s + 1) * half)
        x1 = out_proj(rows, gate_norm(rows))
        h2 = ffn_norm(x1)
        for c in range(D_FF // FF_BLOCK):
            ffn_block(rows, h2, c)
        ffn_down(rows, x1)


def _out_ffn(x, y_pre, z, y_na, mod, g_ssd, w_out, g_ffn, w_gate, w_up, w_down):
    B, L, _ = x.shape
    T = TOKEN_TILE
    tok = lambda b, i: (b, i, 0)
    const = lambda b, i: (0, 0)
    single = pl.Buffered(1)
    return pl.pallas_call(
        _out_kernel,
        grid=(B, L // T),
        in_specs=[pl.BlockSpec((1, T, D_MODEL), tok),
                  pl.BlockSpec((1, T, SSD_INNER), tok),
                  pl.BlockSpec((1, T, SSD_INNER), tok),
                  pl.BlockSpec((1, HEAD_PAIRS, T, LANES), lambda b, i: (b, 0, i, 0)),
                  pl.BlockSpec((1, 6, D_MODEL), lambda b, i: (b, 0, 0)),
                  pl.BlockSpec((1, SSD_INNER), const),
                  pl.BlockSpec((D_MODEL, D_MODEL), const, pipeline_mode=single),
                  pl.BlockSpec((1, D_MODEL), const),
                  pl.BlockSpec((D_MODEL, D_FF), const, pipeline_mode=single),
                  pl.BlockSpec((D_MODEL, D_FF), const, pipeline_mode=single),
                  pl.BlockSpec((D_FF, D_MODEL), const, pipeline_mode=single)],
        out_specs=pl.BlockSpec((1, T, D_MODEL), tok),
        out_shape=jax.ShapeDtypeStruct((B, L, D_MODEL), f32),
        scratch_shapes=[pltpu.VMEM((T, D_FF), bf16)],
        compiler_params=pltpu.CompilerParams(dimension_semantics=("arbitrary", "arbitrary"),
                                             vmem_limit_bytes=VMEM_LIMIT),
        name="out_ffn",
    )(x, y_pre, z, y_na, mod, g_ssd, w_out, g_ffn, w_gate, w_up, w_down)


def _pad_lanes(parts):
    pieces, pos = [], 0
    for off, v in parts:
        pieces += [jnp.zeros((off - pos,), f32), v.astype(f32)]
        pos = off + v.shape[0]
    pieces.append(jnp.zeros((LANES - pos,), f32))
    return jnp.concatenate(pieces).reshape(1, LANES)


def kernel(x_prompt, x_sample, c_prompt, c_sample, w_ada, b_ada, g_mix, w_in, conv_w, conv_b, dt_bias_fwd, dt_bias_bwd, a_log_fwd, a_log_bwd, d_skip, g_ssd, g_q, g_k, rpb, w_out, g_ffn, w_gate, w_up, w_down):
    l = 0
    nb_p = x_prompt.shape[0]
    nb_s = x_sample.shape[0]

    c_all = jnp.concatenate([c_prompt, c_sample, jnp.zeros((8 - nb_p - nb_s, D_MODEL), f32)], axis=0)
    mod = _modulation(c_all, w_ada[l], b_ada[l]).reshape(8, 6, D_MODEL)

    o1 = SSD_INNER
    o2 = o1 + CONV_CH
    o3 = o2 + 2 * SSD_HEADS
    o4 = o3 + NA_INNER
    o5 = o4 + NA_INNER
    wi = w_in[l]
    w_main = jnp.concatenate([wi[:, :o2], wi[:, o3:o5]], axis=1).astype(bf16)
    w_dt16 = wi[:, o2:o3]
    gap = jnp.zeros((D_MODEL, LANES // 2 - 2 * SSD_HEADS), f32)
    w_dt = jnp.concatenate([w_dt16, gap, w_dt16, gap], axis=1).astype(bf16)
    w_vt = wi[:, o5:].T.astype(bf16)
    dtb = _pad_lanes([(0, dt_bias_fwd[l]), (8, dt_bias_bwd[l]), (64, dt_bias_fwd[l]), (72, dt_bias_bwd[l])])
    alog = _pad_lanes([(0, a_log_fwd[l]), (8, a_log_bwd[l])])
    gq = jnp.tile(g_q[l].astype(f32) * (HEAD_DIM ** -0.5 * LOG2E), NA_HEADS).reshape(1, NA_INNER)
    gk = jnp.tile(g_k[l].astype(f32), NA_HEADS).reshape(1, NA_INNER)
    dsk = jnp.repeat(d_skip[l].astype(f32), HEAD_DIM).reshape(1, SSD_INNER)
    tab = _na_bias_tables(rpb[l])
    wo = w_out[l].astype(bf16)
    wg = w_gate[l].astype(bf16)
    wu = w_up[l].astype(bf16)
    wd = w_down[l].astype(bf16)
    row = lambda v: v.astype(f32).reshape(1, -1)

    def trunk(x, m):
        z, xbc, dtp, q, k, vt = _in_proj(x, m, row(g_mix[l]), w_main, w_dt, w_vt, gq, gk, dtb, alog)
        xo, yb = _ssd_bwd(xbc, dtp, conv_w[l].astype(f32), row(conv_b[l]))
        y_pre = _ssd_fwd(xo, dtp, yb, dsk)
        y_na = _na(q, k, vt, tab)
        return _out_ffn(x, y_pre, z, y_na, m, row(g_ssd[l]), wo, row(g_ffn[l]), wg, wu, wd)

    return (trunk(x_prompt, mod[:nb_p]), trunk(x_sample, mod[nb_p:nb_p + nb_s]))
```

```python
import functools
import math

import numpy as np
import jax
import jax.numpy as jnp
from jax import lax
from jax.experimental import pallas as pl
from jax.experimental.pallas import tpu as pltpu

D_MODEL = 1024
HEAD_DIM = 64
SSD_HEADS = 8
SSD_INNER = SSD_HEADS * HEAD_DIM
SSD_GROUPS = 2
SSD_STATE = 128
CONV_K = 5
CHUNK = 128
NA_HEADS = 8
NA_INNER = NA_HEADS * HEAD_DIM
NA_ROWS = 8
NA_COLS = 16
GRID_W = 64
CONV_CH = SSD_INNER + 2 * SSD_GROUPS * SSD_STATE
D_FF = 2816
EPS = 1e-6

LANES = 128
BF16_ROWS = 16
CONV_HALO = BF16_ROWS
CONV_ROWS = 128
CONV_COLS = 256
GROUP_W = (SSD_HEADS // SSD_GROUPS) * HEAD_DIM
HEAD_PAIRS = NA_HEADS // 2
LOG2E = math.log2(math.e)
NEG_BIG = -1e30

TOKEN_TILE = 1024
TOKEN_SPLIT = 1
SSD_TILE = 1024
NA_WIN_ROWS = 10
NA_WIN = NA_WIN_ROWS * GRID_W
NA_PAIRS_PER_STEP = 16
NA_QT = NA_PAIRS_PER_STEP * 2 * GRID_W
NA_HALO = 256
FF_BLOCK = 256
VMEM_LIMIT = 56 * 1024 * 1024

f32 = jnp.float32
bf16 = jnp.bfloat16


def _dot(a, b):
    return jnp.dot(a, b, preferred_element_type=f32)


def _dot_nt(a, b):
    return lax.dot_general(a, b, (((1,), (1,)), ((), ())), preferred_element_type=f32)


def _split3(v):
    p1 = v.astype(bf16)
    r1 = v - p1.astype(f32)
    p2 = r1.astype(bf16)
    p3 = (r1 - p2.astype(f32)).astype(bf16)
    return p1, p2, p3


def _even_lane_mask():
    return lax.broadcasted_iota(jnp.int32, (1, LANES), 1) < HEAD_DIM


def _silu(v):
    return v * jax.nn.sigmoid(v)


def _mod_kernel(ct_ref, w_ref, b_ref, o_ref, *, n_rows):
    st = _silu(ct_ref[...])
    w = w_ref[...]
    rows = [jnp.sum(w * st[:, b:b + 1], axis=0, keepdims=True) for b in range(n_rows)]
    rows.append(jnp.zeros((8 - n_rows, w.shape[1]), f32))
    o_ref[...] = jnp.concatenate(rows, axis=0) + b_ref[...]


def _modulation(c_t, w_ada, b_ada, n_rows):
    n = 6 * D_MODEL
    return pl.pallas_call(
        functools.partial(_mod_kernel, n_rows=n_rows),
        grid=(n // D_MODEL,),
        in_specs=[pl.BlockSpec((D_MODEL, 8), lambda j: (0, 0)),
                  pl.BlockSpec((D_MODEL, D_MODEL), lambda j: (0, j)),
                  pl.BlockSpec((1, D_MODEL), lambda j: (0, j))],
        out_specs=pl.BlockSpec((8, D_MODEL), lambda j: (0, j)),
        out_shape=jax.ShapeDtypeStruct((8, n), f32),
        name="mod",
    )(c_t, w_ada, b_ada.reshape(1, n))


def _head_rmsnorm(v, gain):
    me = _even_lane_mask()
    s = v * v
    se = jnp.sum(jnp.where(me, s, 0.0), axis=-1, keepdims=True)
    so = jnp.sum(jnp.where(me, 0.0, s), axis=-1, keepdims=True)
    ms = jnp.where(me, se, so) * (1.0 / HEAD_DIM)
    return v * lax.rsqrt(ms + EPS) * gain


def _in_kernel(x_ref, mod_ref, g_ref, w_ref, wdt_ref, wvt_ref, gq_ref, gk_ref, dtb_ref, alog_ref,
               z_ref, xbc_ref, dtp_ref, q_ref, k_ref, vt_ref):
    half = TOKEN_TILE // TOKEN_SPLIT

    def norm_mod(rows):
        x = x_ref[0, rows, :]
        ms = jnp.mean(x * x, axis=-1, keepdims=True)
        y = x * lax.rsqrt(ms + EPS) * g_ref[...]
        return (y * (1.0 + mod_ref[0, 1:2, :]) + mod_ref[0, 0:1, :]).astype(bf16)

    def proj_xbc(rows, h):
        xbc_ref[0, rows, :] = _dot(h, w_ref[:, SSD_INNER:SSD_INNER + CONV_CH]).astype(bf16)

    def proj_z(rows, h):
        z_ref[0, rows, :] = _dot(h, w_ref[:, 0:SSD_INNER]).astype(bf16)

    def proj_head_normed(rows, h, dst_ref, gain_ref, col0):
        full = _dot(h, w_ref[:, col0:col0 + NA_INNER])
        for j in range(HEAD_PAIRS):
            sl = slice(j * LANES, (j + 1) * LANES)
            dst_ref[0, j, rows, :] = _head_rmsnorm(full[:, sl], gain_ref[:, sl]).astype(bf16)

    def proj_v_dt(rows, h):
        vt = _dot_nt(wvt_ref[...], h).astype(bf16)
        tb0 = rows.start // LANES
        for p in range(HEAD_PAIRS):
            for tb in range(half // LANES):
                vt_ref[0, p, tb0 + tb] = vt[p * LANES:(p + 1) * LANES, tb * LANES:(tb + 1) * LANES]
        raw = _dot(h, wdt_ref[...]) + dtb_ref[...]
        dt = jnp.maximum(raw, 0.0) + jnp.log1p(jnp.exp(-jnp.abs(raw)))
        lane = lax.broadcasted_iota(jnp.int32, (1, LANES), 1)
        a_neg = -jnp.exp(alog_ref[...])
        mul = jnp.where(lane < 2 * SSD_HEADS, a_neg,
                        jnp.where((lane >= 64) & (lane < 64 + 2 * SSD_HEADS), 1.0, 0.0))
        dtp_ref[0, rows, :] = dt * mul

    o = SSD_INNER + CONV_CH
    for s in range(TOKEN_SPLIT):
        rows = slice(s * half, (s + 1) * half)
        h = norm_mod(rows)
        proj_xbc(rows, h)
        proj_z(rows, h)
        proj_head_normed(rows, h, q_ref, gq_ref, o)
        proj_head_normed(rows, h, k_ref, gk_ref, o + NA_INNER)
        proj_v_dt(rows, h)


def _in_proj(x, mod, g_mix, w_main, w_dt, w_vt, gq, gk, dt_bias, a_log):
    B, L, _ = x.shape
    T = TOKEN_TILE
    nw = w_main.shape[1]
    const = lambda b, i: (0, 0)
    tok = lambda b, i: (b, i, 0)
    return pl.pallas_call(
        _in_kernel,
        grid=(B, L // T),
        in_specs=[pl.BlockSpec((1, T, D_MODEL), tok),
                  pl.BlockSpec((1, 6, D_MODEL), lambda b, i: (b, 0, 0)),
                  pl.BlockSpec((1, D_MODEL), const),
                  pl.BlockSpec((D_MODEL, nw), const),
                  pl.BlockSpec((D_MODEL, LANES), const),
                  pl.BlockSpec((NA_INNER, D_MODEL), const),
                  pl.BlockSpec((1, NA_INNER), const),
                  pl.BlockSpec((1, NA_INNER), const),
                  pl.BlockSpec((1, LANES), const),
                  pl.BlockSpec((1, LANES), const)],
        out_specs=[pl.BlockSpec((1, T, SSD_INNER), tok),
                   pl.BlockSpec((1, T, CONV_CH), tok),
                   pl.BlockSpec((1, T, LANES), tok),
                   pl.BlockSpec((1, HEAD_PAIRS, T, LANES), lambda b, i: (b, 0, i, 0)),
                   pl.BlockSpec((1, HEAD_PAIRS, T, LANES), lambda b, i: (b, 0, i, 0)),
                   pl.BlockSpec((1, HEAD_PAIRS, T // LANES, LANES, LANES), lambda b, i: (b, 0, i, 0, 0))],
        out_shape=[jax.ShapeDtypeStruct((B, L, SSD_INNER), bf16),
                   jax.ShapeDtypeStruct((B, L, CONV_CH), bf16),
                   jax.ShapeDtypeStruct((B, L, LANES), f32),
                   jax.ShapeDtypeStruct((B, HEAD_PAIRS, L, LANES), bf16),
                   jax.ShapeDtypeStruct((B, HEAD_PAIRS, L, LANES), bf16),
                   jax.ShapeDtypeStruct((B, HEAD_PAIRS, L // LANES, LANES, LANES), bf16)],
        compiler_params=pltpu.CompilerParams(dimension_semantics=("arbitrary", "arbitrary"),
                                             vmem_limit_bytes=VMEM_LIMIT),
        name="in_proj",
    )(x, mod, g_mix, w_main, w_dt, w_vt, gq, gk, dt_bias, a_log)


def _head_selector(base):
    r = lax.broadcasted_iota(jnp.int32, (LANES, SSD_INNER), 0)
    c = lax.broadcasted_iota(jnp.int32, (LANES, SSD_INNER), 1)
    return (lax.shift_right_logical(c, 6) == r - base).astype(bf16)


def _expand_heads(v, sel):
    hi = v.astype(bf16)
    lo = (v - hi.astype(f32)).astype(bf16)
    return _dot(hi, sel) + _dot(lo, sel)


def _cumsum_matrix():
    row = lax.broadcasted_iota(jnp.int32, (CHUNK, CHUNK), 0)
    col = lax.broadcasted_iota(jnp.int32, (CHUNK, CHUNK), 1)
    return (col <= row).astype(bf16)


def _chunk_decays(p, tri):
    p1, p2, p3 = _split3(p)
    incl = _dot(tri, p1) + _dot(tri, p2) + _dot(tri, p3)
    excl = incl - p
    dts = pltpu.roll(p, 64, 1)
    tot = incl[CHUNK - 1:CHUNK, :]
    is_fwd = lax.broadcasted_iota(jnp.int32, (1, LANES), 1) < SSD_HEADS
    w = jnp.exp(jnp.where(is_fwd, tot - incl, excl)) * dts
    sc = jnp.exp(jnp.where(is_fwd, incl, tot - excl))
    return incl, excl, dts, w, sc, jnp.exp(tot)


def _b_transposed(xo_ref, rows, g):
    b_g = xo_ref[0, rows, SSD_INNER + g * SSD_STATE:SSD_INNER + (g + 1) * SSD_STATE].astype(f32)
    return b_g.T.astype(bf16)


def _expand_decays(w, sc, dec, sel):
    return (_expand_heads(w, sel), _expand_heads(sc, sel),
            _expand_heads(jnp.broadcast_to(dec, (8, LANES)), sel)[0:1])


def _state_step(xo_ref, rows, st_ref, expanded, b_t):
    w_x, sc_x, dec_x = expanded
    outs = []
    for g in range(SSD_GROUPS):
        gcols = slice(g * GROUP_W, (g + 1) * GROUP_W)
        xs_g = xo_ref[0, rows, g * GROUP_W:(g + 1) * GROUP_W].astype(f32)
        c_g = xo_ref[0, rows, SSD_INNER + (SSD_GROUPS + g) * SSD_STATE:
                     SSD_INNER + (SSD_GROUPS + g + 1) * SSD_STATE]
        s_g = st_ref[g]
        y_off = _dot(c_g, s_g.astype(bf16)) * sc_x[:, gcols]
        xw = (xs_g * w_x[:, gcols]).astype(bf16)
        st_ref[g] = s_g * dec_x[:, gcols] + _dot(b_t[g], xw)
        outs.append(y_off)
    return jnp.concatenate(outs, axis=1)


def _ssd_bwd_kernel(xc_ref, xp_ref, xn_ref, dtp_ref, cw_ref, cb_ref, xo_ref, yb_ref, win_ref, st_ref, *, n_tiles):
    i = pl.program_id(1)
    t = n_tiles - 1 - i

    @pl.when(i == 0)
    def _():
        st_ref[...] = jnp.zeros_like(st_ref)

    H = CONV_HALO
    win_ref[0:H, :] = jnp.where(t > 0, xp_ref[0], jnp.zeros_like(xp_ref[0]))
    win_ref[H:H + SSD_TILE, :] = xc_ref[0]
    win_ref[H + SSD_TILE:, :] = jnp.where(t < n_tiles - 1, xn_ref[0], jnp.zeros_like(xn_ref[0]))
    r_i = lax.broadcasted_iota(jnp.int32, (CONV_ROWS, CONV_ROWS + 2 * H), 0)
    c_i = lax.broadcasted_iota(jnp.int32, (CONV_ROWS, CONV_ROWS + 2 * H), 1)
    mid = CONV_K // 2
    shift = {k: (c_i == r_i + H + k - mid).astype(bf16) for k in range(CONV_K) if k != mid}
    for c in range(SSD_TILE // CONV_ROWS):
        for cb in range(CONV_CH // CONV_COLS):
            cols = slice(cb * CONV_COLS, (cb + 1) * CONV_COLS)
            xw = win_ref[c * CONV_ROWS:(c + 1) * CONV_ROWS + 2 * H, cols]
            centre = win_ref[H + c * CONV_ROWS:H + (c + 1) * CONV_ROWS, cols].astype(f32)
            acc = cb_ref[:, cols] + centre * cw_ref[mid:mid + 1, cols]
            for k in shift:
                acc = acc + _dot(shift[k], xw) * cw_ref[k:k + 1, cols]
            xo_ref[0, c * CONV_ROWS:(c + 1) * CONV_ROWS, cols] = _silu(acc).astype(bf16)

    sel = _head_selector(SSD_HEADS)
    tri = _cumsum_matrix()

    def body(j, carry):
        c = SSD_TILE // CHUNK - 1 - j
        rows = pl.ds(pl.multiple_of(c * CHUNK, CHUNK), CHUNK)
        _, _, _, w, sc, dec = _chunk_decays(dtp_ref[0, rows, :], tri)
        b_t = [_b_transposed(xo_ref, rows, g) for g in range(SSD_GROUPS)]
        yb_ref[0, rows, :] = _state_step(xo_ref, rows, st_ref, _expand_decays(w, sc, dec, sel), b_t).astype(bf16)
        return carry

    lax.fori_loop(0, SSD_TILE // CHUNK, body, 0, unroll=True)


def _ssd_bwd(xbc, dtp, conv_w, conv_b):
    B, L, _ = xbc.shape
    T = SSD_TILE
    nT = L // T
    hb = T // CONV_HALO
    cur = lambda b, i: (b, nT - 1 - i, 0)
    prev = lambda b, i: (b, jnp.maximum((nT - 1 - i) * hb - 1, 0), 0)
    nxt = lambda b, i: (b, jnp.minimum((nT - i) * hb, L // CONV_HALO - 1), 0)
    const = lambda b, i: (0, 0)
    return pl.pallas_call(
        functools.partial(_ssd_bwd_kernel, n_tiles=nT),
        grid=(B, nT),
        in_specs=[pl.BlockSpec((1, T, CONV_CH), cur),
                  pl.BlockSpec((1, CONV_HALO, CONV_CH), prev),
                  pl.BlockSpec((1, CONV_HALO, CONV_CH), nxt),
                  pl.BlockSpec((1, T, LANES), cur),
                  pl.BlockSpec((CONV_K, CONV_CH), const),
                  pl.BlockSpec((1, CONV_CH), const)],
        out_specs=[pl.BlockSpec((1, T, CONV_CH), cur),
                   pl.BlockSpec((1, T, SSD_INNER), cur)],
        out_shape=[jax.ShapeDtypeStruct((B, L, CONV_CH), bf16),
                   jax.ShapeDtypeStruct((B, L, SSD_INNER), bf16)],
        scratch_shapes=[pltpu.VMEM((T + 2 * CONV_HALO, CONV_CH), bf16),
                        pltpu.VMEM((SSD_GROUPS, SSD_STATE, GROUP_W), f32)],
        compiler_params=pltpu.CompilerParams(dimension_semantics=("arbitrary", "arbitrary"),
                                             vmem_limit_bytes=VMEM_LIMIT),
        name="ssd_bwd",
    )(xbc, xbc, xbc, dtp, conv_w, conv_b)


def _ssd_fwd_kernel(xo_ref, dtp_ref, yb_ref, dsk_ref, o_ref, st_ref):
    @pl.when(pl.program_id(1) == 0)
    def _():
        st_ref[...] = jnp.zeros_like(st_ref)

    row = lax.broadcasted_iota(jnp.int32, (CHUNK, CHUNK), 0)
    col = lax.broadcasted_iota(jnp.int32, (CHUNK, CHUNK), 1)
    low = col < row
    diag = col == row
    me = _even_lane_mask()
    sel = _head_selector(0)
    tri = _cumsum_matrix()

    def body(c, carry):
        rows = pl.ds(pl.multiple_of(c * CHUNK, CHUNK), CHUNK)
        incl, excl, dts, w, sc, dec = _chunk_decays(dtp_ref[0, rows, :], tri)
        b_t = [_b_transposed(xo_ref, rows, g) for g in range(SSD_GROUPS)]
        y = _state_step(xo_ref, rows, st_ref, _expand_decays(w, sc, dec, sel), b_t)

        incl_t = incl.T
        excl_t = excl.T
        dts_t = dts.T
        y_diag = []
        for g in range(SSD_GROUPS):
            c_g = xo_ref[0, rows, SSD_INNER + (SSD_GROUPS + g) * SSD_STATE:
                         SSD_INNER + (SSD_GROUPS + g + 1) * SSD_STATE]
            cb = _dot(c_g, b_t[g])
            for pr in range(2):
                xs_pair = xo_ref[0, rows, g * GROUP_W + pr * LANES:g * GROUP_W + (pr + 1) * LANES]
                acc = None
                for e in range(2):
                    hd = 4 * g + 2 * pr + e
                    af_c = jnp.broadcast_to(incl[:, hd:hd + 1], (CHUNK, CHUNK))
                    eb_c = jnp.broadcast_to(excl[:, SSD_HEADS + hd:SSD_HEADS + hd + 1], (CHUNK, CHUNK))
                    af_r = incl_t[hd:hd + 1, :]
                    eb_r = excl_t[SSD_HEADS + hd:SSD_HEADS + hd + 1, :]
                    df_r = dts_t[hd:hd + 1, :]
                    db_r = dts_t[SSD_HEADS + hd:SSD_HEADS + hd + 1, :]
                    arg = jnp.where(low, af_c - af_r, eb_r - eb_c)
                    coef = jnp.where(low, df_r, jnp.where(diag, df_r + db_r, db_r))
                    wm = (cb * jnp.exp(arg) * coef).astype(bf16)
                    xm = jnp.where(me, xs_pair, jnp.zeros_like(xs_pair)) if e == 0 else \
                        jnp.where(me, jnp.zeros_like(xs_pair), xs_pair)
                    d = _dot(wm, xm)
                    acc = d if acc is None else acc + d
                y_diag.append(acc)
        xs = xo_ref[0, rows, 0:SSD_INNER].astype(f32)
        y = y + jnp.concatenate(y_diag, axis=1) + yb_ref[0, rows, :].astype(f32) + xs * dsk_ref[...]
        o_ref[0, rows, :] = y.astype(bf16)
        return carry

    lax.fori_loop(0, SSD_TILE // CHUNK, body, 0, unroll=True)


def _ssd_fwd(xo, dtp, yb, d_skip_x):
    B, L, _ = xo.shape
    T = SSD_TILE
    tok = lambda b, i: (b, i, 0)
    const = lambda b, i: (0, 0)
    return pl.pallas_call(
        _ssd_fwd_kernel,
        grid=(B, L // T),
        in_specs=[pl.BlockSpec((1, T, CONV_CH), tok),
                  pl.BlockSpec((1, T, LANES), tok),
                  pl.BlockSpec((1, T, SSD_INNER), tok),
                  pl.BlockSpec((1, SSD_INNER), const)],
        out_specs=pl.BlockSpec((1, T, SSD_INNER), tok),
        out_shape=jax.ShapeDtypeStruct((B, L, SSD_INNER), bf16),
        scratch_shapes=[pltpu.VMEM((SSD_GROUPS, SSD_STATE, GROUP_W), f32)],
        compiler_params=pltpu.CompilerParams(dimension_semantics=("arbitrary", "arbitrary"),
                                             vmem_limit_bytes=VMEM_LIMIT),
        name="ssd_fwd",
    )(xo, dtp, yb, d_skip_x)


def _na_bias_tables(rpb):
    variants = [(0, 0, 0), (2, 0, 0), (4, 0, 1), (6, 2, 2), (8, 2, 2)]
    kc = np.arange(GRID_W)[:, None]
    w = np.arange(GRID_W)[None, :]
    cs = np.clip(w - NA_COLS // 2, 0, GRID_W - NA_COLS)
    col_ok = (kc >= cs) & (kc < cs + NA_COLS)
    taps = np.arange(2 * NA_COLS - 1)[:, None, None]
    onehot = (((kc - w + NA_COLS - 1)[None] == taps) & col_ok[None]).astype(np.float32)
    base = jnp.einsum('hrk,kcw->hrcw', rpb.astype(f32) * LOG2E, onehot, precision=lax.Precision.HIGHEST)
    base = jnp.where(col_ok, base, NEG_BIG).reshape(HEAD_PAIRS, 2, 2 * NA_ROWS - 1, GRID_W, GRID_W)
    basep = jnp.pad(base, ((0, 0), (0, 0), (2, 2), (0, 0), (0, 0)), constant_values=NEG_BIG)
    n_blk = 2 * NA_ROWS + 2
    blk = jnp.concatenate([basep[:, 0, 1:1 + n_blk], basep[:, 0, 0:n_blk],
                           basep[:, 1, 1:1 + n_blk], basep[:, 1, 0:n_blk]], axis=-1)
    kr = np.arange(NA_WIN)[:, None] // GRID_W
    par = (np.arange(2 * LANES)[None, :] // GRID_W) % 2
    tabs = []
    for roff, st0, st1 in variants:
        st = np.where(par == 0, st0, st1)
        valid = (kr >= st) & (kr < st + NA_ROWS)
        j0 = NA_ROWS - roff
        rows = blk[:, j0:j0 + NA_WIN_ROWS].reshape(HEAD_PAIRS, NA_WIN, 2 * LANES)
        tabs.append(jnp.where(valid, rows, NEG_BIG))
    return jnp.stack(tabs)


def _na_kernel(q_ref, kp_ref, kc_ref, kn_ref, vp_ref, vc_ref, vn_ref, tab_ref, o_ref, kwin_ref, vwin_ref,
               *, grid_rows):
    step = pl.program_id(2)
    me = _even_lane_mask()

    kwin_ref[0:NA_HALO, :] = kp_ref[0, 0]
    kwin_ref[NA_HALO:NA_HALO + NA_QT, :] = kc_ref[0, 0]
    kwin_ref[NA_HALO + NA_QT:, :] = kn_ref[0, 0]
    ht = NA_HALO // LANES
    vwin_ref[0:ht] = vp_ref[0, 0]
    vwin_ref[ht:ht + NA_QT // LANES] = vc_ref[0, 0]
    vwin_ref[ht + NA_QT // LANES:] = vn_ref[0, 0]
    tok0 = step * NA_QT - NA_HALO

    def window(i):
        r = 2 * (step * NA_PAIRS_PER_STEP + i)
        u0 = jnp.clip(r - NA_ROWS // 2, 0, grid_rows - NA_WIN_ROWS)
        var = jnp.where(r < 4, r // 2, jnp.where(r >= grid_rows - 4, 3 + (r - (grid_rows - 4)) // 2, 2))
        return u0, var

    def run(win_rows):
        n_keys = win_rows * GRID_W
        half = n_keys // 2

        def scores(i):
            u0, var = window(i)
            q2 = q_ref[0, 0, i * LANES:(i + 1) * LANES, :]
            zero = jnp.zeros_like(q2)
            qm = jnp.concatenate([jnp.where(me, q2, zero), jnp.where(me, zero, q2)], axis=0)
            parts = []
            for hk in range(2):
                start = pl.multiple_of(u0 * GRID_W - tok0 + hk * half, BF16_ROWS)
                kk = kwin_ref[pl.ds(start, half), :]
                parts.append(_dot_nt(kk, qm) + tab_ref[var, 0, hk * half:(hk + 1) * half, :])
            return jnp.concatenate(parts, axis=0)

        def finish(i, parts, rl):
            oc = jnp.concatenate([parts[e] * rl[:, e * LANES:(e + 1) * LANES] for e in range(2)], axis=0)
            o_ref[0, 0, i * LANES:(i + 1) * LANES, :] = oc.T.astype(bf16)

        s = scores(0)
        pending = None
        for i in range(NA_PAIRS_PER_STEP):
            s_next = scores(i + 1) if i + 1 < NA_PAIRS_PER_STEP else None
            u0, _ = window(i)
            m = jnp.max(s, axis=0, keepdims=True)
            p = jnp.exp2(s - m)
            l = jnp.sum(p, axis=0, keepdims=True)
            j0 = lax.shift_right_logical(u0 * GRID_W - tok0, 7)
            vtw = jnp.concatenate([vwin_ref[j0 + j] for j in range(NA_WIN // LANES)], axis=1)
            pb = p.astype(bf16)
            if n_keys < NA_WIN:
                pb = jnp.concatenate([pb, jnp.zeros((NA_WIN - n_keys, 2 * LANES), bf16)], axis=0)
            parts = [_dot(vtw[e * HEAD_DIM:(e + 1) * HEAD_DIM, :], pb[:, e * LANES:(e + 1) * LANES])
                     for e in range(2)]
            if pending is not None:
                finish(*pending)
            pending = (i, parts, 1.0 / l)
            s = s_next
        finish(*pending)

    last = pl.num_programs(2) - 1
    pl.when(step != last)(functools.partial(run, NA_WIN_ROWS - 1))
    pl.when(step == last)(functools.partial(run, NA_WIN_ROWS))


def _na(q, k, vt, tab):
    B, _, L, _ = q.shape
    grid_rows = L // GRID_W
    QT, H = NA_QT, NA_HALO
    r = QT // H
    nh = L // H
    nv = tab.shape[0]
    cur = lambda p, b, s: (b, p, s, 0)
    prev = lambda p, b, s: (b, p, jnp.maximum(s * r - 1, 0), 0)
    nxt = lambda p, b, s: (b, p, jnp.minimum((s + 1) * r, nh - 1), 0)
    tile = lambda f: (lambda p, b, s: f(p, b, s) + (0,))
    ht = H // LANES
    return pl.pallas_call(
        functools.partial(_na_kernel, grid_rows=grid_rows),
        grid=(HEAD_PAIRS, B, L // QT),
        in_specs=[pl.BlockSpec((1, 1, QT, LANES), cur),
                  pl.BlockSpec((1, 1, H, LANES), prev),
                  pl.BlockSpec((1, 1, QT, LANES), cur),
                  pl.BlockSpec((1, 1, H, LANES), nxt),
                  pl.BlockSpec((1, 1, ht, LANES, LANES), tile(prev)),
                  pl.BlockSpec((1, 1, QT // LANES, LANES, LANES), tile(cur)),
                  pl.BlockSpec((1, 1, ht, LANES, LANES), tile(nxt)),
                  pl.BlockSpec((nv, 1, NA_WIN, 2 * LANES), lambda p, b, s: (0, p, 0, 0))],
        out_specs=pl.BlockSpec((1, 1, QT, LANES), cur),
        out_shape=jax.ShapeDtypeStruct((B, HEAD_PAIRS, L, LANES), bf16),
        scratch_shapes=[pltpu.VMEM((QT + 2 * H, LANES), bf16),
                        pltpu.VMEM((QT // LANES + 2 * ht, LANES, LANES), bf16)],
        compiler_params=pltpu.CompilerParams(dimension_semantics=("arbitrary", "arbitrary", "arbitrary"),
                                             vmem_limit_bytes=VMEM_LIMIT),
        name="na",
    )(q, k, k, k, vt, vt, vt, tab)


def _out_kernel(x_ref, ys_ref, z_ref, yn_ref, mod_ref, gs_ref, wo_ref, gf_ref, wg_ref, wu_ref, wd_ref, o_ref, a_ref):
    half = TOKEN_TILE // TOKEN_SPLIT

    def gate_norm(rows):
        gated = ys_ref[0, rows, :].astype(f32) * _silu(z_ref[0, rows, :].astype(f32))
        gms = jnp.mean(gated * gated, axis=-1, keepdims=True)
        return (gated * lax.rsqrt(gms + EPS) * gs_ref[...]).astype(bf16)

    def out_proj(rows, y_ssd):
        y_na = jnp.concatenate([yn_ref[0, p, rows, :] for p in range(HEAD_PAIRS)], axis=1)
        mix = _dot(y_ssd, wo_ref[0:SSD_INNER, :]) + _dot(y_na, wo_ref[SSD_INNER:, :])
        return x_ref[0, rows, :] + mod_ref[0, 2:3, :] * mix

    def ffn_norm(x1):
        ms = jnp.mean(x1 * x1, axis=-1, keepdims=True)
        h2 = x1 * lax.rsqrt(ms + EPS) * gf_ref[...]
        return (h2 * (1.0 + mod_ref[0, 4:5, :]) + mod_ref[0, 3:4, :]).astype(bf16)

    def ffn_block(rows, h2, c):
        cols = slice(c * FF_BLOCK, (c + 1) * FF_BLOCK)
        gate = _dot(h2, wg_ref[:, cols])
        up = _dot(h2, wu_ref[:, cols])
        a_ref[rows, cols] = (_silu(gate) * up).astype(bf16)

    def ffn_down(rows, x1):
        ff = _dot(a_ref[rows, :], wd_ref[...])
        o_ref[0, rows, :] = x1 + mod_ref[0, 5:6, :] * ff

    for s in range(TOKEN_SPLIT):
        rows = slice(s * half, (s + 1) * half)
        x1 = out_proj(rows, gate_norm(rows))
        h2 = ffn_norm(x1)
        for c in range(D_FF // FF_BLOCK):
            ffn_block(rows, h2, c)
        ffn_down(rows, x1)


def _out_ffn(x, y_pre, z, y_na, mod, g_ssd, w_out, g_ffn, w_gate, w_up, w_down):
    B, L, _ = x.shape
    T = TOKEN_TILE
    tok = lambda b, i: (b, i, 0)
    const = lambda b, i: (0, 0)
    single = pl.Buffered(1)
    return pl.pallas_call(
        _out_kernel,
        grid=(B, L // T),
        in_specs=[pl.BlockSpec((1, T, D_MODEL), tok),
                  pl.BlockSpec((1, T, SSD_INNER), tok),
                  pl.BlockSpec((1, T, SSD_INNER), tok),
                  pl.BlockSpec((1, HEAD_PAIRS, T, LANES), lambda b, i: (b, 0, i, 0)),
                  pl.BlockSpec((1, 6, D_MODEL), lambda b, i: (b, 0, 0)),
                  pl.BlockSpec((1, SSD_INNER), const),
                  pl.BlockSpec((D_MODEL, D_MODEL), const, pipeline_mode=single),
                  pl.BlockSpec((1, D_MODEL), const),
                  pl.BlockSpec((D_MODEL, D_FF), const, pipeline_mode=single),
                  pl.BlockSpec((D_MODEL, D_FF), const, pipeline_mode=single),
                  pl.BlockSpec((D_FF, D_MODEL), const, pipeline_mode=single)],
        out_specs=pl.BlockSpec((1, T, D_MODEL), tok),
        out_shape=jax.ShapeDtypeStruct((B, L, D_MODEL), f32),
        scratch_shapes=[pltpu.VMEM((T, D_FF), bf16)],
        compiler_params=pltpu.CompilerParams(dimension_semantics=("arbitrary", "arbitrary"),
                                             vmem_limit_bytes=VMEM_LIMIT),
        name="out_ffn",
    )(x, y_pre, z, y_na, mod, g_ssd, w_out, g_ffn, w_gate, w_up, w_down)


def _pad_lanes(parts):
    pieces, pos = [], 0
    for off, v in parts:
        pieces += [jnp.zeros((off - pos,), f32), v.astype(f32)]
        pos = off + v.shape[0]
    pieces.append(jnp.zeros((LANES - pos,), f32))
    return jnp.concatenate(pieces).reshape(1, LANES)


def kernel(x_prompt, x_sample, c_prompt, c_sample, w_ada, b_ada, g_mix, w_in, conv_w, conv_b, dt_bias_fwd, dt_bias_bwd, a_log_fwd, a_log_bwd, d_skip, g_ssd, g_q, g_k, rpb, w_out, g_ffn, w_gate, w_up, w_down):
    l = 0
    nb_p = x_prompt.shape[0]
    nb_s = x_sample.shape[0]

    c_all = jnp.concatenate([c_prompt, c_sample, jnp.zeros((8 - nb_p - nb_s, D_MODEL), f32)], axis=0)
    mod = _modulation(c_all.T, w_ada[l], b_ada[l], nb_p + nb_s).reshape(8, 6, D_MODEL)

    o1 = SSD_INNER
    o2 = o1 + CONV_CH
    o3 = o2 + 2 * SSD_HEADS
    o4 = o3 + NA_INNER
    o5 = o4 + NA_INNER
    wi = w_in[l]
    w_main = jnp.concatenate([wi[:, :o2], wi[:, o3:o5]], axis=1).astype(bf16)
    w_dt16 = wi[:, o2:o3]
    gap = jnp.zeros((D_MODEL, LANES // 2 - 2 * SSD_HEADS), f32)
    w_dt = jnp.concatenate([w_dt16, gap, w_dt16, gap], axis=1).astype(bf16)
    w_vt = wi[:, o5:].T.astype(bf16)
    dtb = _pad_lanes([(0, dt_bias_fwd[l]), (8, dt_bias_bwd[l]), (64, dt_bias_fwd[l]), (72, dt_bias_bwd[l])])
    alog = _pad_lanes([(0, a_log_fwd[l]), (8, a_log_bwd[l])])
    gq = jnp.tile(g_q[l].astype(f32) * (HEAD_DIM ** -0.5 * LOG2E), NA_HEADS).reshape(1, NA_INNER)
    gk = jnp.tile(g_k[l].astype(f32), NA_HEADS).reshape(1, NA_INNER)
    dsk = jnp.repeat(d_skip[l].astype(f32), HEAD_DIM).reshape(1, SSD_INNER)
    tab = _na_bias_tables(rpb[l])
    wo = w_out[l].astype(bf16)
    wg = w_gate[l].astype(bf16)
    wu = w_up[l].astype(bf16)
    wd = w_down[l].astype(bf16)
    row = lambda v: v.astype(f32).reshape(1, -1)

    def trunk(x, m):
        z, xbc, dtp, q, k, vt = _in_proj(x, m, row(g_mix[l]), w_main, w_dt, w_vt, gq, gk, dtb, alog)
        xo, yb = _ssd_bwd(xbc, dtp, conv_w[l].astype(f32), row(conv_b[l]))
        y_pre = _ssd_fwd(xo, dtp, yb, dsk)
        y_na = _na(q, k, vt, tab)
        return _out_ffn(x, y_pre, z, y_na, m, row(g_ssd[l]), wo, row(g_ffn[l]), wg, wu, wd)

    return (trunk(x_prompt, mod[:nb_p]), trunk(x_sample, mod[nb_p:nb_p + nb_s]))
```

```python
import functools
import math

import numpy as np
import jax
import jax.numpy as jnp
from jax import lax
from jax.experimental import pallas as pl
from jax.experimental.pallas import tpu as pltpu

D_MODEL = 1024
HEAD_DIM = 64
SSD_HEADS = 8
SSD_INNER = SSD_HEADS * HEAD_DIM
SSD_GROUPS = 2
SSD_STATE = 128
CONV_K = 5
CHUNK = 128
NA_HEADS = 8
NA_INNER = NA_HEADS * HEAD_DIM
NA_ROWS = 8
NA_COLS = 16
GRID_W = 64
CONV_CH = SSD_INNER + 2 * SSD_GROUPS * SSD_STATE
D_FF = 2816
EPS = 1e-6

LANES = 128
BF16_ROWS = 16
CONV_HALO = BF16_ROWS
CONV_ROWS = 128
CONV_COLS = 256
GROUP_W = (SSD_HEADS // SSD_GROUPS) * HEAD_DIM
HEAD_PAIRS = NA_HEADS // 2
LOG2E = math.log2(math.e)
NEG_BIG = -1e30

TOKEN_TILE = 1024
TOKEN_SPLIT = 1
SSD_TILE = 1024
NA_WIN_ROWS = 10
NA_WIN = NA_WIN_ROWS * GRID_W
NA_PAIRS_PER_STEP = 16
NA_QT = NA_PAIRS_PER_STEP * 2 * GRID_W
NA_HALO = 256
FF_BLOCK = 256
VMEM_LIMIT = 56 * 1024 * 1024

f32 = jnp.float32
bf16 = jnp.bfloat16


def _dot(a, b):
    return jnp.dot(a, b, preferred_element_type=f32)


def _dot_nt(a, b):
    return lax.dot_general(a, b, (((1,), (1,)), ((), ())), preferred_element_type=f32)


def _split3(v):
    p1 = v.astype(bf16)
    r1 = v - p1.astype(f32)
    p2 = r1.astype(bf16)
    p3 = (r1 - p2.astype(f32)).astype(bf16)
    return p1, p2, p3


def _even_lane_mask():
    return lax.broadcasted_iota(jnp.int32, (1, LANES), 1) < HEAD_DIM


def _silu(v):
    return v * jax.nn.sigmoid(v)


def _mod_kernel(ct_ref, w_ref, b_ref, o_ref, *, n_rows):
    st = _silu(ct_ref[...])
    w = w_ref[...]
    rows = [jnp.sum(w * st[:, b:b + 1], axis=0, keepdims=True) for b in range(n_rows)]
    rows.append(jnp.zeros((8 - n_rows, w.shape[1]), f32))
    o_ref[...] = jnp.concatenate(rows, axis=0) + b_ref[...]


def _modulation(c_t, w_ada, b_ada, n_rows):
    n = 6 * D_MODEL
    return pl.pallas_call(
        functools.partial(_mod_kernel, n_rows=n_rows),
        grid=(n // D_MODEL,),
        in_specs=[pl.BlockSpec((D_MODEL, 8), lambda j: (0, 0)),
                  pl.BlockSpec((D_MODEL, D_MODEL), lambda j: (0, j)),
                  pl.BlockSpec((1, D_MODEL), lambda j: (0, j))],
        out_specs=pl.BlockSpec((8, D_MODEL), lambda j: (0, j)),
        out_shape=jax.ShapeDtypeStruct((8, n), f32),
        name="mod",
    )(c_t, w_ada, b_ada.reshape(1, n))


def _head_rmsnorm(v, gain):
    me = _even_lane_mask()
    s = v * v
    se = jnp.sum(jnp.where(me, s, 0.0), axis=-1, keepdims=True)
    so = jnp.sum(jnp.where(me, 0.0, s), axis=-1, keepdims=True)
    ms = jnp.where(me, se, so) * (1.0 / HEAD_DIM)
    return v * lax.rsqrt(ms + EPS) * gain


def _in_kernel(x_ref, mod_ref, g_ref, w_ref, wvt_ref, gq_ref, gk_ref, dtb_ref, alog_ref,
               z_ref, xbc_ref, dtp_ref, q_ref, k_ref, vt_ref):
    half = TOKEN_TILE // TOKEN_SPLIT

    def norm_mod(rows):
        x = x_ref[0, rows, :]
        ms = jnp.mean(x * x, axis=-1, keepdims=True)
        y = x * lax.rsqrt(ms + EPS) * g_ref[...]
        return (y * (1.0 + mod_ref[0, 1:2, :]) + mod_ref[0, 0:1, :]).astype(bf16)

    def proj_xbc(rows, h):
        xbc_ref[0, rows, :] = _dot(h, w_ref[:, SSD_INNER:SSD_INNER + CONV_CH]).astype(bf16)

    def proj_z(rows, h):
        z_ref[0, rows, :] = _dot(h, w_ref[:, 0:SSD_INNER]).astype(bf16)

    def proj_head_normed(rows, h, dst_ref, gain_ref, col0):
        full = _dot(h, w_ref[:, col0:col0 + NA_INNER])
        for j in range(HEAD_PAIRS):
            sl = slice(j * LANES, (j + 1) * LANES)
            dst_ref[0, j, rows, :] = _head_rmsnorm(full[:, sl], gain_ref[:, sl]).astype(bf16)

    def proj_v_dt(rows, h):
        full = _dot_nt(wvt_ref[...], h)
        vt = full[0:NA_INNER].astype(bf16)
        tb0 = rows.start // LANES
        for p in range(HEAD_PAIRS):
            for tb in range(half // LANES):
                vt_ref[0, p, tb0 + tb] = vt[p * LANES:(p + 1) * LANES, tb * LANES:(tb + 1) * LANES]
        lane = lax.broadcasted_iota(jnp.int32, (1, LANES), 1)
        a_neg = -jnp.exp(alog_ref[...])
        mul = jnp.where(lane < 2 * SSD_HEADS, a_neg,
                        jnp.where((lane >= 64) & (lane < 64 + 2 * SSD_HEADS), 1.0, 0.0))
        n_dt = 2 * SSD_HEADS
        gap = jnp.zeros((LANES // 2 - n_dt, LANES), f32)
        for tb in range(half // LANES):
            cols = slice(tb * LANES, (tb + 1) * LANES)
            raw_t = jnp.concatenate([full[NA_INNER:NA_INNER + n_dt, cols], gap,
                                     full[NA_INNER + n_dt:NA_INNER + 2 * n_dt, cols], gap], axis=0)
            raw = raw_t.T + dtb_ref[...]
            dt = jnp.maximum(raw, 0.0) + jnp.log1p(jnp.exp(-jnp.abs(raw)))
            dtp_ref[0, rows.start + tb * LANES:rows.start + (tb + 1) * LANES, :] = dt * mul

    o = SSD_INNER + CONV_CH
    for s in range(TOKEN_SPLIT):
        rows = slice(s * half, (s + 1) * half)
        h = norm_mod(rows)
        proj_xbc(rows, h)
        proj_z(rows, h)
        proj_head_normed(rows, h, q_ref, gq_ref, o)
        proj_head_normed(rows, h, k_ref, gk_ref, o + NA_INNER)
        proj_v_dt(rows, h)


def _in_proj(x, mod, g_mix, w_main, w_vt, gq, gk, dt_bias, a_log):
    B, L, _ = x.shape
    T = TOKEN_TILE
    nw = w_main.shape[1]
    const = lambda b, i: (0, 0)
    tok = lambda b, i: (b, i, 0)
    return pl.pallas_call(
        _in_kernel,
        grid=(B, L // T),
        in_specs=[pl.BlockSpec((1, T, D_MODEL), tok),
                  pl.BlockSpec((1, 6, D_MODEL), lambda b, i: (b, 0, 0)),
                  pl.BlockSpec((1, D_MODEL), const),
                  pl.BlockSpec((D_MODEL, nw), const),
                  pl.BlockSpec((NA_INNER + 4 * SSD_HEADS, D_MODEL), const),
                  pl.BlockSpec((1, NA_INNER), const),
                  pl.BlockSpec((1, NA_INNER), const),
                  pl.BlockSpec((1, LANES), const),
                  pl.BlockSpec((1, LANES), const)],
        out_specs=[pl.BlockSpec((1, T, SSD_INNER), tok),
                   pl.BlockSpec((1, T, CONV_CH), tok),
                   pl.BlockSpec((1, T, LANES), tok),
                   pl.BlockSpec((1, HEAD_PAIRS, T, LANES), lambda b, i: (b, 0, i, 0)),
                   pl.BlockSpec((1, HEAD_PAIRS, T, LANES), lambda b, i: (b, 0, i, 0)),
                   pl.BlockSpec((1, HEAD_PAIRS, T // LANES, LANES, LANES), lambda b, i: (b, 0, i, 0, 0))],
        out_shape=[jax.ShapeDtypeStruct((B, L, SSD_INNER), bf16),
                   jax.ShapeDtypeStruct((B, L, CONV_CH), bf16),
                   jax.ShapeDtypeStruct((B, L, LANES), f32),
                   jax.ShapeDtypeStruct((B, HEAD_PAIRS, L, LANES), bf16),
                   jax.ShapeDtypeStruct((B, HEAD_PAIRS, L, LANES), bf16),
                   jax.ShapeDtypeStruct((B, HEAD_PAIRS, L // LANES, LANES, LANES), bf16)],
        compiler_params=pltpu.CompilerParams(dimension_semantics=("arbitrary", "arbitrary"),
                                             vmem_limit_bytes=VMEM_LIMIT),
        name="in_proj",
    )(x, mod, g_mix, w_main, w_vt, gq, gk, dt_bias, a_log)


def _head_selector(base):
    r = lax.broadcasted_iota(jnp.int32, (LANES, SSD_INNER), 0)
    c = lax.broadcasted_iota(jnp.int32, (LANES, SSD_INNER), 1)
    return (lax.shift_right_logical(c, 6) == r - base).astype(bf16)


def _expand_heads(v, sel):
    hi = v.astype(bf16)
    lo = (v - hi.astype(f32)).astype(bf16)
    return _dot(hi, sel) + _dot(lo, sel)


def _cumsum_matrix():
    row = lax.broadcasted_iota(jnp.int32, (CHUNK, CHUNK), 0)
    col = lax.broadcasted_iota(jnp.int32, (CHUNK, CHUNK), 1)
    return (col <= row).astype(bf16)


def _chunk_decays(p, tri):
    p1, p2, p3 = _split3(p)
    incl = _dot(tri, p1) + _dot(tri, p2) + _dot(tri, p3)
    excl = incl - p
    dts = pltpu.roll(p, 64, 1)
    tot = incl[CHUNK - 1:CHUNK, :]
    is_fwd = lax.broadcasted_iota(jnp.int32, (1, LANES), 1) < SSD_HEADS
    w = jnp.exp(jnp.where(is_fwd, tot - incl, excl)) * dts
    sc = jnp.exp(jnp.where(is_fwd, incl, tot - excl))
    return incl, excl, dts, w, sc, jnp.exp(tot)


def _b_transposed(xo_ref, rows, g):
    b_g = xo_ref[0, rows, SSD_INNER + g * SSD_STATE:SSD_INNER + (g + 1) * SSD_STATE].astype(f32)
    return b_g.T.astype(bf16)


def _expand_decays(w, sc, dec, sel):
    return (_expand_heads(w, sel), _expand_heads(sc, sel),
            _expand_heads(jnp.broadcast_to(dec, (8, LANES)), sel)[0:1])


def _state_step(xo_ref, rows, st_ref, expanded, b_t):
    w_x, sc_x, dec_x = expanded
    outs = []
    for g in range(SSD_GROUPS):
        gcols = slice(g * GROUP_W, (g + 1) * GROUP_W)
        xs_g = xo_ref[0, rows, g * GROUP_W:(g + 1) * GROUP_W].astype(f32)
        c_g = xo_ref[0, rows, SSD_INNER + (SSD_GROUPS + g) * SSD_STATE:
                     SSD_INNER + (SSD_GROUPS + g + 1) * SSD_STATE]
        s_g = st_ref[g]
        y_off = _dot(c_g, s_g.astype(bf16)) * sc_x[:, gcols]
        xw = (xs_g * w_x[:, gcols]).astype(bf16)
        st_ref[g] = s_g * dec_x[:, gcols] + _dot(b_t[g], xw)
        outs.append(y_off)
    return jnp.concatenate(outs, axis=1)


def _ssd_bwd_kernel(xc_ref, xp_ref, xn_ref, dtp_ref, cw_ref, cb_ref, xo_ref, yb_ref, win_ref, st_ref, *, n_tiles):
    i = pl.program_id(1)
    t = n_tiles - 1 - i

    @pl.when(i == 0)
    def _():
        st_ref[...] = jnp.zeros_like(st_ref)

    H = CONV_HALO
    win_ref[0:H, :] = jnp.where(t > 0, xp_ref[0], jnp.zeros_like(xp_ref[0]))
    win_ref[H:H + SSD_TILE, :] = xc_ref[0]
    win_ref[H + SSD_TILE:, :] = jnp.where(t < n_tiles - 1, xn_ref[0], jnp.zeros_like(xn_ref[0]))
    r_i = lax.broadcasted_iota(jnp.int32, (CONV_ROWS, CONV_ROWS + 2 * H), 0)
    c_i = lax.broadcasted_iota(jnp.int32, (CONV_ROWS, CONV_ROWS + 2 * H), 1)
    mid = CONV_K // 2
    shift = {k: (c_i == r_i + H + k - mid).astype(bf16) for k in range(CONV_K) if k != mid}
    for c in range(SSD_TILE // CONV_ROWS):
        for cb in range(CONV_CH // CONV_COLS):
            cols = slice(cb * CONV_COLS, (cb + 1) * CONV_COLS)
            xw = win_ref[c * CONV_ROWS:(c + 1) * CONV_ROWS + 2 * H, cols]
            centre = win_ref[H + c * CONV_ROWS:H + (c + 1) * CONV_ROWS, cols].astype(f32)
            acc = cb_ref[:, cols] + centre * cw_ref[mid:mid + 1, cols]
            for k in shift:
                acc = acc + _dot(shift[k], xw) * cw_ref[k:k + 1, cols]
            xo_ref[0, c * CONV_ROWS:(c + 1) * CONV_ROWS, cols] = _silu(acc).astype(bf16)

    sel = _head_selector(SSD_HEADS)
    tri = _cumsum_matrix()

    def body(j, carry):
        c = SSD_TILE // CHUNK - 1 - j
        rows = pl.ds(pl.multiple_of(c * CHUNK, CHUNK), CHUNK)
        _, _, _, w, sc, dec = _chunk_decays(dtp_ref[0, rows, :], tri)
        b_t = [_b_transposed(xo_ref, rows, g) for g in range(SSD_GROUPS)]
        yb_ref[0, rows, :] = _state_step(xo_ref, rows, st_ref, _expand_decays(w, sc, dec, sel), b_t).astype(bf16)
        return carry

    lax.fori_loop(0, SSD_TILE // CHUNK, body, 0, unroll=True)


def _ssd_bwd(xbc, dtp, conv_w, conv_b):
    B, L, _ = xbc.shape
    T = SSD_TILE
    nT = L // T
    hb = T // CONV_HALO
    cur = lambda b, i: (b, nT - 1 - i, 0)
    prev = lambda b, i: (b, jnp.maximum((nT - 1 - i) * hb - 1, 0), 0)
    nxt = lambda b, i: (b, jnp.minimum((nT - i) * hb, L // CONV_HALO - 1), 0)
    const = lambda b, i: (0, 0)
    return pl.pallas_call(
        functools.partial(_ssd_bwd_kernel, n_tiles=nT),
        grid=(B, nT),
        in_specs=[pl.BlockSpec((1, T, CONV_CH), cur),
                  pl.BlockSpec((1, CONV_HALO, CONV_CH), prev),
                  pl.BlockSpec((1, CONV_HALO, CONV_CH), nxt),
                  pl.BlockSpec((1, T, LANES), cur),
                  pl.BlockSpec((CONV_K, CONV_CH), const),
                  pl.BlockSpec((1, CONV_CH), const)],
        out_specs=[pl.BlockSpec((1, T, CONV_CH), cur),
                   pl.BlockSpec((1, T, SSD_INNER), cur)],
        out_shape=[jax.ShapeDtypeStruct((B, L, CONV_CH), bf16),
                   jax.ShapeDtypeStruct((B, L, SSD_INNER), bf16)],
        scratch_shapes=[pltpu.VMEM((T + 2 * CONV_HALO, CONV_CH), bf16),
                        pltpu.VMEM((SSD_GROUPS, SSD_STATE, GROUP_W), f32)],
        compiler_params=pltpu.CompilerParams(dimension_semantics=("arbitrary", "arbitrary"),
                                             vmem_limit_bytes=VMEM_LIMIT),
        name="ssd_bwd",
    )(xbc, xbc, xbc, dtp, conv_w, conv_b)


def _ssd_fwd_kernel(xo_ref, dtp_ref, yb_ref, dsk_ref, o_ref, st_ref):
    @pl.when(pl.program_id(1) == 0)
    def _():
        st_ref[...] = jnp.zeros_like(st_ref)

    row = lax.broadcasted_iota(jnp.int32, (CHUNK, CHUNK), 0)
    col = lax.broadcasted_iota(jnp.int32, (CHUNK, CHUNK), 1)
    low = col < row
    diag = col == row
    me = _even_lane_mask()
    sel = _head_selector(0)
    tri = _cumsum_matrix()

    def body(c, carry):
        rows = pl.ds(pl.multiple_of(c * CHUNK, CHUNK), CHUNK)
        incl, excl, dts, w, sc, dec = _chunk_decays(dtp_ref[0, rows, :], tri)
        b_t = [_b_transposed(xo_ref, rows, g) for g in range(SSD_GROUPS)]
        y = _state_step(xo_ref, rows, st_ref, _expand_decays(w, sc, dec, sel), b_t)

        incl_t = incl.T
        excl_t = excl.T
        dts_t = dts.T
        y_diag = []
        for g in range(SSD_GROUPS):
            c_g = xo_ref[0, rows, SSD_INNER + (SSD_GROUPS + g) * SSD_STATE:
                         SSD_INNER + (SSD_GROUPS + g + 1) * SSD_STATE]
            cb = _dot(c_g, b_t[g])
            for pr in range(2):
                xs_pair = xo_ref[0, rows, g * GROUP_W + pr * LANES:g * GROUP_W + (pr + 1) * LANES]
                acc = None
                for e in range(2):
                    hd = 4 * g + 2 * pr + e
                    af_c = jnp.broadcast_to(incl[:, hd:hd + 1], (CHUNK, CHUNK))
                    eb_c = jnp.broadcast_to(excl[:, SSD_HEADS + hd:SSD_HEADS + hd + 1], (CHUNK, CHUNK))
                    af_r = incl_t[hd:hd + 1, :]
                    eb_r = excl_t[SSD_HEADS + hd:SSD_HEADS + hd + 1, :]
                    df_r = dts_t[hd:hd + 1, :]
                    db_r = dts_t[SSD_HEADS + hd:SSD_HEADS + hd + 1, :]
                    arg = jnp.where(low, af_c - af_r, eb_r - eb_c)
                    coef = jnp.where(low, df_r, jnp.where(diag, df_r + db_r, db_r))
                    wm = (cb * jnp.exp(arg) * coef).astype(bf16)
                    xm = jnp.where(me, xs_pair, jnp.zeros_like(xs_pair)) if e == 0 else \
                        jnp.where(me, jnp.zeros_like(xs_pair), xs_pair)
                    d = _dot(wm, xm)
                    acc = d if acc is None else acc + d
                y_diag.append(acc)
        xs = xo_ref[0, rows, 0:SSD_INNER].astype(f32)
        y = y + jnp.concatenate(y_diag, axis=1) + yb_ref[0, rows, :].astype(f32) + xs * dsk_ref[...]
        o_ref[0, rows, :] = y.astype(bf16)
        return carry

    lax.fori_loop(0, SSD_TILE // CHUNK, body, 0, unroll=True)


def _ssd_fwd(xo, dtp, yb, d_skip_x):
    B, L, _ = xo.shape
    T = SSD_TILE
    tok = lambda b, i: (b, i, 0)
    const = lambda b, i: (0, 0)
    return pl.pallas_call(
        _ssd_fwd_kernel,
        grid=(B, L // T),
        in_specs=[pl.BlockSpec((1, T, CONV_CH), tok),
                  pl.BlockSpec((1, T, LANES), tok),
                  pl.BlockSpec((1, T, SSD_INNER), tok),
                  pl.BlockSpec((1, SSD_INNER), const)],
        out_specs=pl.BlockSpec((1, T, SSD_INNER), tok),
        out_shape=jax.ShapeDtypeStruct((B, L, SSD_INNER), bf16),
        scratch_shapes=[pltpu.VMEM((SSD_GROUPS, SSD_STATE, GROUP_W), f32)],
        compiler_params=pltpu.CompilerParams(dimension_semantics=("arbitrary", "arbitrary"),
                                             vmem_limit_bytes=VMEM_LIMIT),
        name="ssd_fwd",
    )(xo, dtp, yb, d_skip_x)


def _na_bias_tables(rpb):
    variants = [(0, 0, 0), (2, 0, 0), (4, 0, 1), (6, 2, 2), (8, 2, 2)]
    kc = np.arange(GRID_W)[:, None]
    w = np.arange(GRID_W)[None, :]
    cs = np.clip(w - NA_COLS // 2, 0, GRID_W - NA_COLS)
    col_ok = (kc >= cs) & (kc < cs + NA_COLS)
    taps = np.arange(2 * NA_COLS - 1)[:, None, None]
    onehot = (((kc - w + NA_COLS - 1)[None] == taps) & col_ok[None]).astype(np.float32)
    base = jnp.einsum('hrk,kcw->hrcw', rpb.astype(f32) * LOG2E, onehot, precision=lax.Precision.HIGHEST)
    base = jnp.where(col_ok, base, NEG_BIG).reshape(HEAD_PAIRS, 2, 2 * NA_ROWS - 1, GRID_W, GRID_W)
    basep = jnp.pad(base, ((0, 0), (0, 0), (2, 2), (0, 0), (0, 0)), constant_values=NEG_BIG)
    n_blk = 2 * NA_ROWS + 2
    blk = jnp.concatenate([basep[:, 0, 1:1 + n_blk], basep[:, 0, 0:n_blk],
                           basep[:, 1, 1:1 + n_blk], basep[:, 1, 0:n_blk]], axis=-1)
    kr = np.arange(NA_WIN)[:, None] // GRID_W
    par = (np.arange(2 * LANES)[None, :] // GRID_W) % 2
    tabs = []
    for roff, st0, st1 in variants:
        st = np.where(par == 0, st0, st1)
        valid = (kr >= st) & (kr < st + NA_ROWS)
        j0 = NA_ROWS - roff
        rows = blk[:, j0:j0 + NA_WIN_ROWS].reshape(HEAD_PAIRS, NA_WIN, 2 * LANES)
        tabs.append(jnp.where(valid, rows, NEG_BIG))
    return jnp.stack(tabs)


def _na_kernel(q_ref, kp_ref, kc_ref, kn_ref, vp_ref, vc_ref, vn_ref, tab_ref, o_ref, kwin_ref, vwin_ref,
               *, grid_rows):
    step = pl.program_id(2)
    me = _even_lane_mask()

    kwin_ref[0:NA_HALO, :] = kp_ref[0, 0]
    kwin_ref[NA_HALO:NA_HALO + NA_QT, :] = kc_ref[0, 0]
    kwin_ref[NA_HALO + NA_QT:, :] = kn_ref[0, 0]
    ht = NA_HALO // LANES
    vwin_ref[0:ht] = vp_ref[0, 0]
    vwin_ref[ht:ht + NA_QT // LANES] = vc_ref[0, 0]
    vwin_ref[ht + NA_QT // LANES:] = vn_ref[0, 0]
    tok0 = step * NA_QT - NA_HALO

    def window(i):
        r = 2 * (step * NA_PAIRS_PER_STEP + i)
        u0 = jnp.clip(r - NA_ROWS // 2, 0, grid_rows - NA_WIN_ROWS)
        var = jnp.where(r < 4, r // 2, jnp.where(r >= grid_rows - 4, 3 + (r - (grid_rows - 4)) // 2, 2))
        return u0, var

    def run(win_rows):
        n_keys = win_rows * GRID_W
        half = n_keys // 2

        def scores(i):
            u0, var = window(i)
            q2 = q_ref[0, 0, i * LANES:(i + 1) * LANES, :]
            zero = jnp.zeros_like(q2)
            qm = jnp.concatenate([jnp.where(me, q2, zero), jnp.where(me, zero, q2)], axis=0)
            parts = []
            for hk in range(2):
                start = pl.multiple_of(u0 * GRID_W - tok0 + hk * half, BF16_ROWS)
                kk = kwin_ref[pl.ds(start, half), :]
                parts.append(_dot_nt(kk, qm) + tab_ref[var, 0, hk * half:(hk + 1) * half, :])
            return jnp.concatenate(parts, axis=0)

        def finish(i, parts, rl):
            oc = jnp.concatenate([parts[e] * rl[:, e * LANES:(e + 1) * LANES] for e in range(2)], axis=0)
            o_ref[0, 0, i * LANES:(i + 1) * LANES, :] = oc.T.astype(bf16)

        s = scores(0)
        pending = None
        for i in range(NA_PAIRS_PER_STEP):
            s_next = scores(i + 1) if i + 1 < NA_PAIRS_PER_STEP else None
            u0, _ = window(i)
            m = jnp.max(s, axis=0, keepdims=True)
            p = jnp.exp2(s - m)
            l = jnp.sum(p, axis=0, keepdims=True)
            j0 = lax.shift_right_logical(u0 * GRID_W - tok0, 7)
            vtw = jnp.concatenate([vwin_ref[j0 + j] for j in range(NA_WIN // LANES)], axis=1)
            pb = p.astype(bf16)
            if n_keys < NA_WIN:
                pb = jnp.concatenate([pb, jnp.zeros((NA_WIN - n_keys, 2 * LANES), bf16)], axis=0)
            parts = [_dot(vtw[e * HEAD_DIM:(e + 1) * HEAD_DIM, :], pb[:, e * LANES:(e + 1) * LANES])
                     for e in range(2)]
            if pending is not None:
                finish(*pending)
            pending = (i, parts, 1.0 / l)
            s = s_next
        finish(*pending)

    last = pl.num_programs(2) - 1
    pl.when(step != last)(functools.partial(run, NA_WIN_ROWS - 1))
    pl.when(step == last)(functools.partial(run, NA_WIN_ROWS))


def _na(q, k, vt, tab):
    B, _, L, _ = q.shape
    grid_rows = L // GRID_W
    QT, H = NA_QT, NA_HALO
    r = QT // H
    nh = L // H
    nv = tab.shape[0]
    cur = lambda p, b, s: (b, p, s, 0)
    prev = lambda p, b, s: (b, p, jnp.maximum(s * r - 1, 0), 0)
    nxt = lambda p, b, s: (b, p, jnp.minimum((s + 1) * r, nh - 1), 0)
    tile = lambda f: (lambda p, b, s: f(p, b, s) + (0,))
    ht = H // LANES
    return pl.pallas_call(
        functools.partial(_na_kernel, grid_rows=grid_rows),
        grid=(HEAD_PAIRS, B, L // QT),
        in_specs=[pl.BlockSpec((1, 1, QT, LANES), cur),
                  pl.BlockSpec((1, 1, H, LANES), prev),
                  pl.BlockSpec((1, 1, QT, LANES), cur),
                  pl.BlockSpec((1, 1, H, LANES), nxt),
                  pl.BlockSpec((1, 1, ht, LANES, LANES), tile(prev)),
                  pl.BlockSpec((1, 1, QT // LANES, LANES, LANES), tile(cur)),
                  pl.BlockSpec((1, 1, ht, LANES, LANES), tile(nxt)),
                  pl.BlockSpec((nv, 1, NA_WIN, 2 * LANES), lambda p, b, s: (0, p, 0, 0))],
        out_specs=pl.BlockSpec((1, 1, QT, LANES), cur),
        out_shape=jax.ShapeDtypeStruct((B, HEAD_PAIRS, L, LANES), bf16),
        scratch_shapes=[pltpu.VMEM((QT + 2 * H, LANES), bf16),
                        pltpu.VMEM((QT // LANES + 2 * ht, LANES, LANES), bf16)],
        compiler_params=pltpu.CompilerParams(dimension_semantics=("arbitrary", "arbitrary", "arbitrary"),
                                             vmem_limit_bytes=VMEM_LIMIT),
        name="na",
    )(q, k, k, k, vt, vt, vt, tab)


def _out_kernel(x_ref, ys_ref, z_ref, yn_ref, mod_ref, gs_ref, wo_ref, gf_ref, wg_ref, wu_ref, wd_ref, o_ref, a_ref):
    half = TOKEN_TILE // TOKEN_SPLIT

    def gate_norm(rows):
        gated = ys_ref[0, rows, :].astype(f32) * _silu(z_ref[0, rows, :].astype(f32))
        gms = jnp.mean(gated * gated, axis=-1, keepdims=True)
        return (gated * lax.rsqrt(gms + EPS) * gs_ref[...]).astype(bf16)

    def out_proj(rows, y_ssd):
        y_na = jnp.concatenate([yn_ref[0, p, rows, :] for p in range(HEAD_PAIRS)], axis=1)
        mix = _dot(y_ssd, wo_ref[0:SSD_INNER, :]) + _dot(y_na, wo_ref[SSD_INNER:, :])
        return x_ref[0, rows, :] + mod_ref[0, 2:3, :] * mix

    def ffn_norm(x1):
        ms = jnp.mean(x1 * x1, axis=-1, keepdims=True)
        h2 = x1 * lax.rsqrt(ms + EPS) * gf_ref[...]
        return (h2 * (1.0 + mod_ref[0, 4:5, :]) + mod_ref[0, 3:4, :]).astype(bf16)

    def ffn_block(rows, h2, c):
        cols = slice(c * FF_BLOCK, (c + 1) * FF_BLOCK)
        gate = _dot(h2, wg_ref[:, cols])
        up = _dot(h2, wu_ref[:, cols])
        a_ref[rows, cols] = (_silu(gate) * up).astype(bf16)

    def ffn_down(rows, x1):
        ff = _dot(a_ref[rows, :], wd_ref[...])
        o_ref[0, rows, :] = x1 + mod_ref[0, 5:6, :] * ff

    for s in range(TOKEN_SPLIT):
        rows = slice(s * half, (s + 1) * half)
        x1 = out_proj(rows, gate_norm(rows))
        h2 = ffn_norm(x1)
        for c in range(D_FF // FF_BLOCK):
            ffn_block(rows, h2, c)
        ffn_down(rows, x1)


def _out_ffn(x, y_pre, z, y_na, mod, g_ssd, w_out, g_ffn, w_gate, w_up, w_down):
    B, L, _ = x.shape
    T = TOKEN_TILE
    tok = lambda b, i: (b, i, 0)
    const = lambda b, i: (0, 0)
    single = pl.Buffered(1)
    return pl.pallas_call(
        _out_kernel,
        grid=(B, L // T),
        in_specs=[pl.BlockSpec((1, T, D_MODEL), tok),
                  pl.BlockSpec((1, T, SSD_INNER), tok),
                  pl.BlockSpec((1, T, SSD_INNER), tok),
                  pl.BlockSpec((1, HEAD_PAIRS, T, LANES), lambda b, i: (b, 0, i, 0)),
                  pl.BlockSpec((1, 6, D_MODEL), lambda b, i: (b, 0, 0)),
                  pl.BlockSpec((1, SSD_INNER), const),
                  pl.BlockSpec((D_MODEL, D_MODEL), const, pipeline_mode=single),
                  pl.BlockSpec((1, D_MODEL), const),
                  pl.BlockSpec((D_MODEL, D_FF), const, pipeline_mode=single),
                  pl.BlockSpec((D_MODEL, D_FF), const, pipeline_mode=single),
                  pl.BlockSpec((D_FF, D_MODEL), const, pipeline_mode=single)],
        out_specs=pl.BlockSpec((1, T, D_MODEL), tok),
        out_shape=jax.ShapeDtypeStruct((B, L, D_MODEL), f32),
        scratch_shapes=[pltpu.VMEM((T, D_FF), bf16)],
        compiler_params=pltpu.CompilerParams(dimension_semantics=("arbitrary", "arbitrary"),
                                             vmem_limit_bytes=VMEM_LIMIT),
        name="out_ffn",
    )(x, y_pre, z, y_na, mod, g_ssd, w_out, g_ffn, w_gate, w_up, w_down)


def _pad_lanes(parts):
    pieces, pos = [], 0
    for off, v in parts:
        pieces += [jnp.zeros((off - pos,), f32), v.astype(f32)]
        pos = off + v.shape[0]
    pieces.append(jnp.zeros((LANES - pos,), f32))
    return jnp.concatenate(pieces).reshape(1, LANES)


def kernel(x_prompt, x_sample, c_prompt, c_sample, w_ada, b_ada, g_mix, w_in, conv_w, conv_b, dt_bias_fwd, dt_bias_bwd, a_log_fwd, a_log_bwd, d_skip, g_ssd, g_q, g_k, rpb, w_out, g_ffn, w_gate, w_up, w_down):
    l = 0
    nb_p = x_prompt.shape[0]
    nb_s = x_sample.shape[0]

    c_all = jnp.concatenate([c_prompt, c_sample, jnp.zeros((8 - nb_p - nb_s, D_MODEL), f32)], axis=0)
    mod = _modulation(c_all.T, w_ada[l], b_ada[l], nb_p + nb_s).reshape(8, 6, D_MODEL)

    o1 = SSD_INNER
    o2 = o1 + CONV_CH
    o3 = o2 + 2 * SSD_HEADS
    o4 = o3 + NA_INNER
    o5 = o4 + NA_INNER
    wi = w_in[l]
    w_main = jnp.concatenate([wi[:, :o2], wi[:, o3:o5]], axis=1).astype(bf16)
    w_vt = jnp.concatenate([wi[:, o5:], wi[:, o2:o3], wi[:, o2:o3]], axis=1).T.astype(bf16)
    dtb = _pad_lanes([(0, dt_bias_fwd[l]), (8, dt_bias_bwd[l]), (64, dt_bias_fwd[l]), (72, dt_bias_bwd[l])])
    alog = _pad_lanes([(0, a_log_fwd[l]), (8, a_log_bwd[l])])
    gq = jnp.tile(g_q[l].astype(f32) * (HEAD_DIM ** -0.5 * LOG2E), NA_HEADS).reshape(1, NA_INNER)
    gk = jnp.tile(g_k[l].astype(f32), NA_HEADS).reshape(1, NA_INNER)
    dsk = jnp.repeat(d_skip[l].astype(f32), HEAD_DIM).reshape(1, SSD_INNER)
    tab = _na_bias_tables(rpb[l])
    wo = w_out[l].astype(bf16)
    wg = w_gate[l].astype(bf16)
    wu = w_up[l].astype(bf16)
    wd = w_down[l].astype(bf16)
    row = lambda v: v.astype(f32).reshape(1, -1)

    def trunk(x, m):
        z, xbc, dtp, q, k, vt = _in_proj(x, m, row(g_mix[l]), w_main, w_vt, gq, gk, dtb, alog)
        xo, yb = _ssd_bwd(xbc, dtp, conv_w[l].astype(f32), row(conv_b[l]))
        y_pre = _ssd_fwd(xo, dtp, yb, dsk)
        y_na = _na(q, k, vt, tab)
        return _out_ffn(x, y_pre, z, y_na, m, row(g_ssd[l]), wo, row(g_ffn[l]), wg, wu, wd)

    return (trunk(x_prompt, mod[:nb_p]), trunk(x_sample, mod[nb_p:nb_p + nb_s]))
```

```python
import functools
import math

import numpy as np
import jax
import jax.numpy as jnp
from jax import lax
from jax.experimental import pallas as pl
from jax.experimental.pallas import tpu as pltpu

D_MODEL = 1024
HEAD_DIM = 64
SSD_HEADS = 8
SSD_INNER = SSD_HEADS * HEAD_DIM
SSD_GROUPS = 2
SSD_STATE = 128
CONV_K = 5
CHUNK = 128
NA_HEADS = 8
NA_INNER = NA_HEADS * HEAD_DIM
NA_ROWS = 8
NA_COLS = 16
GRID_W = 64
CONV_CH = SSD_INNER + 2 * SSD_GROUPS * SSD_STATE
D_FF = 2816
EPS = 1e-6

LANES = 128
BF16_ROWS = 16
CONV_HALO = BF16_ROWS
CONV_ROWS = 128
CONV_COLS = 256
GROUP_W = (SSD_HEADS // SSD_GROUPS) * HEAD_DIM
HEAD_PAIRS = NA_HEADS // 2
LOG2E = math.log2(math.e)
NEG_BIG = -1e30

TOKEN_TILE = 1024
TOKEN_SPLIT = 1
SSD_TILE = 1024
NA_WIN_ROWS = 10
NA_WIN = NA_WIN_ROWS * GRID_W
NA_PAIRS_PER_STEP = 16
NA_QT = NA_PAIRS_PER_STEP * 2 * GRID_W
NA_HALO = 256
FF_BLOCK = 256
VMEM_LIMIT = 56 * 1024 * 1024

f32 = jnp.float32
bf16 = jnp.bfloat16


def _dot(a, b):
    return jnp.dot(a, b, preferred_element_type=f32)


def _dot_nt(a, b):
    return lax.dot_general(a, b, (((1,), (1,)), ((), ())), preferred_element_type=f32)


def _split3(v):
    p1 = v.astype(bf16)
    r1 = v - p1.astype(f32)
    p2 = r1.astype(bf16)
    p3 = (r1 - p2.astype(f32)).astype(bf16)
    return p1, p2, p3


def _even_lane_mask():
    return lax.broadcasted_iota(jnp.int32, (1, LANES), 1) < HEAD_DIM


def _silu(v):
    return v * jax.nn.sigmoid(v)


def _mod_kernel(ct_ref, w_ref, b_ref, o_ref, *, n_rows):
    st = _silu(ct_ref[...])
    w = w_ref[...]
    rows = [jnp.sum(w * st[:, b:b + 1], axis=0, keepdims=True) for b in range(n_rows)]
    rows.append(jnp.zeros((8 - n_rows, w.shape[1]), f32))
    o_ref[...] = jnp.concatenate(rows, axis=0) + b_ref[...]


def _modulation(c_t, w_ada, b_ada, n_rows):
    n = 6 * D_MODEL
    return pl.pallas_call(
        functools.partial(_mod_kernel, n_rows=n_rows),
        grid=(n // D_MODEL,),
        in_specs=[pl.BlockSpec((D_MODEL, 8), lambda j: (0, 0)),
                  pl.BlockSpec((D_MODEL, D_MODEL), lambda j: (0, j)),
                  pl.BlockSpec((1, D_MODEL), lambda j: (0, j))],
        out_specs=pl.BlockSpec((8, D_MODEL), lambda j: (0, j)),
        out_shape=jax.ShapeDtypeStruct((8, n), f32),
        name="mod",
    )(c_t, w_ada, b_ada.reshape(1, n))


def _head_rmsnorm(v, gain):
    me = _even_lane_mask()
    s = v * v
    se = jnp.sum(jnp.where(me, s, 0.0), axis=-1, keepdims=True)
    so = jnp.sum(jnp.where(me, 0.0, s), axis=-1, keepdims=True)
    ms = jnp.where(me, se, so) * (1.0 / HEAD_DIM)
    return v * lax.rsqrt(ms + EPS) * gain


def _in_kernel(x_ref, mod_ref, g_ref, w_ref, wqk_ref, wvt_ref, gq_ref, gk_ref, dtb_ref, alog_ref,
               z_ref, xbc_ref, dtp_ref, q_ref, k_ref, vt_ref):
    half = TOKEN_TILE // TOKEN_SPLIT

    def norm_mod(rows):
        x = x_ref[0, rows, :]
        ms = jnp.mean(x * x, axis=-1, keepdims=True)
        y = x * lax.rsqrt(ms + EPS) * g_ref[...]
        return (y * (1.0 + mod_ref[0, 1:2, :]) + mod_ref[0, 0:1, :]).astype(bf16)

    def proj_xbc(rows, h):
        xbc_ref[0, rows, :] = _dot(h, w_ref[:, SSD_INNER:SSD_INNER + CONV_CH]).astype(bf16)

    def proj_z(rows, h):
        z_ref[0, rows, :] = _dot(h, w_ref[:, 0:SSD_INNER]).astype(bf16)

    def proj_head_normed(rows, h, dst_ref, gain_ref, col0):
        full = _dot(h, wqk_ref[:, col0:col0 + NA_INNER])
        for j in range(HEAD_PAIRS):
            sl = slice(j * LANES, (j + 1) * LANES)
            dst_ref[0, j, rows, :] = _head_rmsnorm(full[:, sl], gain_ref[:, sl]).astype(bf16)

    def proj_v_dt(rows, h):
        full = _dot_nt(wvt_ref[...], h)
        vt = full[0:NA_INNER].astype(bf16)
        tb0 = rows.start // LANES
        for p in range(HEAD_PAIRS):
            for tb in range(half // LANES):
                vt_ref[0, p, tb0 + tb] = vt[p * LANES:(p + 1) * LANES, tb * LANES:(tb + 1) * LANES]
        lane = lax.broadcasted_iota(jnp.int32, (1, LANES), 1)
        a_neg = -jnp.exp(alog_ref[...])
        mul = jnp.where(lane < 2 * SSD_HEADS, a_neg,
                        jnp.where((lane >= 64) & (lane < 64 + 2 * SSD_HEADS), 1.0, 0.0))
        n_dt = 2 * SSD_HEADS
        gap = jnp.zeros((LANES // 2 - n_dt, LANES), f32)
        for tb in range(half // LANES):
            cols = slice(tb * LANES, (tb + 1) * LANES)
            raw_t = jnp.concatenate([full[NA_INNER:NA_INNER + n_dt, cols], gap,
                                     full[NA_INNER + n_dt:NA_INNER + 2 * n_dt, cols], gap], axis=0)
            raw = raw_t.T + dtb_ref[...]
            dt = jnp.maximum(raw, 0.0) + jnp.log1p(jnp.exp(-jnp.abs(raw)))
            dtp_ref[0, rows.start + tb * LANES:rows.start + (tb + 1) * LANES, :] = dt * mul

    for s in range(TOKEN_SPLIT):
        rows = slice(s * half, (s + 1) * half)
        h = norm_mod(rows)
        proj_xbc(rows, h)
        proj_z(rows, h)
        proj_head_normed(rows, h, q_ref, gq_ref, 0)
        proj_head_normed(rows, h, k_ref, gk_ref, NA_INNER)
        proj_v_dt(rows, h)


def _in_proj(x, mod, g_mix, w_zx, w_qk, w_vt, gq, gk, dt_bias, a_log):
    B, L, _ = x.shape
    T = TOKEN_TILE
    const = lambda b, i: (0, 0)
    tok = lambda b, i: (b, i, 0)
    return pl.pallas_call(
        _in_kernel,
        grid=(B, L // T),
        in_specs=[pl.BlockSpec((1, T, D_MODEL), tok),
                  pl.BlockSpec((1, 6, D_MODEL), lambda b, i: (b, 0, 0)),
                  pl.BlockSpec((1, D_MODEL), const),
                  pl.BlockSpec((D_MODEL, SSD_INNER + CONV_CH), const),
                  pl.BlockSpec((D_MODEL, 2 * NA_INNER), const),
                  pl.BlockSpec((NA_INNER + 4 * SSD_HEADS, D_MODEL), const),
                  pl.BlockSpec((1, NA_INNER), const),
                  pl.BlockSpec((1, NA_INNER), const),
                  pl.BlockSpec((1, LANES), const),
                  pl.BlockSpec((1, LANES), const)],
        out_specs=[pl.BlockSpec((1, T, SSD_INNER), tok),
                   pl.BlockSpec((1, T, CONV_CH), tok),
                   pl.BlockSpec((1, T, LANES), tok),
                   pl.BlockSpec((1, HEAD_PAIRS, T, LANES), lambda b, i: (b, 0, i, 0)),
                   pl.BlockSpec((1, HEAD_PAIRS, T, LANES), lambda b, i: (b, 0, i, 0)),
                   pl.BlockSpec((1, HEAD_PAIRS, T // LANES, LANES, LANES), lambda b, i: (b, 0, i, 0, 0))],
        out_shape=[jax.ShapeDtypeStruct((B, L, SSD_INNER), bf16),
                   jax.ShapeDtypeStruct((B, L, CONV_CH), bf16),
                   jax.ShapeDtypeStruct((B, L, LANES), f32),
                   jax.ShapeDtypeStruct((B, HEAD_PAIRS, L, LANES), bf16),
                   jax.ShapeDtypeStruct((B, HEAD_PAIRS, L, LANES), bf16),
                   jax.ShapeDtypeStruct((B, HEAD_PAIRS, L // LANES, LANES, LANES), bf16)],
        compiler_params=pltpu.CompilerParams(dimension_semantics=("arbitrary", "arbitrary"),
                                             vmem_limit_bytes=VMEM_LIMIT),
        name="in_proj",
    )(x, mod, g_mix, w_zx, w_qk, w_vt, gq, gk, dt_bias, a_log)


def _head_selector(base):
    r = lax.broadcasted_iota(jnp.int32, (LANES, SSD_INNER), 0)
    c = lax.broadcasted_iota(jnp.int32, (LANES, SSD_INNER), 1)
    return (lax.shift_right_logical(c, 6) == r - base).astype(bf16)


def _cumsum_matrix():
    row = lax.broadcasted_iota(jnp.int32, (CHUNK, CHUNK), 0)
    col = lax.broadcasted_iota(jnp.int32, (CHUNK, CHUNK), 1)
    return (col <= row).astype(bf16)


def _chunk_decays(p, tri):
    p1, p2, p3 = _split3(p)
    parts = _dot(tri, jnp.concatenate([p1, p2, p3], axis=1))
    incl = parts[:, 0:LANES] + parts[:, LANES:2 * LANES] + parts[:, 2 * LANES:3 * LANES]
    excl = incl - p
    dts = pltpu.roll(p, 64, 1)
    tot = incl[CHUNK - 1:CHUNK, :]
    is_fwd = lax.broadcasted_iota(jnp.int32, (1, LANES), 1) < SSD_HEADS
    w = jnp.exp(jnp.where(is_fwd, tot - incl, excl)) * dts
    sc = jnp.exp(jnp.where(is_fwd, incl, tot - excl))
    return incl, excl, dts, w, sc, jnp.exp(tot)


def _b_transposed(xo_ref, rows, g):
    b_g = xo_ref[0, rows, SSD_INNER + g * SSD_STATE:SSD_INNER + (g + 1) * SSD_STATE].astype(f32)
    return b_g.T.astype(bf16)


def _expand_decays(w, sc, dec, sel):
    v = jnp.concatenate([w, sc, jnp.broadcast_to(dec, (BF16_ROWS, LANES))], axis=0)
    n = v.shape[0]
    hi = v.astype(bf16)
    lo = (v - hi.astype(f32)).astype(bf16)
    both = _dot(jnp.concatenate([hi, lo], axis=0), sel)
    x = both[0:n] + both[n:2 * n]
    return x[0:CHUNK], x[CHUNK:2 * CHUNK], x[2 * CHUNK:2 * CHUNK + 1]


def _state_step(xo_ref, rows, st_ref, expanded, b_t):
    w_x, sc_x, dec_x = expanded
    outs = []
    for g in range(SSD_GROUPS):
        gcols = slice(g * GROUP_W, (g + 1) * GROUP_W)
        xs_g = xo_ref[0, rows, g * GROUP_W:(g + 1) * GROUP_W].astype(f32)
        c_g = xo_ref[0, rows, SSD_INNER + (SSD_GROUPS + g) * SSD_STATE:
                     SSD_INNER + (SSD_GROUPS + g + 1) * SSD_STATE]
        s_g = st_ref[g]
        y_off = _dot(c_g, s_g.astype(bf16)) * sc_x[:, gcols]
        xw = (xs_g * w_x[:, gcols]).astype(bf16)
        st_ref[g] = s_g * dec_x[:, gcols] + _dot(b_t[g], xw)
        outs.append(y_off)
    return jnp.concatenate(outs, axis=1)


def _ssd_bwd_kernel(xc_ref, xp_ref, xn_ref, dtp_ref, cw_ref, cb_ref, xo_ref, yb_ref, win_ref, st_ref, *, n_tiles):
    i = pl.program_id(1)
    t = n_tiles - 1 - i

    @pl.when(i == 0)
    def _():
        st_ref[...] = jnp.zeros_like(st_ref)

    H = CONV_HALO
    win_ref[0:H, :] = jnp.where(t > 0, xp_ref[0], jnp.zeros_like(xp_ref[0]))
    win_ref[H:H + SSD_TILE, :] = xc_ref[0]
    win_ref[H + SSD_TILE:, :] = jnp.where(t < n_tiles - 1, xn_ref[0], jnp.zeros_like(xn_ref[0]))
    r_i = lax.broadcasted_iota(jnp.int32, (CONV_ROWS, CONV_ROWS + 2 * H), 0)
    c_i = lax.broadcasted_iota(jnp.int32, (CONV_ROWS, CONV_ROWS + 2 * H), 1)
    mid = CONV_K // 2
    taps = [k for k in range(CONV_K) if k != mid]
    shifts = jnp.concatenate([(c_i == r_i + H + k - mid).astype(bf16) for k in taps], axis=0)
    for c in range(SSD_TILE // CONV_ROWS):
        for cb in range(CONV_CH // CONV_COLS):
            cols = slice(cb * CONV_COLS, (cb + 1) * CONV_COLS)
            xw = win_ref[c * CONV_ROWS:(c + 1) * CONV_ROWS + 2 * H, cols]
            centre = win_ref[H + c * CONV_ROWS:H + (c + 1) * CONV_ROWS, cols].astype(f32)
            acc = cb_ref[:, cols] + centre * cw_ref[mid:mid + 1, cols]
            shifted = _dot(shifts, xw)
            for j, k in enumerate(taps):
                acc = acc + shifted[j * CONV_ROWS:(j + 1) * CONV_ROWS] * cw_ref[k:k + 1, cols]
            xo_ref[0, c * CONV_ROWS:(c + 1) * CONV_ROWS, cols] = _silu(acc).astype(bf16)

    sel = _head_selector(SSD_HEADS)
    tri = _cumsum_matrix()

    def body(j, carry):
        c = SSD_TILE // CHUNK - 1 - j
        rows = pl.ds(pl.multiple_of(c * CHUNK, CHUNK), CHUNK)
        _, _, _, w, sc, dec = _chunk_decays(dtp_ref[0, rows, :], tri)
        b_t = [_b_transposed(xo_ref, rows, g) for g in range(SSD_GROUPS)]
        yb_ref[0, rows, :] = _state_step(xo_ref, rows, st_ref, _expand_decays(w, sc, dec, sel), b_t).astype(bf16)
        return carry

    lax.fori_loop(0, SSD_TILE // CHUNK, body, 0, unroll=True)


def _ssd_bwd(xbc, dtp, conv_w, conv_b):
    B, L, _ = xbc.shape
    T = SSD_TILE
    nT = L // T
    hb = T // CONV_HALO
    cur = lambda b, i: (b, nT - 1 - i, 0)
    prev = lambda b, i: (b, jnp.maximum((nT - 1 - i) * hb - 1, 0), 0)
    nxt = lambda b, i: (b, jnp.minimum((nT - i) * hb, L // CONV_HALO - 1), 0)
    const = lambda b, i: (0, 0)
    return pl.pallas_call(
        functools.partial(_ssd_bwd_kernel, n_tiles=nT),
        grid=(B, nT),
        in_specs=[pl.BlockSpec((1, T, CONV_CH), cur),
                  pl.BlockSpec((1, CONV_HALO, CONV_CH), prev),
                  pl.BlockSpec((1, CONV_HALO, CONV_CH), nxt),
                  pl.BlockSpec((1, T, LANES), cur),
                  pl.BlockSpec((CONV_K, CONV_CH), const),
                  pl.BlockSpec((1, CONV_CH), const)],
        out_specs=[pl.BlockSpec((1, T, CONV_CH), cur),
                   pl.BlockSpec((1, T, SSD_INNER), cur)],
        out_shape=[jax.ShapeDtypeStruct((B, L, CONV_CH), bf16),
                   jax.ShapeDtypeStruct((B, L, SSD_INNER), bf16)],
        scratch_shapes=[pltpu.VMEM((T + 2 * CONV_HALO, CONV_CH), bf16),
                        pltpu.VMEM((SSD_GROUPS, SSD_STATE, GROUP_W), f32)],
        compiler_params=pltpu.CompilerParams(dimension_semantics=("arbitrary", "arbitrary"),
                                             vmem_limit_bytes=VMEM_LIMIT),
        name="ssd_bwd",
    )(xbc, xbc, xbc, dtp, conv_w, conv_b)


def _ssd_fwd_kernel(xo_ref, dtp_ref, yb_ref, dsk_ref, o_ref, st_ref):
    @pl.when(pl.program_id(1) == 0)
    def _():
        st_ref[...] = jnp.zeros_like(st_ref)

    row = lax.broadcasted_iota(jnp.int32, (CHUNK, CHUNK), 0)
    col = lax.broadcasted_iota(jnp.int32, (CHUNK, CHUNK), 1)
    low = col < row
    diag = col == row
    me = _even_lane_mask()
    sel = _head_selector(0)
    tri = _cumsum_matrix()

    def body(c, carry):
        rows = pl.ds(pl.multiple_of(c * CHUNK, CHUNK), CHUNK)
        incl, excl, dts, w, sc, dec = _chunk_decays(dtp_ref[0, rows, :], tri)
        b_t = [_b_transposed(xo_ref, rows, g) for g in range(SSD_GROUPS)]
        y = _state_step(xo_ref, rows, st_ref, _expand_decays(w, sc, dec, sel), b_t)

        incl_t = incl.T
        excl_t = excl.T
        dts_t = dts.T
        y_diag = []
        for g in range(SSD_GROUPS):
            c_g = xo_ref[0, rows, SSD_INNER + (SSD_GROUPS + g) * SSD_STATE:
                         SSD_INNER + (SSD_GROUPS + g + 1) * SSD_STATE]
            cb = _dot(c_g, b_t[g])
            for pr in range(2):
                xs_pair = xo_ref[0, rows, g * GROUP_W + pr * LANES:g * GROUP_W + (pr + 1) * LANES]
                wms = []
                for e in range(2):
                    hd = 4 * g + 2 * pr + e
                    af_c = jnp.broadcast_to(incl[:, hd:hd + 1], (CHUNK, CHUNK))
                    eb_c = jnp.broadcast_to(excl[:, SSD_HEADS + hd:SSD_HEADS + hd + 1], (CHUNK, CHUNK))
                    af_r = incl_t[hd:hd + 1, :]
                    eb_r = excl_t[SSD_HEADS + hd:SSD_HEADS + hd + 1, :]
                    df_r = dts_t[hd:hd + 1, :]
                    db_r = dts_t[SSD_HEADS + hd:SSD_HEADS + hd + 1, :]
                    arg = jnp.where(low, af_c - af_r, eb_r - eb_c)
                    coef = jnp.where(low, df_r, jnp.where(diag, df_r + db_r, db_r))
                    wms.append((cb * jnp.exp(arg) * coef).astype(bf16))
                both = _dot(jnp.concatenate(wms, axis=0), xs_pair)
                y_diag.append(jnp.where(me, both[0:CHUNK], both[CHUNK:2 * CHUNK]))
        xs = xo_ref[0, rows, 0:SSD_INNER].astype(f32)
        y = y + jnp.concatenate(y_diag, axis=1) + yb_ref[0, rows, :].astype(f32) + xs * dsk_ref[...]
        o_ref[0, rows, :] = y.astype(bf16)
        return carry

    lax.fori_loop(0, SSD_TILE // CHUNK, body, 0, unroll=True)


def _ssd_fwd(xo, dtp, yb, d_skip_x):
    B, L, _ = xo.shape
    T = SSD_TILE
    tok = lambda b, i: (b, i, 0)
    const = lambda b, i: (0, 0)
    return pl.pallas_call(
        _ssd_fwd_kernel,
        grid=(B, L // T),
        in_specs=[pl.BlockSpec((1, T, CONV_CH), tok),
                  pl.BlockSpec((1, T, LANES), tok),
                  pl.BlockSpec((1, T, SSD_INNER), tok),
                  pl.BlockSpec((1, SSD_INNER), const)],
        out_specs=pl.BlockSpec((1, T, SSD_INNER), tok),
        out_shape=jax.ShapeDtypeStruct((B, L, SSD_INNER), bf16),
        scratch_shapes=[pltpu.VMEM((SSD_GROUPS, SSD_STATE, GROUP_W), f32)],
        compiler_params=pltpu.CompilerParams(dimension_semantics=("arbitrary", "arbitrary"),
                                             vmem_limit_bytes=VMEM_LIMIT),
        name="ssd_fwd",
    )(xo, dtp, yb, d_skip_x)


def _na_bias_tables(rpb):
    variants = [(0, 0, 0), (2, 0, 0), (4, 0, 1), (6, 2, 2), (8, 2, 2)]
    kc = np.arange(GRID_W)[:, None]
    w = np.arange(GRID_W)[None, :]
    cs = np.clip(w - NA_COLS // 2, 0, GRID_W - NA_COLS)
    col_ok = (kc >= cs) & (kc < cs + NA_COLS)
    taps = np.arange(2 * NA_COLS - 1)[:, None, None]
    onehot = (((kc - w + NA_COLS - 1)[None] == taps) & col_ok[None]).astype(np.float32)
    base = jnp.einsum('hrk,kcw->hrcw', rpb.astype(f32) * LOG2E, onehot, precision=lax.Precision.HIGHEST)
    base = jnp.where(col_ok, base, NEG_BIG).reshape(HEAD_PAIRS, 2, 2 * NA_ROWS - 1, GRID_W, GRID_W)
    basep = jnp.pad(base, ((0, 0), (0, 0), (2, 2), (0, 0), (0, 0)), constant_values=NEG_BIG)
    n_blk = 2 * NA_ROWS + 2
    blk = jnp.concatenate([basep[:, 0, 1:1 + n_blk], basep[:, 0, 0:n_blk],
                           basep[:, 1, 1:1 + n_blk], basep[:, 1, 0:n_blk]], axis=-1)
    kr = np.arange(NA_WIN)[:, None] // GRID_W
    par = (np.arange(2 * LANES)[None, :] // GRID_W) % 2
    tabs = []
    for roff, st0, st1 in variants:
        st = np.where(par == 0, st0, st1)
        valid = (kr >= st) & (kr < st + NA_ROWS)
        j0 = NA_ROWS - roff
        rows = blk[:, j0:j0 + NA_WIN_ROWS].reshape(HEAD_PAIRS, NA_WIN, 2 * LANES)
        tabs.append(jnp.where(valid, rows, NEG_BIG))
    return jnp.stack(tabs)


def _na_kernel(q_ref, kp_ref, kc_ref, kn_ref, vp_ref, vc_ref, vn_ref, tab_ref, o_ref, kwin_ref, vwin_ref,
               *, grid_rows):
    step = pl.program_id(2)
    me = _even_lane_mask()

    kwin_ref[0:NA_HALO, :] = kp_ref[0, 0]
    kwin_ref[NA_HALO:NA_HALO + NA_QT, :] = kc_ref[0, 0]
    kwin_ref[NA_HALO + NA_QT:, :] = kn_ref[0, 0]
    ht = NA_HALO // LANES
    vwin_ref[0:ht] = vp_ref[0, 0]
    vwin_ref[ht:ht + NA_QT // LANES] = vc_ref[0, 0]
    vwin_ref[ht + NA_QT // LANES:] = vn_ref[0, 0]
    tok0 = step * NA_QT - NA_HALO

    def window(i):
        r = 2 * (step * NA_PAIRS_PER_STEP + i)
        u0 = jnp.clip(r - NA_ROWS // 2, 0, grid_rows - NA_WIN_ROWS)
        var = jnp.where(r < 4, r // 2, jnp.where(r >= grid_rows - 4, 3 + (r - (grid_rows - 4)) // 2, 2))
        return u0, var

    def run(win_rows):
        n_keys = win_rows * GRID_W
        half = n_keys // 2

        def scores(i):
            u0, var = window(i)
            q2 = q_ref[0, 0, i * LANES:(i + 1) * LANES, :]
            zero = jnp.zeros_like(q2)
            qm = jnp.concatenate([jnp.where(me, q2, zero), jnp.where(me, zero, q2)], axis=0)
            parts = []
            for hk in range(2):
                start = pl.multiple_of(u0 * GRID_W - tok0 + hk * half, BF16_ROWS)
                kk = kwin_ref[pl.ds(start, half), :]
                parts.append(_dot_nt(kk, qm) + tab_ref[var, 0, hk * half:(hk + 1) * half, :])
            return jnp.concatenate(parts, axis=0)

        def finish(i, parts, rl):
            oc = jnp.concatenate([parts[e] * rl[:, e * LANES:(e + 1) * LANES] for e in range(2)], axis=0)
            o_ref[0, 0, i * LANES:(i + 1) * LANES, :] = oc.T.astype(bf16)

        s = scores(0)
        pending = None
        for i in range(NA_PAIRS_PER_STEP):
            s_next = scores(i + 1) if i + 1 < NA_PAIRS_PER_STEP else None
            u0, _ = window(i)
            m = jnp.max(s, axis=0, keepdims=True)
            p = jnp.exp2(s - m)
            l = jnp.sum(p, axis=0, keepdims=True)
            j0 = lax.shift_right_logical(u0 * GRID_W - tok0, 7)
            vtw = jnp.concatenate([vwin_ref[j0 + j] for j in range(NA_WIN // LANES)], axis=1)
            pb = p.astype(bf16)
            if n_keys < NA_WIN:
                pb = jnp.concatenate([pb, jnp.zeros((NA_WIN - n_keys, 2 * LANES), bf16)], axis=0)
            parts = [_dot(vtw[e * HEAD_DIM:(e + 1) * HEAD_DIM, :], pb[:, e * LANES:(e + 1) * LANES])
                     for e in range(2)]
            if pending is not None:
                finish(*pending)
            pending = (i, parts, 1.0 / l)
            s = s_next
        finish(*pending)

    last = pl.num_programs(2) - 1
    pl.when(step != last)(functools.partial(run, NA_WIN_ROWS - 1))
    pl.when(step == last)(functools.partial(run, NA_WIN_ROWS))


def _na(q, k, vt, tab):
    B, _, L, _ = q.shape
    grid_rows = L // GRID_W
    QT, H = NA_QT, NA_HALO
    r = QT // H
    nh = L // H
    nv = tab.shape[0]
    cur = lambda p, b, s: (b, p, s, 0)
    prev = lambda p, b, s: (b, p, jnp.maximum(s * r - 1, 0), 0)
    nxt = lambda p, b, s: (b, p, jnp.minimum((s + 1) * r, nh - 1), 0)
    tile = lambda f: (lambda p, b, s: f(p, b, s) + (0,))
    ht = H // LANES
    return pl.pallas_call(
        functools.partial(_na_kernel, grid_rows=grid_rows),
        grid=(HEAD_PAIRS, B, L // QT),
        in_specs=[pl.BlockSpec((1, 1, QT, LANES), cur),
                  pl.BlockSpec((1, 1, H, LANES), prev),
                  pl.BlockSpec((1, 1, QT, LANES), cur),
                  pl.BlockSpec((1, 1, H, LANES), nxt),
                  pl.BlockSpec((1, 1, ht, LANES, LANES), tile(prev)),
                  pl.BlockSpec((1, 1, QT // LANES, LANES, LANES), tile(cur)),
                  pl.BlockSpec((1, 1, ht, LANES, LANES), tile(nxt)),
                  pl.BlockSpec((nv, 1, NA_WIN, 2 * LANES), lambda p, b, s: (0, p, 0, 0))],
        out_specs=pl.BlockSpec((1, 1, QT, LANES), cur),
        out_shape=jax.ShapeDtypeStruct((B, HEAD_PAIRS, L, LANES), bf16),
        scratch_shapes=[pltpu.VMEM((QT + 2 * H, LANES), bf16),
                        pltpu.VMEM((QT // LANES + 2 * ht, LANES, LANES), bf16)],
        compiler_params=pltpu.CompilerParams(dimension_semantics=("arbitrary", "arbitrary", "arbitrary"),
                                             vmem_limit_bytes=VMEM_LIMIT),
        name="na",
    )(q, k, k, k, vt, vt, vt, tab)


def _out_kernel(x_ref, ys_ref, z_ref, yn_ref, mod_ref, gs_ref, wo_ref, gf_ref, wg_ref, wu_ref, wd_ref, o_ref, a_ref):
    half = TOKEN_TILE // TOKEN_SPLIT

    def gate_norm(rows):
        gated = ys_ref[0, rows, :].astype(f32) * _silu(z_ref[0, rows, :].astype(f32))
        gms = jnp.mean(gated * gated, axis=-1, keepdims=True)
        return (gated * lax.rsqrt(gms + EPS) * gs_ref[...]).astype(bf16)

    def out_proj(rows, y_ssd):
        y_na = jnp.concatenate([yn_ref[0, p, rows, :] for p in range(HEAD_PAIRS)], axis=1)
        mix = _dot(y_ssd, wo_ref[0:SSD_INNER, :]) + _dot(y_na, wo_ref[SSD_INNER:, :])
        return x_ref[0, rows, :] + mod_ref[0, 2:3, :] * mix

    def ffn_norm(x1):
        ms = jnp.mean(x1 * x1, axis=-1, keepdims=True)
        h2 = x1 * lax.rsqrt(ms + EPS) * gf_ref[...]
        return (h2 * (1.0 + mod_ref[0, 4:5, :]) + mod_ref[0, 3:4, :]).astype(bf16)

    def ffn_block(rows, h2, c):
        cols = slice(c * FF_BLOCK, (c + 1) * FF_BLOCK)
        gate = _dot(h2, wg_ref[:, cols])
        up = _dot(h2, wu_ref[:, cols])
        a_ref[rows, cols] = (_silu(gate) * up).astype(bf16)

    def ffn_down(rows, x1):
        ff = _dot(a_ref[rows, :], wd_ref[...])
        o_ref[0, rows, :] = x1 + mod_ref[0, 5:6, :] * ff

    for s in range(TOKEN_SPLIT):
        rows = slice(s * half, (s + 1) * half)
        x1 = out_proj(rows, gate_norm(rows))
        h2 = ffn_norm(x1)
        for c in range(D_FF // FF_BLOCK):
            ffn_block(rows, h2, c)
        ffn_down(rows, x1)


def _out_ffn(x, y_pre, z, y_na, mod, g_ssd, w_out, g_ffn, w_gate, w_up, w_down):
    B, L, _ = x.shape
    T = TOKEN_TILE
    tok = lambda b, i: (b, i, 0)
    const = lambda b, i: (0, 0)
    single = pl.Buffered(1)
    return pl.pallas_call(
        _out_kernel,
        grid=(B, L // T),
        in_specs=[pl.BlockSpec((1, T, D_MODEL), tok),
                  pl.BlockSpec((1, T, SSD_INNER), tok),
                  pl.BlockSpec((1, T, SSD_INNER), tok),
                  pl.BlockSpec((1, HEAD_PAIRS, T, LANES), lambda b, i: (b, 0, i, 0)),
                  pl.BlockSpec((1, 6, D_MODEL), lambda b, i: (b, 0, 0)),
                  pl.BlockSpec((1, SSD_INNER), const),
                  pl.BlockSpec((D_MODEL, D_MODEL), const, pipeline_mode=single),
                  pl.BlockSpec((1, D_MODEL), const),
                  pl.BlockSpec((D_MODEL, D_FF), const, pipeline_mode=single),
                  pl.BlockSpec((D_MODEL, D_FF), const, pipeline_mode=single),
                  pl.BlockSpec((D_FF, D_MODEL), const, pipeline_mode=single)],
        out_specs=pl.BlockSpec((1, T, D_MODEL), tok),
        out_shape=jax.ShapeDtypeStruct((B, L, D_MODEL), f32),
        scratch_shapes=[pltpu.VMEM((T, D_FF), bf16)],
        compiler_params=pltpu.CompilerParams(dimension_semantics=("arbitrary", "arbitrary"),
                                             vmem_limit_bytes=VMEM_LIMIT),
        name="out_ffn",
    )(x, y_pre, z, y_na, mod, g_ssd, w_out, g_ffn, w_gate, w_up, w_down)


def _pad_lanes(parts):
    pieces, pos = [], 0
    for off, v in parts:
        pieces += [jnp.zeros((off - pos,), f32), v.astype(f32)]
        pos = off + v.shape[0]
    pieces.append(jnp.zeros((LANES - pos,), f32))
    return jnp.concatenate(pieces).reshape(1, LANES)


def kernel(x_prompt, x_sample, c_prompt, c_sample, w_ada, b_ada, g_mix, w_in, conv_w, conv_b, dt_bias_fwd, dt_bias_bwd, a_log_fwd, a_log_bwd, d_skip, g_ssd, g_q, g_k, rpb, w_out, g_ffn, w_gate, w_up, w_down):
    l = 0
    nb_p = x_prompt.shape[0]
    nb_s = x_sample.shape[0]

    c_all = jnp.concatenate([c_prompt, c_sample, jnp.zeros((8 - nb_p - nb_s, D_MODEL), f32)], axis=0)
    mod = _modulation(c_all.T, w_ada[l], b_ada[l], nb_p + nb_s).reshape(8, 6, D_MODEL)

    o1 = SSD_INNER
    o2 = o1 + CONV_CH
    o3 = o2 + 2 * SSD_HEADS
    o4 = o3 + NA_INNER
    o5 = o4 + NA_INNER
    wi = w_in[l]
    w_zx = wi[:, :o2].astype(bf16)
    w_qk = wi[:, o3:o5].astype(bf16)
    w_vt = jnp.concatenate([wi[:, o5:], wi[:, o2:o3], wi[:, o2:o3]], axis=1).astype(bf16).T
    dtb = _pad_lanes([(0, dt_bias_fwd[l]), (8, dt_bias_bwd[l]), (64, dt_bias_fwd[l]), (72, dt_bias_bwd[l])])
    alog = _pad_lanes([(0, a_log_fwd[l]), (8, a_log_bwd[l])])
    gq = jnp.tile(g_q[l].astype(f32) * (HEAD_DIM ** -0.5 * LOG2E), NA_HEADS).reshape(1, NA_INNER)
    gk = jnp.tile(g_k[l].astype(f32), NA_HEADS).reshape(1, NA_INNER)
    dsk = jnp.repeat(d_skip[l].astype(f32), HEAD_DIM).reshape(1, SSD_INNER)
    tab = _na_bias_tables(rpb[l])
    wo = w_out[l].astype(bf16)
    wg = w_gate[l].astype(bf16)
    wu = w_up[l].astype(bf16)
    wd = w_down[l].astype(bf16)
    row = lambda v: v.astype(f32).reshape(1, -1)

    def trunk(x, m):
        z, xbc, dtp, q, k, vt = _in_proj(x, m, row(g_mix[l]), w_zx, w_qk, w_vt, gq, gk, dtb, alog)
        xo, yb = _ssd_bwd(xbc, dtp, conv_w[l].astype(f32), row(conv_b[l]))
        y_pre = _ssd_fwd(xo, dtp, yb, dsk)
        y_na = _na(q, k, vt, tab)
        return _out_ffn(x, y_pre, z, y_na, m, row(g_ssd[l]), wo, row(g_ffn[l]), wg, wu, wd)

    return (trunk(x_prompt, mod[:nb_p]), trunk(x_sample, mod[nb_p:nb_p + nb_s]))
```

```python
import functools
import math

import numpy as np
import jax
import jax.numpy as jnp
from jax import lax
from jax.experimental import pallas as pl
from jax.experimental.pallas import tpu as pltpu

D_MODEL = 1024
HEAD_DIM = 64
SSD_HEADS = 8
SSD_INNER = SSD_HEADS * HEAD_DIM
SSD_GROUPS = 2
SSD_STATE = 128
CONV_K = 5
CHUNK = 128
NA_HEADS = 8
NA_INNER = NA_HEADS * HEAD_DIM
NA_ROWS = 8
NA_COLS = 16
GRID_W = 64
CONV_CH = SSD_INNER + 2 * SSD_GROUPS * SSD_STATE
D_FF = 2816
EPS = 1e-6

LANES = 128
BF16_ROWS = 16
CONV_HALO = BF16_ROWS
CONV_ROWS = 128
CONV_COLS = 256
GROUP_W = (SSD_HEADS // SSD_GROUPS) * HEAD_DIM
HEAD_PAIRS = NA_HEADS // 2
LOG2E = math.log2(math.e)
NEG_BIG = -1e30

TOKEN_TILE = 1024
TOKEN_SPLIT = 1
SSD_TILE = 1024
NA_WIN_ROWS = 10
NA_WIN = NA_WIN_ROWS * GRID_W
NA_PAIRS_PER_STEP = 16
NA_QT = NA_PAIRS_PER_STEP * 2 * GRID_W
NA_HALO = 256
FF_BLOCK = 256
VMEM_LIMIT = 56 * 1024 * 1024

f32 = jnp.float32
bf16 = jnp.bfloat16


def _dot(a, b):
    return jnp.dot(a, b, preferred_element_type=f32)


def _dot_nt(a, b):
    return lax.dot_general(a, b, (((1,), (1,)), ((), ())), preferred_element_type=f32)


def _split3(v):
    p1 = v.astype(bf16)
    r1 = v - p1.astype(f32)
    p2 = r1.astype(bf16)
    p3 = (r1 - p2.astype(f32)).astype(bf16)
    return p1, p2, p3


def _even_lane_mask():
    return lax.broadcasted_iota(jnp.int32, (1, LANES), 1) < HEAD_DIM


def _silu(v):
    return v * jax.nn.sigmoid(v)


def _mod_kernel(ct_ref, w_ref, b_ref, o_ref, *, n_rows):
    st = _silu(ct_ref[...])
    w = w_ref[...]
    rows = [jnp.sum(w * st[:, b:b + 1], axis=0, keepdims=True) for b in range(n_rows)]
    rows.append(jnp.zeros((8 - n_rows, w.shape[1]), f32))
    o_ref[...] = jnp.concatenate(rows, axis=0) + b_ref[...]


def _modulation(c_t, w_ada, b_ada, n_rows):
    n = 6 * D_MODEL
    return pl.pallas_call(
        functools.partial(_mod_kernel, n_rows=n_rows),
        grid=(n // D_MODEL,),
        in_specs=[pl.BlockSpec((D_MODEL, 8), lambda j: (0, 0)),
                  pl.BlockSpec((D_MODEL, D_MODEL), lambda j: (0, j)),
                  pl.BlockSpec((1, D_MODEL), lambda j: (0, j))],
        out_specs=pl.BlockSpec((8, D_MODEL), lambda j: (0, j)),
        out_shape=jax.ShapeDtypeStruct((8, n), f32),
        name="mod",
    )(c_t, w_ada, b_ada.reshape(1, n))


def _head_rmsnorm(v, gain):
    me = _even_lane_mask()
    s = v * v
    se = jnp.sum(jnp.where(me, s, 0.0), axis=-1, keepdims=True)
    so = jnp.sum(jnp.where(me, 0.0, s), axis=-1, keepdims=True)
    ms = jnp.where(me, se, so) * (1.0 / HEAD_DIM)
    return v * lax.rsqrt(ms + EPS) * gain


def _in_kernel(x_ref, mod_ref, g_ref, w_ref, wvt_ref, gq_ref, gk_ref, dtb_ref, alog_ref,
               z_ref, xbc_ref, dtp_ref, q_ref, k_ref, vt_ref):
    half = TOKEN_TILE // TOKEN_SPLIT

    def norm_mod(rows):
        x = x_ref[0, rows, :]
        ms = jnp.mean(x * x, axis=-1, keepdims=True)
        y = x * lax.rsqrt(ms + EPS) * g_ref[...]
        return (y * (1.0 + mod_ref[0, 1:2, :]) + mod_ref[0, 0:1, :]).astype(bf16)

    def proj_xbc(rows, h):
        xbc_ref[0, rows, :] = _dot(h, w_ref[:, SSD_INNER:SSD_INNER + CONV_CH]).astype(bf16)

    def proj_z(rows, h):
        z_ref[0, rows, :] = _dot(h, w_ref[:, 0:SSD_INNER]).astype(bf16)

    def proj_head_normed(rows, h, dst_ref, gain_ref, col0):
        full = _dot(h, w_ref[:, col0:col0 + NA_INNER])
        for j in range(HEAD_PAIRS):
            sl = slice(j * LANES, (j + 1) * LANES)
            dst_ref[0, j, rows, :] = _head_rmsnorm(full[:, sl], gain_ref[:, sl]).astype(bf16)

    def proj_v_dt(rows, h):
        full = _dot_nt(wvt_ref[...], h)
        vt = full[0:NA_INNER].astype(bf16)
        tb0 = rows.start // LANES
        for p in range(HEAD_PAIRS):
            for tb in range(half // LANES):
                vt_ref[0, p, tb0 + tb] = vt[p * LANES:(p + 1) * LANES, tb * LANES:(tb + 1) * LANES]
        lane = lax.broadcasted_iota(jnp.int32, (1, LANES), 1)
        a_neg = -jnp.exp(alog_ref[...])
        mul = jnp.where(lane < 2 * SSD_HEADS, a_neg,
                        jnp.where((lane >= 64) & (lane < 64 + 2 * SSD_HEADS), 1.0, 0.0))
        n_dt = 2 * SSD_HEADS
        gap = jnp.zeros((LANES // 2 - n_dt, LANES), f32)
        for tb in range(half // LANES):
            cols = slice(tb * LANES, (tb + 1) * LANES)
            raw_t = jnp.concatenate([full[NA_INNER:NA_INNER + n_dt, cols], gap,
                                     full[NA_INNER + n_dt:NA_INNER + 2 * n_dt, cols], gap], axis=0)
            raw = raw_t.T + dtb_ref[...]
            dt = jnp.maximum(raw, 0.0) + jnp.log1p(jnp.exp(-jnp.abs(raw)))
            dtp_ref[0, rows.start + tb * LANES:rows.start + (tb + 1) * LANES, :] = dt * mul

    o = SSD_INNER + CONV_CH
    for s in range(TOKEN_SPLIT):
        rows = slice(s * half, (s + 1) * half)
        h = norm_mod(rows)
        proj_xbc(rows, h)
        proj_z(rows, h)
        proj_head_normed(rows, h, q_ref, gq_ref, o)
        proj_head_normed(rows, h, k_ref, gk_ref, o + NA_INNER)
        proj_v_dt(rows, h)


def _in_proj(x, mod, g_mix, w_main, w_vt, gq, gk, dt_bias, a_log):
    B, L, _ = x.shape
    T = TOKEN_TILE
    nw = w_main.shape[1]
    const = lambda b, i: (0, 0)
    tok = lambda b, i: (b, i, 0)
    return pl.pallas_call(
        _in_kernel,
        grid=(B, L // T),
        in_specs=[pl.BlockSpec((1, T, D_MODEL), tok),
                  pl.BlockSpec((1, 6, D_MODEL), lambda b, i: (b, 0, 0)),
                  pl.BlockSpec((1, D_MODEL), const),
                  pl.BlockSpec((D_MODEL, nw), const),
                  pl.BlockSpec((NA_INNER + 4 * SSD_HEADS, D_MODEL), const),
                  pl.BlockSpec((1, NA_INNER), const),
                  pl.BlockSpec((1, NA_INNER), const),
                  pl.BlockSpec((1, LANES), const),
                  pl.BlockSpec((1, LANES), const)],
        out_specs=[pl.BlockSpec((1, T, SSD_INNER), tok),
                   pl.BlockSpec((1, T, CONV_CH), tok),
                   pl.BlockSpec((1, T, LANES), tok),
                   pl.BlockSpec((1, HEAD_PAIRS, T, LANES), lambda b, i: (b, 0, i, 0)),
                   pl.BlockSpec((1, HEAD_PAIRS, T, LANES), lambda b, i: (b, 0, i, 0)),
                   pl.BlockSpec((1, HEAD_PAIRS, T // LANES, LANES, LANES), lambda b, i: (b, 0, i, 0, 0))],
        out_shape=[jax.ShapeDtypeStruct((B, L, SSD_INNER), bf16),
                   jax.ShapeDtypeStruct((B, L, CONV_CH), bf16),
                   jax.ShapeDtypeStruct((B, L, LANES), f32),
                   jax.ShapeDtypeStruct((B, HEAD_PAIRS, L, LANES), bf16),
                   jax.ShapeDtypeStruct((B, HEAD_PAIRS, L, LANES), bf16),
                   jax.ShapeDtypeStruct((B, HEAD_PAIRS, L // LANES, LANES, LANES), bf16)],
        compiler_params=pltpu.CompilerParams(dimension_semantics=("arbitrary", "arbitrary"),
                                             vmem_limit_bytes=VMEM_LIMIT),
        name="in_proj",
    )(x, mod, g_mix, w_main, w_vt, gq, gk, dt_bias, a_log)


def _head_selector(base):
    r = lax.broadcasted_iota(jnp.int32, (LANES, SSD_INNER), 0)
    c = lax.broadcasted_iota(jnp.int32, (LANES, SSD_INNER), 1)
    return (lax.shift_right_logical(c, 6) == r - base).astype(bf16)


def _cumsum_matrix():
    row = lax.broadcasted_iota(jnp.int32, (CHUNK, CHUNK), 0)
    col = lax.broadcasted_iota(jnp.int32, (CHUNK, CHUNK), 1)
    return (col <= row).astype(bf16)


def _chunk_decays(p, tri):
    p1, p2, p3 = _split3(p)
    parts = _dot(tri, jnp.concatenate([p1, p2, p3], axis=1))
    incl = parts[:, 0:LANES] + parts[:, LANES:2 * LANES] + parts[:, 2 * LANES:3 * LANES]
    excl = incl - p
    dts = pltpu.roll(p, 64, 1)
    tot = incl[CHUNK - 1:CHUNK, :]
    is_fwd = lax.broadcasted_iota(jnp.int32, (1, LANES), 1) < SSD_HEADS
    w = jnp.exp(jnp.where(is_fwd, tot - incl, excl)) * dts
    sc = jnp.exp(jnp.where(is_fwd, incl, tot - excl))
    return incl, excl, dts, w, sc, jnp.exp(tot)


def _b_transposed(xo_ref, rows, g):
    b_g = xo_ref[0, rows, SSD_INNER + g * SSD_STATE:SSD_INNER + (g + 1) * SSD_STATE].astype(f32)
    return b_g.T.astype(bf16)


def _expand_decays(w, sc, dec, sel):
    v = jnp.concatenate([w, sc, jnp.broadcast_to(dec, (BF16_ROWS, LANES))], axis=0)
    n = v.shape[0]
    hi = v.astype(bf16)
    lo = (v - hi.astype(f32)).astype(bf16)
    both = _dot(jnp.concatenate([hi, lo], axis=0), sel)
    x = both[0:n] + both[n:2 * n]
    return x[0:CHUNK], x[CHUNK:2 * CHUNK], x[2 * CHUNK:2 * CHUNK + 1]


def _state_step(xo_ref, rows, st_ref, expanded, b_t):
    w_x, sc_x, dec_x = expanded
    outs = []
    for g in range(SSD_GROUPS):
        gcols = slice(g * GROUP_W, (g + 1) * GROUP_W)
        xs_g = xo_ref[0, rows, g * GROUP_W:(g + 1) * GROUP_W].astype(f32)
        c_g = xo_ref[0, rows, SSD_INNER + (SSD_GROUPS + g) * SSD_STATE:
                     SSD_INNER + (SSD_GROUPS + g + 1) * SSD_STATE]
        s_g = st_ref[g]
        y_off = _dot(c_g, s_g.astype(bf16)) * sc_x[:, gcols]
        xw = (xs_g * w_x[:, gcols]).astype(bf16)
        st_ref[g] = s_g * dec_x[:, gcols] + _dot(b_t[g], xw)
        outs.append(y_off)
    return jnp.concatenate(outs, axis=1)


def _ssd_bwd_kernel(xc_ref, xp_ref, xn_ref, dtp_ref, cw_ref, cb_ref, xo_ref, yb_ref, win_ref, st_ref, *, n_tiles):
    i = pl.program_id(1)
    t = n_tiles - 1 - i

    @pl.when(i == 0)
    def _():
        st_ref[...] = jnp.zeros_like(st_ref)

    H = CONV_HALO
    win_ref[0:H, :] = jnp.where(t > 0, xp_ref[0], jnp.zeros_like(xp_ref[0]))
    win_ref[H:H + SSD_TILE, :] = xc_ref[0]
    win_ref[H + SSD_TILE:, :] = jnp.where(t < n_tiles - 1, xn_ref[0], jnp.zeros_like(xn_ref[0]))
    r_i = lax.broadcasted_iota(jnp.int32, (CONV_ROWS, CONV_ROWS + 2 * H), 0)
    c_i = lax.broadcasted_iota(jnp.int32, (CONV_ROWS, CONV_ROWS + 2 * H), 1)
    mid = CONV_K // 2
    shift = {k: (c_i == r_i + H + k - mid).astype(bf16) for k in range(CONV_K) if k != mid}
    for c in range(SSD_TILE // CONV_ROWS):
        for cb in range(CONV_CH // CONV_COLS):
            cols = slice(cb * CONV_COLS, (cb + 1) * CONV_COLS)
            xw = win_ref[c * CONV_ROWS:(c + 1) * CONV_ROWS + 2 * H, cols]
            centre = win_ref[H + c * CONV_ROWS:H + (c + 1) * CONV_ROWS, cols].astype(f32)
            acc = cb_ref[:, cols] + centre * cw_ref[mid:mid + 1, cols]
            for k in shift:
                acc = acc + _dot(shift[k], xw) * cw_ref[k:k + 1, cols]
            xo_ref[0, c * CONV_ROWS:(c + 1) * CONV_ROWS, cols] = _silu(acc).astype(bf16)

    sel = _head_selector(SSD_HEADS)
    tri = _cumsum_matrix()

    def body(j, carry):
        c = SSD_TILE // CHUNK - 1 - j
        rows = pl.ds(pl.multiple_of(c * CHUNK, CHUNK), CHUNK)
        _, _, _, w, sc, dec = _chunk_decays(dtp_ref[0, rows, :], tri)
        b_t = [_b_transposed(xo_ref, rows, g) for g in range(SSD_GROUPS)]
        yb_ref[0, rows, :] = _state_step(xo_ref, rows, st_ref, _expand_decays(w, sc, dec, sel), b_t).astype(bf16)
        return carry

    lax.fori_loop(0, SSD_TILE // CHUNK, body, 0, unroll=True)


def _ssd_bwd(xbc, dtp, conv_w, conv_b):
    B, L, _ = xbc.shape
    T = SSD_TILE
    nT = L // T
    hb = T // CONV_HALO
    cur = lambda b, i: (b, nT - 1 - i, 0)
    prev = lambda b, i: (b, jnp.maximum((nT - 1 - i) * hb - 1, 0), 0)
    nxt = lambda b, i: (b, jnp.minimum((nT - i) * hb, L // CONV_HALO - 1), 0)
    const = lambda b, i: (0, 0)
    return pl.pallas_call(
        functools.partial(_ssd_bwd_kernel, n_tiles=nT),
        grid=(B, nT),
        in_specs=[pl.BlockSpec((1, T, CONV_CH), cur),
                  pl.BlockSpec((1, CONV_HALO, CONV_CH), prev),
                  pl.BlockSpec((1, CONV_HALO, CONV_CH), nxt),
                  pl.BlockSpec((1, T, LANES), cur),
                  pl.BlockSpec((CONV_K, CONV_CH), const),
                  pl.BlockSpec((1, CONV_CH), const)],
        out_specs=[pl.BlockSpec((1, T, CONV_CH), cur),
                   pl.BlockSpec((1, T, SSD_INNER), cur)],
        out_shape=[jax.ShapeDtypeStruct((B, L, CONV_CH), bf16),
                   jax.ShapeDtypeStruct((B, L, SSD_INNER), bf16)],
        scratch_shapes=[pltpu.VMEM((T + 2 * CONV_HALO, CONV_CH), bf16),
                        pltpu.VMEM((SSD_GROUPS, SSD_STATE, GROUP_W), f32)],
        compiler_params=pltpu.CompilerParams(dimension_semantics=("arbitrary", "arbitrary"),
                                             vmem_limit_bytes=VMEM_LIMIT),
        name="ssd_bwd",
    )(xbc, xbc, xbc, dtp, conv_w, conv_b)


def _ssd_fwd_kernel(xo_ref, dtp_ref, yb_ref, dsk_ref, o_ref, st_ref):
    @pl.when(pl.program_id(1) == 0)
    def _():
        st_ref[...] = jnp.zeros_like(st_ref)

    row = lax.broadcasted_iota(jnp.int32, (CHUNK, CHUNK), 0)
    col = lax.broadcasted_iota(jnp.int32, (CHUNK, CHUNK), 1)
    low = col < row
    diag = col == row
    me = _even_lane_mask()
    sel = _head_selector(0)
    tri = _cumsum_matrix()

    def body(c, carry):
        rows = pl.ds(pl.multiple_of(c * CHUNK, CHUNK), CHUNK)
        incl, excl, dts, w, sc, dec = _chunk_decays(dtp_ref[0, rows, :], tri)
        b_t = [_b_transposed(xo_ref, rows, g) for g in range(SSD_GROUPS)]
        y = _state_step(xo_ref, rows, st_ref, _expand_decays(w, sc, dec, sel), b_t)

        incl_t = incl.T
        excl_t = excl.T
        dts_t = dts.T
        y_diag = []
        for g in range(SSD_GROUPS):
            c_g = xo_ref[0, rows, SSD_INNER + (SSD_GROUPS + g) * SSD_STATE:
                         SSD_INNER + (SSD_GROUPS + g + 1) * SSD_STATE]
            cb = _dot(c_g, b_t[g])
            for pr in range(2):
                xs_pair = xo_ref[0, rows, g * GROUP_W + pr * LANES:g * GROUP_W + (pr + 1) * LANES]
                wms = []
                for e in range(2):
                    hd = 4 * g + 2 * pr + e
                    af_c = jnp.broadcast_to(incl[:, hd:hd + 1], (CHUNK, CHUNK))
                    eb_c = jnp.broadcast_to(excl[:, SSD_HEADS + hd:SSD_HEADS + hd + 1], (CHUNK, CHUNK))
                    af_r = incl_t[hd:hd + 1, :]
                    eb_r = excl_t[SSD_HEADS + hd:SSD_HEADS + hd + 1, :]
                    df_r = dts_t[hd:hd + 1, :]
                    db_r = dts_t[SSD_HEADS + hd:SSD_HEADS + hd + 1, :]
                    arg = jnp.where(low, af_c - af_r, eb_r - eb_c)
                    coef = jnp.where(low, df_r, jnp.where(diag, df_r + db_r, db_r))
                    wms.append((cb * jnp.exp(arg) * coef).astype(bf16))
                both = _dot(jnp.concatenate(wms, axis=0), xs_pair)
                y_diag.append(jnp.where(me, both[0:CHUNK], both[CHUNK:2 * CHUNK]))
        xs = xo_ref[0, rows, 0:SSD_INNER].astype(f32)
        y = y + jnp.concatenate(y_diag, axis=1) + yb_ref[0, rows, :].astype(f32) + xs * dsk_ref[...]
        o_ref[0, rows, :] = y.astype(bf16)
        return carry

    lax.fori_loop(0, SSD_TILE // CHUNK, body, 0, unroll=True)


def _ssd_fwd(xo, dtp, yb, d_skip_x):
    B, L, _ = xo.shape
    T = SSD_TILE
    tok = lambda b, i: (b, i, 0)
    const = lambda b, i: (0, 0)
    return pl.pallas_call(
        _ssd_fwd_kernel,
        grid=(B, L // T),
        in_specs=[pl.BlockSpec((1, T, CONV_CH), tok),
                  pl.BlockSpec((1, T, LANES), tok),
                  pl.BlockSpec((1, T, SSD_INNER), tok),
                  pl.BlockSpec((1, SSD_INNER), const)],
        out_specs=pl.BlockSpec((1, T, SSD_INNER), tok),
        out_shape=jax.ShapeDtypeStruct((B, L, SSD_INNER), bf16),
        scratch_shapes=[pltpu.VMEM((SSD_GROUPS, SSD_STATE, GROUP_W), f32)],
        compiler_params=pltpu.CompilerParams(dimension_semantics=("arbitrary", "arbitrary"),
                                             vmem_limit_bytes=VMEM_LIMIT),
        name="ssd_fwd",
    )(xo, dtp, yb, d_skip_x)


def _na_bias_tables(rpb):
    variants = [(0, 0, 0), (2, 0, 0), (4, 0, 1), (6, 2, 2), (8, 2, 2)]
    kc = np.arange(GRID_W)[:, None]
    w = np.arange(GRID_W)[None, :]
    cs = np.clip(w - NA_COLS // 2, 0, GRID_W - NA_COLS)
    col_ok = (kc >= cs) & (kc < cs + NA_COLS)
    taps = np.arange(2 * NA_COLS - 1)[:, None, None]
    onehot = (((kc - w + NA_COLS - 1)[None] == taps) & col_ok[None]).astype(np.float32)
    base = jnp.einsum('hrk,kcw->hrcw', rpb.astype(f32) * LOG2E, onehot, precision=lax.Precision.HIGHEST)
    base = jnp.where(col_ok, base, NEG_BIG).reshape(HEAD_PAIRS, 2, 2 * NA_ROWS - 1, GRID_W, GRID_W)
    basep = jnp.pad(base, ((0, 0), (0, 0), (2, 2), (0, 0), (0, 0)), constant_values=NEG_BIG)
    n_blk = 2 * NA_ROWS + 2
    blk = jnp.concatenate([basep[:, 0, 1:1 + n_blk], basep[:, 0, 0:n_blk],
                           basep[:, 1, 1:1 + n_blk], basep[:, 1, 0:n_blk]], axis=-1)
    kr = np.arange(NA_WIN)[:, None] // GRID_W
    par = (np.arange(2 * LANES)[None, :] // GRID_W) % 2
    tabs = []
    for roff, st0, st1 in variants:
        st = np.where(par == 0, st0, st1)
        valid = (kr >= st) & (kr < st + NA_ROWS)
        j0 = NA_ROWS - roff
        rows = blk[:, j0:j0 + NA_WIN_ROWS].reshape(HEAD_PAIRS, NA_WIN, 2 * LANES)
        tabs.append(jnp.where(valid, rows, NEG_BIG))
    return jnp.stack(tabs)


def _na_kernel(q_ref, kp_ref, kc_ref, kn_ref, vp_ref, vc_ref, vn_ref, tab_ref, o_ref, kwin_ref, vwin_ref,
               *, grid_rows):
    step = pl.program_id(2)
    me = _even_lane_mask()

    kwin_ref[0:NA_HALO, :] = kp_ref[0, 0]
    kwin_ref[NA_HALO:NA_HALO + NA_QT, :] = kc_ref[0, 0]
    kwin_ref[NA_HALO + NA_QT:, :] = kn_ref[0, 0]
    ht = NA_HALO // LANES
    vwin_ref[0:ht] = vp_ref[0, 0]
    vwin_ref[ht:ht + NA_QT // LANES] = vc_ref[0, 0]
    vwin_ref[ht + NA_QT // LANES:] = vn_ref[0, 0]
    tok0 = step * NA_QT - NA_HALO

    def window(i):
        r = 2 * (step * NA_PAIRS_PER_STEP + i)
        u0 = jnp.clip(r - NA_ROWS // 2, 0, grid_rows - NA_WIN_ROWS)
        var = jnp.where(r < 4, r // 2, jnp.where(r >= grid_rows - 4, 3 + (r - (grid_rows - 4)) // 2, 2))
        return u0, var

    def run(win_rows):
        n_keys = win_rows * GRID_W
        half = n_keys // 2

        def scores(i):
            u0, var = window(i)
            q2 = q_ref[0, 0, i * LANES:(i + 1) * LANES, :]
            zero = jnp.zeros_like(q2)
            qm = jnp.concatenate([jnp.where(me, q2, zero), jnp.where(me, zero, q2)], axis=0)
            parts = []
            for hk in range(2):
                start = pl.multiple_of(u0 * GRID_W - tok0 + hk * half, BF16_ROWS)
                kk = kwin_ref[pl.ds(start, half), :]
                parts.append(_dot_nt(kk, qm) + tab_ref[var, 0, hk * half:(hk + 1) * half, :])
            return jnp.concatenate(parts, axis=0)

        def finish(i, parts, rl):
            oc = jnp.concatenate([parts[e] * rl[:, e * LANES:(e + 1) * LANES] for e in range(2)], axis=0)
            o_ref[0, 0, i * LANES:(i + 1) * LANES, :] = oc.T.astype(bf16)

        s = scores(0)
        pending = None
        for i in range(NA_PAIRS_PER_STEP):
            s_next = scores(i + 1) if i + 1 < NA_PAIRS_PER_STEP else None
            u0, _ = window(i)
            m = jnp.max(s, axis=0, keepdims=True)
            p = jnp.exp2(s - m)
            l = jnp.sum(p, axis=0, keepdims=True)
            j0 = lax.shift_right_logical(u0 * GRID_W - tok0, 7)
            vtw = jnp.concatenate([vwin_ref[j0 + j] for j in range(NA_WIN // LANES)], axis=1)
            pb = p.astype(bf16)
            if n_keys < NA_WIN:
                pb = jnp.concatenate([pb, jnp.zeros((NA_WIN - n_keys, 2 * LANES), bf16)], axis=0)
            parts = [_dot(vtw[e * HEAD_DIM:(e + 1) * HEAD_DIM, :], pb[:, e * LANES:(e + 1) * LANES])
                     for e in range(2)]
            if pending is not None:
                finish(*pending)
            pending = (i, parts, 1.0 / l)
            s = s_next
        finish(*pending)

    last = pl.num_programs(2) - 1
    pl.when(step != last)(functools.partial(run, NA_WIN_ROWS - 1))
    pl.when(step == last)(functools.partial(run, NA_WIN_ROWS))


def _na(q, k, vt, tab):
    B, _, L, _ = q.shape
    grid_rows = L // GRID_W
    QT, H = NA_QT, NA_HALO
    r = QT // H
    nh = L // H
    nv = tab.shape[0]
    cur = lambda p, b, s: (b, p, s, 0)
    prev = lambda p, b, s: (b, p, jnp.maximum(s * r - 1, 0), 0)
    nxt = lambda p, b, s: (b, p, jnp.minimum((s + 1) * r, nh - 1), 0)
    tile = lambda f: (lambda p, b, s: f(p, b, s) + (0,))
    ht = H // LANES
    return pl.pallas_call(
        functools.partial(_na_kernel, grid_rows=grid_rows),
        grid=(HEAD_PAIRS, B, L // QT),
        in_specs=[pl.BlockSpec((1, 1, QT, LANES), cur),
                  pl.BlockSpec((1, 1, H, LANES), prev),
                  pl.BlockSpec((1, 1, QT, LANES), cur),
                  pl.BlockSpec((1, 1, H, LANES), nxt),
                  pl.BlockSpec((1, 1, ht, LANES, LANES), tile(prev)),
                  pl.BlockSpec((1, 1, QT // LANES, LANES, LANES), tile(cur)),
                  pl.BlockSpec((1, 1, ht, LANES, LANES), tile(nxt)),
                  pl.BlockSpec((nv, 1, NA_WIN, 2 * LANES), lambda p, b, s: (0, p, 0, 0))],
        out_specs=pl.BlockSpec((1, 1, QT, LANES), cur),
        out_shape=jax.ShapeDtypeStruct((B, HEAD_PAIRS, L, LANES), bf16),
        scratch_shapes=[pltpu.VMEM((QT + 2 * H, LANES), bf16),
                        pltpu.VMEM((QT // LANES + 2 * ht, LANES, LANES), bf16)],
        compiler_params=pltpu.CompilerParams(dimension_semantics=("arbitrary", "arbitrary", "arbitrary"),
                                             vmem_limit_bytes=VMEM_LIMIT),
        name="na",
    )(q, k, k, k, vt, vt, vt, tab)


def _out_kernel(x_ref, ys_ref, z_ref, yn_ref, mod_ref, gs_ref, wo_ref, gf_ref, wg_ref, wu_ref, wd_ref, o_ref, a_ref):
    half = TOKEN_TILE // TOKEN_SPLIT

    def gate_norm(rows):
        gated = ys_ref[0, rows, :].astype(f32) * _silu(z_ref[0, rows, :].astype(f32))
        gms = jnp.mean(gated * gated, axis=-1, keepdims=True)
        return (gated * lax.rsqrt(gms + EPS) * gs_ref[...]).astype(bf16)

    def out_proj(rows, y_ssd):
        y_na = jnp.concatenate([yn_ref[0, p, rows, :] for p in range(HEAD_PAIRS)], axis=1)
        mix = _dot(y_ssd, wo_ref[0:SSD_INNER, :]) + _dot(y_na, wo_ref[SSD_INNER:, :])
        return x_ref[0, rows, :] + mod_ref[0, 2:3, :] * mix

    def ffn_norm(x1):
        ms = jnp.mean(x1 * x1, axis=-1, keepdims=True)
        h2 = x1 * lax.rsqrt(ms + EPS) * gf_ref[...]
        return (h2 * (1.0 + mod_ref[0, 4:5, :]) + mod_ref[0, 3:4, :]).astype(bf16)

    def ffn_block(rows, h2, c):
        cols = slice(c * FF_BLOCK, (c + 1) * FF_BLOCK)
        gate = _dot(h2, wg_ref[:, cols])
        up = _dot(h2, wu_ref[:, cols])
        a_ref[rows, cols] = (_silu(gate) * up).astype(bf16)

    def ffn_down(rows, x1):
        ff = _dot(a_ref[rows, :], wd_ref[...])
        o_ref[0, rows, :] = x1 + mod_ref[0, 5:6, :] * ff

    for s in range(TOKEN_SPLIT):
        rows = slice(s * half, (s + 1) * half)
        x1 = out_proj(rows, gate_norm(rows))
        h2 = ffn_norm(x1)
        for c in range(D_FF // FF_BLOCK):
            ffn_block(rows, h2, c)
        ffn_down(rows, x1)


def _out_ffn(x, y_pre, z, y_na, mod, g_ssd, w_out, g_ffn, w_gate, w_up, w_down):
    B, L, _ = x.shape
    T = TOKEN_TILE
    tok = lambda b, i: (b, i, 0)
    const = lambda b, i: (0, 0)
    single = pl.Buffered(1)
    return pl.pallas_call(
        _out_kernel,
        grid=(B, L // T),
        in_specs=[pl.BlockSpec((1, T, D_MODEL), tok),
                  pl.BlockSpec((1, T, SSD_INNER), tok),
                  pl.BlockSpec((1, T, SSD_INNER), tok),
                  pl.BlockSpec((1, HEAD_PAIRS, T, LANES), lambda b, i: (b, 0, i, 0)),
                  pl.BlockSpec((1, 6, D_MODEL), lambda b, i: (b, 0, 0)),
                  pl.BlockSpec((1, SSD_INNER), const),
                  pl.BlockSpec((D_MODEL, D_MODEL), const, pipeline_mode=single),
                  pl.BlockSpec((1, D_MODEL), const),
                  pl.BlockSpec((D_MODEL, D_FF), const, pipeline_mode=single),
                  pl.BlockSpec((D_MODEL, D_FF), const, pipeline_mode=single),
                  pl.BlockSpec((D_FF, D_MODEL), const, pipeline_mode=single)],
        out_specs=pl.BlockSpec((1, T, D_MODEL), tok),
        out_shape=jax.ShapeDtypeStruct((B, L, D_MODEL), f32),
        scratch_shapes=[pltpu.VMEM((T, D_FF), bf16)],
        compiler_params=pltpu.CompilerParams(dimension_semantics=("arbitrary", "arbitrary"),
                                             vmem_limit_bytes=VMEM_LIMIT),
        name="out_ffn",
    )(x, y_pre, z, y_na, mod, g_ssd, w_out, g_ffn, w_gate, w_up, w_down)


def _pad_lanes(parts):
    pieces, pos = [], 0
    for off, v in parts:
        pieces += [jnp.zeros((off - pos,), f32), v.astype(f32)]
        pos = off + v.shape[0]
    pieces.append(jnp.zeros((LANES - pos,), f32))
    return jnp.concatenate(pieces).reshape(1, LANES)


def kernel(x_prompt, x_sample, c_prompt, c_sample, w_ada, b_ada, g_mix, w_in, conv_w, conv_b, dt_bias_fwd, dt_bias_bwd, a_log_fwd, a_log_bwd, d_skip, g_ssd, g_q, g_k, rpb, w_out, g_ffn, w_gate, w_up, w_down):
    l = 0
    nb_p = x_prompt.shape[0]
    nb_s = x_sample.shape[0]

    c_all = jnp.concatenate([c_prompt, c_sample, jnp.zeros((8 - nb_p - nb_s, D_MODEL), f32)], axis=0)
    mod = _modulation(c_all.T, w_ada[l], b_ada[l], nb_p + nb_s).reshape(8, 6, D_MODEL)

    o1 = SSD_INNER
    o2 = o1 + CONV_CH
    o3 = o2 + 2 * SSD_HEADS
    o4 = o3 + NA_INNER
    o5 = o4 + NA_INNER
    wi = w_in[l]
    w_main = jnp.concatenate([wi[:, :o2], wi[:, o3:o5]], axis=1).astype(bf16)
    w_vt = jnp.concatenate([wi[:, o5:], wi[:, o2:o3], wi[:, o2:o3]], axis=1).T.astype(bf16)
    dtb = _pad_lanes([(0, dt_bias_fwd[l]), (8, dt_bias_bwd[l]), (64, dt_bias_fwd[l]), (72, dt_bias_bwd[l])])
    alog = _pad_lanes([(0, a_log_fwd[l]), (8, a_log_bwd[l])])
    gq = jnp.tile(g_q[l].astype(f32) * (HEAD_DIM ** -0.5 * LOG2E), NA_HEADS).reshape(1, NA_INNER)
    gk = jnp.tile(g_k[l].astype(f32), NA_HEADS).reshape(1, NA_INNER)
    dsk = jnp.repeat(d_skip[l].astype(f32), HEAD_DIM).reshape(1, SSD_INNER)
    tab = _na_bias_tables(rpb[l])
    wo = w_out[l].astype(bf16)
    wg = w_gate[l].astype(bf16)
    wu = w_up[l].astype(bf16)
    wd = w_down[l].astype(bf16)
    row = lambda v: v.astype(f32).reshape(1, -1)

    def trunk(x, m):
        z, xbc, dtp, q, k, vt = _in_proj(x, m, row(g_mix[l]), w_main, w_vt, gq, gk, dtb, alog)
        xo, yb = _ssd_bwd(xbc, dtp, conv_w[l].astype(f32), row(conv_b[l]))
        y_pre = _ssd_fwd(xo, dtp, yb, dsk)
        y_na = _na(q, k, vt, tab)
        return _out_ffn(x, y_pre, z, y_na, m, row(g_ssd[l]), wo, row(g_ffn[l]), wg, wu, wd)

    return (trunk(x_prompt, mod[:nb_p]), trunk(x_sample, mod[nb_p:nb_p + nb_s]))
```

```python
import functools
import math

import numpy as np
import jax
import jax.numpy as jnp
from jax import lax
from jax.experimental import pallas as pl
from jax.experimental.pallas import tpu as pltpu

D_MODEL = 1024
HEAD_DIM = 64
SSD_HEADS = 8
SSD_INNER = SSD_HEADS * HEAD_DIM
SSD_GROUPS = 2
SSD_STATE = 128
CONV_K = 5
CHUNK = 128
NA_HEADS = 8
NA_INNER = NA_HEADS * HEAD_DIM
NA_ROWS = 8
NA_COLS = 16
GRID_W = 64
CONV_CH = SSD_INNER + 2 * SSD_GROUPS * SSD_STATE
D_FF = 2816
EPS = 1e-6

LANES = 128
BF16_ROWS = 16
CONV_HALO = BF16_ROWS
CONV_ROWS = 128
CONV_COLS = 256
GROUP_W = (SSD_HEADS // SSD_GROUPS) * HEAD_DIM
HEAD_PAIRS = NA_HEADS // 2
LOG2E = math.log2(math.e)
NEG_BIG = -1e30

TOKEN_TILE = 1024
TOKEN_SPLIT = 1
SSD_TILE = 1024
NA_WIN_ROWS = 10
NA_WIN = NA_WIN_ROWS * GRID_W
NA_PAIRS_PER_STEP = 16
NA_QT = NA_PAIRS_PER_STEP * 2 * GRID_W
NA_HALO = 256
FF_BLOCK = 256
VMEM_LIMIT = 56 * 1024 * 1024

f32 = jnp.float32
bf16 = jnp.bfloat16


def _dot(a, b):
    return jnp.dot(a, b, preferred_element_type=f32)


def _dot_nt(a, b):
    return lax.dot_general(a, b, (((1,), (1,)), ((), ())), preferred_element_type=f32)


def _split3(v):
    p1 = v.astype(bf16)
    r1 = v - p1.astype(f32)
    p2 = r1.astype(bf16)
    p3 = (r1 - p2.astype(f32)).astype(bf16)
    return p1, p2, p3


def _even_lane_mask():
    return lax.broadcasted_iota(jnp.int32, (1, LANES), 1) < HEAD_DIM


def _silu(v):
    return v * jax.nn.sigmoid(v)


def _mod_kernel(ct_ref, w_ref, b_ref, o_ref, *, n_rows):
    st = _silu(ct_ref[...])
    w = w_ref[...]
    rows = [jnp.sum(w * st[:, b:b + 1], axis=0, keepdims=True) for b in range(n_rows)]
    rows.append(jnp.zeros((8 - n_rows, w.shape[1]), f32))
    o_ref[...] = jnp.concatenate(rows, axis=0) + b_ref[...]


def _modulation(c_t, w_ada, b_ada, n_rows):
    n = 6 * D_MODEL
    return pl.pallas_call(
        functools.partial(_mod_kernel, n_rows=n_rows),
        grid=(n // D_MODEL,),
        in_specs=[pl.BlockSpec((D_MODEL, 8), lambda j: (0, 0)),
                  pl.BlockSpec((D_MODEL, D_MODEL), lambda j: (0, j)),
                  pl.BlockSpec((1, D_MODEL), lambda j: (0, j))],
        out_specs=pl.BlockSpec((8, D_MODEL), lambda j: (0, j)),
        out_shape=jax.ShapeDtypeStruct((8, n), f32),
        name="mod",
    )(c_t, w_ada, b_ada.reshape(1, n))


def _head_rmsnorm(v, gain):
    me = _even_lane_mask()
    s = v * v
    se = jnp.sum(jnp.where(me, s, 0.0), axis=-1, keepdims=True)
    so = jnp.sum(jnp.where(me, 0.0, s), axis=-1, keepdims=True)
    ms = jnp.where(me, se, so) * (1.0 / HEAD_DIM)
    return v * lax.rsqrt(ms + EPS) * gain


def _in_kernel(x_ref, mod_ref, g_ref, w_ref, wvt_ref, gq_ref, gk_ref, dtb_ref, alog_ref,
               z_ref, xbc_ref, dtp_ref, q_ref, k_ref, vt_ref):
    half = TOKEN_TILE // TOKEN_SPLIT

    def norm_mod(rows):
        x = x_ref[0, rows, :]
        ms = jnp.mean(x * x, axis=-1, keepdims=True)
        y = x * lax.rsqrt(ms + EPS) * g_ref[...]
        return (y * (1.0 + mod_ref[0, 1:2, :]) + mod_ref[0, 0:1, :]).astype(bf16)

    def proj_xbc(rows, h):
        xbc_ref[0, rows, :] = _dot(h, w_ref[:, SSD_INNER:SSD_INNER + CONV_CH]).astype(bf16)

    def proj_z(rows, h):
        z_ref[0, rows, :] = _dot(h, w_ref[:, 0:SSD_INNER]).astype(bf16)

    def proj_head_normed(rows, h, dst_ref, gain_ref, col0):
        full = _dot(h, w_ref[:, col0:col0 + NA_INNER])
        for j in range(HEAD_PAIRS):
            sl = slice(j * LANES, (j + 1) * LANES)
            dst_ref[0, j, rows, :] = _head_rmsnorm(full[:, sl], gain_ref[:, sl]).astype(bf16)

    def proj_v_dt(rows, h):
        full = _dot_nt(wvt_ref[...], h)
        vt = full[0:NA_INNER].astype(bf16)
        tb0 = rows.start // LANES
        for p in range(HEAD_PAIRS):
            for tb in range(half // LANES):
                vt_ref[0, p, tb0 + tb] = vt[p * LANES:(p + 1) * LANES, tb * LANES:(tb + 1) * LANES]
        lane = lax.broadcasted_iota(jnp.int32, (1, LANES), 1)
        a_neg = -jnp.exp(alog_ref[...])
        mul = jnp.where(lane < 2 * SSD_HEADS, a_neg,
                        jnp.where((lane >= 64) & (lane < 64 + 2 * SSD_HEADS), 1.0, 0.0))
        n_dt = 2 * SSD_HEADS
        gap = jnp.zeros((LANES // 2 - n_dt, LANES), f32)
        for tb in range(half // LANES):
            cols = slice(tb * LANES, (tb + 1) * LANES)
            raw_t = jnp.concatenate([full[NA_INNER:NA_INNER + n_dt, cols], gap,
                                     full[NA_INNER + n_dt:NA_INNER + 2 * n_dt, cols], gap], axis=0)
            raw = raw_t.T + dtb_ref[...]
            dt = jnp.maximum(raw, 0.0) + jnp.log1p(jnp.exp(-jnp.abs(raw)))
            dtp_ref[0, rows.start + tb * LANES:rows.start + (tb + 1) * LANES, :] = dt * mul

    o = SSD_INNER + CONV_CH
    for s in range(TOKEN_SPLIT):
        rows = slice(s * half, (s + 1) * half)
        h = norm_mod(rows)
        proj_xbc(rows, h)
        proj_z(rows, h)
        proj_head_normed(rows, h, q_ref, gq_ref, o)
        proj_head_normed(rows, h, k_ref, gk_ref, o + NA_INNER)
        proj_v_dt(rows, h)


def _in_proj(x, mod, g_mix, w_main, w_vt, gq, gk, dt_bias, a_log):
    B, L, _ = x.shape
    T = TOKEN_TILE
    nw = w_main.shape[1]
    const = lambda b, i: (0, 0)
    tok = lambda b, i: (b, i, 0)
    return pl.pallas_call(
        _in_kernel,
        grid=(B, L // T),
        in_specs=[pl.BlockSpec((1, T, D_MODEL), tok),
                  pl.BlockSpec((1, 6, D_MODEL), lambda b, i: (b, 0, 0)),
                  pl.BlockSpec((1, D_MODEL), const),
                  pl.BlockSpec((D_MODEL, nw), const),
                  pl.BlockSpec((NA_INNER + 4 * SSD_HEADS, D_MODEL), const),
                  pl.BlockSpec((1, NA_INNER), const),
                  pl.BlockSpec((1, NA_INNER), const),
                  pl.BlockSpec((1, LANES), const),
                  pl.BlockSpec((1, LANES), const)],
        out_specs=[pl.BlockSpec((1, T, SSD_INNER), tok),
                   pl.BlockSpec((1, T, CONV_CH), tok),
                   pl.BlockSpec((1, T, LANES), tok),
                   pl.BlockSpec((1, HEAD_PAIRS, T, LANES), lambda b, i: (b, 0, i, 0)),
                   pl.BlockSpec((1, HEAD_PAIRS, T, LANES), lambda b, i: (b, 0, i, 0)),
                   pl.BlockSpec((1, HEAD_PAIRS, T // LANES, LANES, LANES), lambda b, i: (b, 0, i, 0, 0))],
        out_shape=[jax.ShapeDtypeStruct((B, L, SSD_INNER), bf16),
                   jax.ShapeDtypeStruct((B, L, CONV_CH), bf16),
                   jax.ShapeDtypeStruct((B, L, LANES), f32),
                   jax.ShapeDtypeStruct((B, HEAD_PAIRS, L, LANES), bf16),
                   jax.ShapeDtypeStruct((B, HEAD_PAIRS, L, LANES), bf16),
                   jax.ShapeDtypeStruct((B, HEAD_PAIRS, L // LANES, LANES, LANES), bf16)],
        compiler_params=pltpu.CompilerParams(dimension_semantics=("arbitrary", "arbitrary"),
                                             vmem_limit_bytes=VMEM_LIMIT),
        name="in_proj",
    )(x, mod, g_mix, w_main, w_vt, gq, gk, dt_bias, a_log)


def _head_selector(base):
    r = lax.broadcasted_iota(jnp.int32, (LANES, SSD_INNER), 0)
    c = lax.broadcasted_iota(jnp.int32, (LANES, SSD_INNER), 1)
    return (lax.shift_right_logical(c, 6) == r - base).astype(bf16)


def _cumsum_matrix():
    row = lax.broadcasted_iota(jnp.int32, (CHUNK, CHUNK), 0)
    col = lax.broadcasted_iota(jnp.int32, (CHUNK, CHUNK), 1)
    return (col <= row).astype(bf16)


def _chunk_decays(p, tri):
    p1, p2, p3 = _split3(p)
    parts = _dot(tri, jnp.concatenate([p1, p2, p3], axis=1))
    incl = parts[:, 0:LANES] + parts[:, LANES:2 * LANES] + parts[:, 2 * LANES:3 * LANES]
    excl = incl - p
    dts = pltpu.roll(p, 64, 1)
    tot = incl[CHUNK - 1:CHUNK, :]
    is_fwd = lax.broadcasted_iota(jnp.int32, (1, LANES), 1) < SSD_HEADS
    w = jnp.exp(jnp.where(is_fwd, tot - incl, excl)) * dts
    sc = jnp.exp(jnp.where(is_fwd, incl, tot - excl))
    return incl, excl, dts, w, sc, jnp.exp(tot)


def _b_transposed(xo_ref, rows, g):
    b_g = xo_ref[0, rows, SSD_INNER + g * SSD_STATE:SSD_INNER + (g + 1) * SSD_STATE].astype(f32)
    return b_g.T.astype(bf16)


def _expand_decays(w, sc, dec, sel):
    v = jnp.concatenate([w, sc, jnp.broadcast_to(dec, (BF16_ROWS, LANES))], axis=0)
    n = v.shape[0]
    hi = v.astype(bf16)
    lo = (v - hi.astype(f32)).astype(bf16)
    both = _dot(jnp.concatenate([hi, lo], axis=0), sel)
    x = both[0:n] + both[n:2 * n]
    return x[0:CHUNK], x[CHUNK:2 * CHUNK], x[2 * CHUNK:2 * CHUNK + 1]


def _state_step(xo_ref, rows, st_ref, expanded, b_t):
    w_x, sc_x, dec_x = expanded
    outs = []
    for g in range(SSD_GROUPS):
        gcols = slice(g * GROUP_W, (g + 1) * GROUP_W)
        xs_g = xo_ref[0, rows, g * GROUP_W:(g + 1) * GROUP_W].astype(f32)
        c_g = xo_ref[0, rows, SSD_INNER + (SSD_GROUPS + g) * SSD_STATE:
                     SSD_INNER + (SSD_GROUPS + g + 1) * SSD_STATE]
        s_g = st_ref[g]
        y_off = _dot(c_g, s_g.astype(bf16)) * sc_x[:, gcols]
        xw = (xs_g * w_x[:, gcols]).astype(bf16)
        st_ref[g] = s_g * dec_x[:, gcols] + _dot(b_t[g], xw)
        outs.append(y_off)
    return jnp.concatenate(outs, axis=1)


def _ssd_bwd_kernel(xc_ref, xp_ref, xn_ref, dtp_ref, cw_ref, cb_ref, xo_ref, yb_ref, win_ref, st_ref, *, n_tiles):
    i = pl.program_id(1)
    t = n_tiles - 1 - i

    @pl.when(i == 0)
    def _():
        st_ref[...] = jnp.zeros_like(st_ref)

    H = CONV_HALO
    win_ref[0:H, :] = jnp.where(t > 0, xp_ref[0], jnp.zeros_like(xp_ref[0]))
    win_ref[H:H + SSD_TILE, :] = xc_ref[0]
    win_ref[H + SSD_TILE:, :] = jnp.where(t < n_tiles - 1, xn_ref[0], jnp.zeros_like(xn_ref[0]))
    r_i = lax.broadcasted_iota(jnp.int32, (CONV_ROWS, CONV_ROWS + 2 * H), 0)
    c_i = lax.broadcasted_iota(jnp.int32, (CONV_ROWS, CONV_ROWS + 2 * H), 1)
    mid = CONV_K // 2
    shift = {k: (c_i == r_i + H + k - mid).astype(bf16) for k in range(CONV_K) if k != mid}
    for c in range(SSD_TILE // CONV_ROWS):
        for cb in range(CONV_CH // CONV_COLS):
            cols = slice(cb * CONV_COLS, (cb + 1) * CONV_COLS)
            xw = win_ref[c * CONV_ROWS:(c + 1) * CONV_ROWS + 2 * H, cols]
            centre = win_ref[H + c * CONV_ROWS:H + (c + 1) * CONV_ROWS, cols].astype(f32)
            acc = cb_ref[:, cols] + centre * cw_ref[mid:mid + 1, cols]
            for k in shift:
                acc = acc + _dot(shift[k], xw) * cw_ref[k:k + 1, cols]
            xo_ref[0, c * CONV_ROWS:(c + 1) * CONV_ROWS, cols] = _silu(acc).astype(bf16)

    sel = _head_selector(SSD_HEADS)
    tri = _cumsum_matrix()

    def body(j, carry):
        c = SSD_TILE // CHUNK - 1 - j
        rows = pl.ds(pl.multiple_of(c * CHUNK, CHUNK), CHUNK)
        _, _, _, w, sc, dec = _chunk_decays(dtp_ref[0, rows, :], tri)
        b_t = [_b_transposed(xo_ref, rows, g) for g in range(SSD_GROUPS)]
        yb_ref[0, rows, :] = _state_step(xo_ref, rows, st_ref, _expand_decays(w, sc, dec, sel), b_t).astype(bf16)
        return carry

    lax.fori_loop(0, SSD_TILE // CHUNK, body, 0, unroll=True)


def _ssd_bwd(xbc, dtp, conv_w, conv_b):
    B, L, _ = xbc.shape
    T = SSD_TILE
    nT = L // T
    hb = T // CONV_HALO
    cur = lambda b, i: (b, nT - 1 - i, 0)
    prev = lambda b, i: (b, jnp.maximum((nT - 1 - i) * hb - 1, 0), 0)
    nxt = lambda b, i: (b, jnp.minimum((nT - i) * hb, L // CONV_HALO - 1), 0)
    const = lambda b, i: (0, 0)
    return pl.pallas_call(
        functools.partial(_ssd_bwd_kernel, n_tiles=nT),
        grid=(B, nT),
        in_specs=[pl.BlockSpec((1, T, CONV_CH), cur),
                  pl.BlockSpec((1, CONV_HALO, CONV_CH), prev),
                  pl.BlockSpec((1, CONV_HALO, CONV_CH), nxt),
                  pl.BlockSpec((1, T, LANES), cur),
                  pl.BlockSpec((CONV_K, CONV_CH), const),
                  pl.BlockSpec((1, CONV_CH), const)],
        out_specs=[pl.BlockSpec((1, T, CONV_CH), cur),
                   pl.BlockSpec((1, T, SSD_INNER), cur)],
        out_shape=[jax.ShapeDtypeStruct((B, L, CONV_CH), bf16),
                   jax.ShapeDtypeStruct((B, L, SSD_INNER), bf16)],
        scratch_shapes=[pltpu.VMEM((T + 2 * CONV_HALO, CONV_CH), bf16),
                        pltpu.VMEM((SSD_GROUPS, SSD_STATE, GROUP_W), f32)],
        compiler_params=pltpu.CompilerParams(dimension_semantics=("arbitrary", "arbitrary"),
                                             vmem_limit_bytes=VMEM_LIMIT),
        name="ssd_bwd",
    )(xbc, xbc, xbc, dtp, conv_w, conv_b)


def _ssd_fwd_kernel(xo_ref, dtp_ref, yb_ref, dsk_ref, o_ref, st_ref):
    @pl.when(pl.program_id(1) == 0)
    def _():
        st_ref[...] = jnp.zeros_like(st_ref)

    row = lax.broadcasted_iota(jnp.int32, (CHUNK, CHUNK), 0)
    col = lax.broadcasted_iota(jnp.int32, (CHUNK, CHUNK), 1)
    low = col < row
    diag = col == row
    me = _even_lane_mask()
    sel = _head_selector(0)
    tri = _cumsum_matrix()

    def body(c, carry):
        rows = pl.ds(pl.multiple_of(c * CHUNK, CHUNK), CHUNK)
        incl, excl, dts, w, sc, dec = _chunk_decays(dtp_ref[0, rows, :], tri)
        b_t = [_b_transposed(xo_ref, rows, g) for g in range(SSD_GROUPS)]
        y = _state_step(xo_ref, rows, st_ref, _expand_decays(w, sc, dec, sel), b_t)

        incl_t = incl.T
        excl_t = excl.T
        dts_t = dts.T
        y_diag = []
        for g in range(SSD_GROUPS):
            c_g = xo_ref[0, rows, SSD_INNER + (SSD_GROUPS + g) * SSD_STATE:
                         SSD_INNER + (SSD_GROUPS + g + 1) * SSD_STATE]
            cb = _dot(c_g, b_t[g])
            for pr in range(2):
                xs_pair = xo_ref[0, rows, g * GROUP_W + pr * LANES:g * GROUP_W + (pr + 1) * LANES]
                wms = []
                for e in range(2):
                    hd = 4 * g + 2 * pr + e
                    af_c = jnp.broadcast_to(incl[:, hd:hd + 1], (CHUNK, CHUNK))
                    eb_c = jnp.broadcast_to(excl[:, SSD_HEADS + hd:SSD_HEADS + hd + 1], (CHUNK, CHUNK))
                    af_r = incl_t[hd:hd + 1, :]
                    eb_r = excl_t[SSD_HEADS + hd:SSD_HEADS + hd + 1, :]
                    df_r = dts_t[hd:hd + 1, :]
                    db_r = dts_t[SSD_HEADS + hd:SSD_HEADS + hd + 1, :]
                    arg = jnp.where(low, af_c - af_r, eb_r - eb_c)
                    coef = jnp.where(low, df_r, jnp.where(diag, df_r + db_r, db_r))
                    wms.append((cb * jnp.exp(arg) * coef).astype(bf16))
                both = _dot(jnp.concatenate(wms, axis=0), xs_pair)
                y_diag.append(jnp.where(me, both[0:CHUNK], both[CHUNK:2 * CHUNK]))
        xs = xo_ref[0, rows, 0:SSD_INNER].astype(f32)
        y = y + jnp.concatenate(y_diag, axis=1) + yb_ref[0, rows, :].astype(f32) + xs * dsk_ref[...]
        o_ref[0, rows, :] = y.astype(bf16)
        return carry

    lax.fori_loop(0, SSD_TILE // CHUNK, body, 0, unroll=True)


def _ssd_fwd(xo, dtp, yb, d_skip_x):
    B, L, _ = xo.shape
    T = SSD_TILE
    tok = lambda b, i: (b, i, 0)
    const = lambda b, i: (0, 0)
    return pl.pallas_call(
        _ssd_fwd_kernel,
        grid=(B, L // T),
        in_specs=[pl.BlockSpec((1, T, CONV_CH), tok),
                  pl.BlockSpec((1, T, LANES), tok),
                  pl.BlockSpec((1, T, SSD_INNER), tok),
                  pl.BlockSpec((1, SSD_INNER), const)],
        out_specs=pl.BlockSpec((1, T, SSD_INNER), tok),
        out_shape=jax.ShapeDtypeStruct((B, L, SSD_INNER), bf16),
        scratch_shapes=[pltpu.VMEM((SSD_GROUPS, SSD_STATE, GROUP_W), f32)],
        compiler_params=pltpu.CompilerParams(dimension_semantics=("arbitrary", "arbitrary"),
                                             vmem_limit_bytes=VMEM_LIMIT),
        name="ssd_fwd",
    )(xo, dtp, yb, d_skip_x)


def _na_bias_tables(rpb):
    variants = [(0, 0, 0), (2, 0, 0), (4, 0, 1), (6, 2, 2), (8, 2, 2)]
    kc = np.arange(GRID_W)[:, None]
    w = np.arange(GRID_W)[None, :]
    cs = np.clip(w - NA_COLS // 2, 0, GRID_W - NA_COLS)
    col_ok = (kc >= cs) & (kc < cs + NA_COLS)
    taps = np.arange(2 * NA_COLS - 1)[:, None, None]
    onehot = (((kc - w + NA_COLS - 1)[None] == taps) & col_ok[None]).astype(np.float32)
    base = jnp.einsum('hrk,kcw->hrcw', rpb.astype(f32) * LOG2E, onehot, precision=lax.Precision.HIGHEST)
    base = jnp.where(col_ok, base, NEG_BIG).reshape(HEAD_PAIRS, 2, 2 * NA_ROWS - 1, GRID_W, GRID_W)
    basep = jnp.pad(base, ((0, 0), (0, 0), (2, 2), (0, 0), (0, 0)), constant_values=NEG_BIG)
    n_blk = 2 * NA_ROWS + 2
    blk = jnp.concatenate([basep[:, 0, 1:1 + n_blk], basep[:, 0, 0:n_blk],
                           basep[:, 1, 1:1 + n_blk], basep[:, 1, 0:n_blk]], axis=-1)
    kr = np.arange(NA_WIN)[:, None] // GRID_W
    par = (np.arange(2 * LANES)[None, :] // GRID_W) % 2
    tabs = []
    for roff, st0, st1 in variants:
        st = np.where(par == 0, st0, st1)
        valid = (kr >= st) & (kr < st + NA_ROWS)
        j0 = NA_ROWS - roff
        rows = blk[:, j0:j0 + NA_WIN_ROWS].reshape(HEAD_PAIRS, NA_WIN, 2 * LANES)
        tabs.append(jnp.where(valid, rows, NEG_BIG))
    return jnp.stack(tabs)


def _na_kernel(q_ref, kp_ref, kc_ref, kn_ref, vp_ref, vc_ref, vn_ref, tab_ref, o_ref, kwin_ref, vwin_ref,
               *, grid_rows):
    step = pl.program_id(2)
    me = _even_lane_mask()

    kwin_ref[0:NA_HALO, :] = kp_ref[0, 0]
    kwin_ref[NA_HALO:NA_HALO + NA_QT, :] = kc_ref[0, 0]
    kwin_ref[NA_HALO + NA_QT:, :] = kn_ref[0, 0]
    ht = NA_HALO // LANES
    vwin_ref[0:ht] = vp_ref[0, 0]
    vwin_ref[ht:ht + NA_QT // LANES] = vc_ref[0, 0]
    vwin_ref[ht + NA_QT // LANES:] = vn_ref[0, 0]
    tok0 = step * NA_QT - NA_HALO

    def window(i):
        r = 2 * (step * NA_PAIRS_PER_STEP + i)
        u0 = jnp.clip(r - NA_ROWS // 2, 0, grid_rows - NA_WIN_ROWS)
        var = jnp.where(r < 4, r // 2, jnp.where(r >= grid_rows - 4, 3 + (r - (grid_rows - 4)) // 2, 2))
        return u0, var

    def run(win_rows):
        n_keys = win_rows * GRID_W
        half = n_keys // 2

        def scores(i):
            u0, var = window(i)
            q2 = q_ref[0, 0, i * LANES:(i + 1) * LANES, :]
            zero = jnp.zeros_like(q2)
            qm = jnp.concatenate([jnp.where(me, q2, zero), jnp.where(me, zero, q2)], axis=0)
            parts = []
            for hk in range(2):
                start = pl.multiple_of(u0 * GRID_W - tok0 + hk * half, BF16_ROWS)
                kk = kwin_ref[pl.ds(start, half), :]
                parts.append(_dot_nt(kk, qm) + tab_ref[var, 0, hk * half:(hk + 1) * half, :])
            return jnp.concatenate(parts, axis=0)

        def finish(i, parts, rl):
            oc = jnp.concatenate([parts[e] * rl[:, e * LANES:(e + 1) * LANES] for e in range(2)], axis=0)
            o_ref[0, 0, i * LANES:(i + 1) * LANES, :] = oc.T.astype(bf16)

        s = scores(0)
        pending = None
        for i in range(NA_PAIRS_PER_STEP):
            s_next = scores(i + 1) if i + 1 < NA_PAIRS_PER_STEP else None
            u0, _ = window(i)
            m = jnp.max(s, axis=0, keepdims=True)
            p = jnp.exp2(s - m)
            l = jnp.sum(p, axis=0, keepdims=True)
            j0 = lax.shift_right_logical(u0 * GRID_W - tok0, 7)
            vtw = jnp.concatenate([vwin_ref[j0 + j] for j in range(NA_WIN // LANES)], axis=1)
            pb = p.astype(bf16)
            if n_keys < NA_WIN:
                pb = jnp.concatenate([pb, jnp.zeros((NA_WIN - n_keys, 2 * LANES), bf16)], axis=0)
            both = _dot(vtw, pb)
            parts = [both[e * HEAD_DIM:(e + 1) * HEAD_DIM, e * LANES:(e + 1) * LANES] for e in range(2)]
            if pending is not None:
                finish(*pending)
            pending = (i, parts, 1.0 / l)
            s = s_next
        finish(*pending)

    last = pl.num_programs(2) - 1
    pl.when(step != last)(functools.partial(run, NA_WIN_ROWS - 1))
    pl.when(step == last)(functools.partial(run, NA_WIN_ROWS))


def _na(q, k, vt, tab):
    B, _, L, _ = q.shape
    grid_rows = L // GRID_W
    QT, H = NA_QT, NA_HALO
    r = QT // H
    nh = L // H
    nv = tab.shape[0]
    cur = lambda p, b, s: (b, p, s, 0)
    prev = lambda p, b, s: (b, p, jnp.maximum(s * r - 1, 0), 0)
    nxt = lambda p, b, s: (b, p, jnp.minimum((s + 1) * r, nh - 1), 0)
    tile = lambda f: (lambda p, b, s: f(p, b, s) + (0,))
    ht = H // LANES
    return pl.pallas_call(
        functools.partial(_na_kernel, grid_rows=grid_rows),
        grid=(HEAD_PAIRS, B, L // QT),
        in_specs=[pl.BlockSpec((1, 1, QT, LANES), cur),
                  pl.BlockSpec((1, 1, H, LANES), prev),
                  pl.BlockSpec((1, 1, QT, LANES), cur),
                  pl.BlockSpec((1, 1, H, LANES), nxt),
                  pl.BlockSpec((1, 1, ht, LANES, LANES), tile(prev)),
                  pl.BlockSpec((1, 1, QT // LANES, LANES, LANES), tile(cur)),
                  pl.BlockSpec((1, 1, ht, LANES, LANES), tile(nxt)),
                  pl.BlockSpec((nv, 1, NA_WIN, 2 * LANES), lambda p, b, s: (0, p, 0, 0))],
        out_specs=pl.BlockSpec((1, 1, QT, LANES), cur),
        out_shape=jax.ShapeDtypeStruct((B, HEAD_PAIRS, L, LANES), bf16),
        scratch_shapes=[pltpu.VMEM((QT + 2 * H, LANES), bf16),
                        pltpu.VMEM((QT // LANES + 2 * ht, LANES, LANES), bf16)],
        compiler_params=pltpu.CompilerParams(dimension_semantics=("arbitrary", "arbitrary", "arbitrary"),
                                             vmem_limit_bytes=VMEM_LIMIT),
        name="na",
    )(q, k, k, k, vt, vt, vt, tab)


def _out_kernel(x_ref, ys_ref, z_ref, yn_ref, mod_ref, gs_ref, wo_ref, gf_ref, wg_ref, wu_ref, wd_ref, o_ref, a_ref):
    half = TOKEN_TILE // TOKEN_SPLIT

    def gate_norm(rows):
        gated = ys_ref[0, rows, :].astype(f32) * _silu(z_ref[0, rows, :].astype(f32))
        gms = jnp.mean(gated * gated, axis=-1, keepdims=True)
        return (gated * lax.rsqrt(gms + EPS) * gs_ref[...]).astype(bf16)

    def out_proj(rows, y_ssd):
        y_na = jnp.concatenate([yn_ref[0, p, rows, :] for p in range(HEAD_PAIRS)], axis=1)
        mix = _dot(y_ssd, wo_ref[0:SSD_INNER, :]) + _dot(y_na, wo_ref[SSD_INNER:, :])
        return x_ref[0, rows, :] + mod_ref[0, 2:3, :] * mix

    def ffn_norm(x1):
        ms = jnp.mean(x1 * x1, axis=-1, keepdims=True)
        h2 = x1 * lax.rsqrt(ms + EPS) * gf_ref[...]
        return (h2 * (1.0 + mod_ref[0, 4:5, :]) + mod_ref[0, 3:4, :]).astype(bf16)

    def ffn_block(rows, h2, c):
        cols = slice(c * FF_BLOCK, (c + 1) * FF_BLOCK)
        gate = _dot(h2, wg_ref[:, cols])
        up = _dot(h2, wu_ref[:, cols])
        a_ref[rows, cols] = (_silu(gate) * up).astype(bf16)

    def ffn_down(rows, x1):
        ff = _dot(a_ref[rows, :], wd_ref[...])
        o_ref[0, rows, :] = x1 + mod_ref[0, 5:6, :] * ff

    for s in range(TOKEN_SPLIT):
        rows = slice(s * half, (s + 1) * half)
        x1 = out_proj(rows, gate_norm(rows))
        h2 = ffn_norm(x1)
        for c in range(D_FF // FF_BLOCK):
            ffn_block(rows, h2, c)
        ffn_down(rows, x1)


def _out_ffn(x, y_pre, z, y_na, mod, g_ssd, w_out, g_ffn, w_gate, w_up, w_down):
    B, L, _ = x.shape
    T = TOKEN_TILE
    tok = lambda b, i: (b, i, 0)
    const = lambda b, i: (0, 0)
    single = pl.Buffered(1)
    return pl.pallas_call(
        _out_kernel,
        grid=(B, L // T),
        in_specs=[pl.BlockSpec((1, T, D_MODEL), tok),
                  pl.BlockSpec((1, T, SSD_INNER), tok),
                  pl.BlockSpec((1, T, SSD_INNER), tok),
                  pl.BlockSpec((1, HEAD_PAIRS, T, LANES), lambda b, i: (b, 0, i, 0)),
                  pl.BlockSpec((1, 6, D_MODEL), lambda b, i: (b, 0, 0)),
                  pl.BlockSpec((1, SSD_INNER), const),
                  pl.BlockSpec((D_MODEL, D_MODEL), const, pipeline_mode=single),
                  pl.BlockSpec((1, D_MODEL), const),
                  pl.BlockSpec((D_MODEL, D_FF), const, pipeline_mode=single),
                  pl.BlockSpec((D_MODEL, D_FF), const, pipeline_mode=single),
                  pl.BlockSpec((D_FF, D_MODEL), const, pipeline_mode=single)],
        out_specs=pl.BlockSpec((1, T, D_MODEL), tok),
        out_shape=jax.ShapeDtypeStruct((B, L, D_MODEL), f32),
        scratch_shapes=[pltpu.VMEM((T, D_FF), bf16)],
        compiler_params=pltpu.CompilerParams(dimension_semantics=("arbitrary", "arbitrary"),
                                             vmem_limit_bytes=VMEM_LIMIT),
        name="out_ffn",
    )(x, y_pre, z, y_na, mod, g_ssd, w_out, g_ffn, w_gate, w_up, w_down)


def _pad_lanes(parts):
    pieces, pos = [], 0
    for off, v in parts:
        pieces += [jnp.zeros((off - pos,), f32), v.astype(f32)]
        pos = off + v.shape[0]
    pieces.append(jnp.zeros((LANES - pos,), f32))
    return jnp.concatenate(pieces).reshape(1, LANES)


def kernel(x_prompt, x_sample, c_prompt, c_sample, w_ada, b_ada, g_mix, w_in, conv_w, conv_b, dt_bias_fwd, dt_bias_bwd, a_log_fwd, a_log_bwd, d_skip, g_ssd, g_q, g_k, rpb, w_out, g_ffn, w_gate, w_up, w_down):
    l = 0
    nb_p = x_prompt.shape[0]
    nb_s = x_sample.shape[0]

    c_all = jnp.concatenate([c_prompt, c_sample, jnp.zeros((8 - nb_p - nb_s, D_MODEL), f32)], axis=0)
    mod = _modulation(c_all.T, w_ada[l], b_ada[l], nb_p + nb_s).reshape(8, 6, D_MODEL)

    o1 = SSD_INNER
    o2 = o1 + CONV_CH
    o3 = o2 + 2 * SSD_HEADS
    o4 = o3 + NA_INNER
    o5 = o4 + NA_INNER
    wi = w_in[l]
    w_main = jnp.concatenate([wi[:, :o2], wi[:, o3:o5]], axis=1).astype(bf16)
    w_vt = jnp.concatenate([wi[:, o5:], wi[:, o2:o3], wi[:, o2:o3]], axis=1).T.astype(bf16)
    dtb = _pad_lanes([(0, dt_bias_fwd[l]), (8, dt_bias_bwd[l]), (64, dt_bias_fwd[l]), (72, dt_bias_bwd[l])])
    alog = _pad_lanes([(0, a_log_fwd[l]), (8, a_log_bwd[l])])
    gq = jnp.tile(g_q[l].astype(f32) * (HEAD_DIM ** -0.5 * LOG2E), NA_HEADS).reshape(1, NA_INNER)
    gk = jnp.tile(g_k[l].astype(f32), NA_HEADS).reshape(1, NA_INNER)
    dsk = jnp.repeat(d_skip[l].astype(f32), HEAD_DIM).reshape(1, SSD_INNER)
    tab = _na_bias_tables(rpb[l])
    wo = w_out[l].astype(bf16)
    wg = w_gate[l].astype(bf16)
    wu = w_up[l].astype(bf16)
    wd = w_down[l].astype(bf16)
    row = lambda v: v.astype(f32).reshape(1, -1)

    def trunk(x, m):
        z, xbc, dtp, q, k, vt = _in_proj(x, m, row(g_mix[l]), w_main, w_vt, gq, gk, dtb, alog)
        xo, yb = _ssd_bwd(xbc, dtp, conv_w[l].astype(f32), row(conv_b[l]))
        y_pre = _ssd_fwd(xo, dtp, yb, dsk)
        y_na = _na(q, k, vt, tab)
        return _out_ffn(x, y_pre, z, y_na, m, row(g_ssd[l]), wo, row(g_ffn[l]), wg, wu, wd)

    return (trunk(x_prompt, mod[:nb_p]), trunk(x_sample, mod[nb_p:nb_p + nb_s]))
```

```python
import functools
import math

import numpy as np
import jax
import jax.numpy as jnp
from jax import lax
from jax.experimental import pallas as pl
from jax.experimental.pallas import tpu as pltpu

D_MODEL = 1024
HEAD_DIM = 64
SSD_HEADS = 8
SSD_INNER = SSD_HEADS * HEAD_DIM
SSD_GROUPS = 2
SSD_STATE = 128
CONV_K = 5
CHUNK = 128
NA_HEADS = 8
NA_INNER = NA_HEADS * HEAD_DIM
NA_ROWS = 8
NA_COLS = 16
GRID_W = 64
CONV_CH = SSD_INNER + 2 * SSD_GROUPS * SSD_STATE
D_FF = 2816
EPS = 1e-6

LANES = 128
BF16_ROWS = 16
CONV_HALO = BF16_ROWS
CONV_ROWS = 128
CONV_COLS = 256
GROUP_W = (SSD_HEADS // SSD_GROUPS) * HEAD_DIM
HEAD_PAIRS = NA_HEADS // 2
LOG2E = math.log2(math.e)
NEG_BIG = -1e30

TOKEN_TILE = 1024
SSD_TILE = 1024
NA_WIN_ROWS = 10
NA_WIN = NA_WIN_ROWS * GRID_W
NA_PAIRS_PER_STEP = 16
NA_QT = NA_PAIRS_PER_STEP * 2 * GRID_W
NA_HALO = 256
FF_BLOCK = 256
VMEM_LIMIT = 56 * 1024 * 1024

f32 = jnp.float32
bf16 = jnp.bfloat16


def _dot(a, b):
    return jnp.dot(a, b, preferred_element_type=f32)


def _dot_nt(a, b):
    return lax.dot_general(a, b, (((1,), (1,)), ((), ())), preferred_element_type=f32)


def _split3(v):
    p1 = v.astype(bf16)
    r1 = v - p1.astype(f32)
    p2 = r1.astype(bf16)
    p3 = (r1 - p2.astype(f32)).astype(bf16)
    return p1, p2, p3


def _even_lane_mask():
    return lax.broadcasted_iota(jnp.int32, (1, LANES), 1) < HEAD_DIM


def _silu(v):
    return v * jax.nn.sigmoid(v)


def _mod_kernel(ct_ref, w_ref, b_ref, o_ref, *, n_rows):
    st = _silu(ct_ref[...])
    w = w_ref[...]
    rows = [jnp.sum(w * st[:, b:b + 1], axis=0, keepdims=True) for b in range(n_rows)]
    rows.append(jnp.zeros((8 - n_rows, w.shape[1]), f32))
    o_ref[...] = jnp.concatenate(rows, axis=0) + b_ref[...]


def _modulation(c_t, w_ada, b_ada, n_rows):
    n = 6 * D_MODEL
    return pl.pallas_call(
        functools.partial(_mod_kernel, n_rows=n_rows),
        grid=(n // D_MODEL,),
        in_specs=[pl.BlockSpec((D_MODEL, 8), lambda j: (0, 0)),
                  pl.BlockSpec((D_MODEL, D_MODEL), lambda j: (0, j)),
                  pl.BlockSpec((1, D_MODEL), lambda j: (0, j))],
        out_specs=pl.BlockSpec((8, D_MODEL), lambda j: (0, j)),
        out_shape=jax.ShapeDtypeStruct((8, n), f32),
        name="mod",
    )(c_t, w_ada, b_ada.reshape(1, n))


def _head_rmsnorm(v, gain):
    me = _even_lane_mask()
    s = v * v
    se = jnp.sum(jnp.where(me, s, 0.0), axis=-1, keepdims=True)
    so = jnp.sum(jnp.where(me, 0.0, s), axis=-1, keepdims=True)
    ms = jnp.where(me, se, so) * (1.0 / HEAD_DIM)
    return v * lax.rsqrt(ms + EPS) * gain


def _in_kernel(x_ref, mod_ref, g_ref, w_ref, wvt_ref, gq_ref, gk_ref, dtb_ref, alog_ref,
               z_ref, xbc_ref, dtp_ref, q_ref, k_ref, vt_ref):
    x = x_ref[0]
    ms = jnp.mean(x * x, axis=-1, keepdims=True)
    y = x * lax.rsqrt(ms + EPS) * g_ref[...]
    h = (y * (1.0 + mod_ref[0, 1:2, :]) + mod_ref[0, 0:1, :]).astype(bf16)

    def proj_head_normed(dst_ref, gain_ref, col0):
        full = _dot(h, w_ref[:, col0:col0 + NA_INNER])
        for j in range(HEAD_PAIRS):
            sl = slice(j * LANES, (j + 1) * LANES)
            dst_ref[0, j] = _head_rmsnorm(full[:, sl], gain_ref[:, sl]).astype(bf16)

    def proj_v_dt():
        full = _dot_nt(wvt_ref[...], h)
        vt = full[0:NA_INNER].astype(bf16)
        for p in range(HEAD_PAIRS):
            for tb in range(TOKEN_TILE // LANES):
                vt_ref[0, p, tb] = vt[p * LANES:(p + 1) * LANES, tb * LANES:(tb + 1) * LANES]
        lane = lax.broadcasted_iota(jnp.int32, (1, LANES), 1)
        a_neg = -jnp.exp(alog_ref[...])
        mul = jnp.where(lane < 2 * SSD_HEADS, a_neg,
                        jnp.where((lane >= 64) & (lane < 64 + 2 * SSD_HEADS), 1.0, 0.0))
        n_dt = 2 * SSD_HEADS
        gap = jnp.zeros((LANES // 2 - n_dt, LANES), f32)
        for tb in range(TOKEN_TILE // LANES):
            cols = slice(tb * LANES, (tb + 1) * LANES)
            raw_t = jnp.concatenate([full[NA_INNER:NA_INNER + n_dt, cols], gap,
                                     full[NA_INNER + n_dt:NA_INNER + 2 * n_dt, cols], gap], axis=0)
            raw = raw_t.T + dtb_ref[...]
            dt = jnp.maximum(raw, 0.0) + jnp.log1p(jnp.exp(-jnp.abs(raw)))
            dtp_ref[0, cols, :] = dt * mul

    o = SSD_INNER + CONV_CH
    xbc_ref[0] = _dot(h, w_ref[:, SSD_INNER:o]).astype(bf16)
    z_ref[0] = _dot(h, w_ref[:, 0:SSD_INNER]).astype(bf16)
    proj_head_normed(q_ref, gq_ref, o)
    proj_head_normed(k_ref, gk_ref, o + NA_INNER)
    proj_v_dt()


def _in_proj(x, mod, g_mix, w_main, w_vt, gq, gk, dt_bias, a_log):
    B, L, _ = x.shape
    T = TOKEN_TILE
    nw = w_main.shape[1]
    const = lambda b, i: (0, 0)
    tok = lambda b, i: (b, i, 0)
    return pl.pallas_call(
        _in_kernel,
        grid=(B, L // T),
        in_specs=[pl.BlockSpec((1, T, D_MODEL), tok),
                  pl.BlockSpec((1, 6, D_MODEL), lambda b, i: (b, 0, 0)),
                  pl.BlockSpec((1, D_MODEL), const),
                  pl.BlockSpec((D_MODEL, nw), const),
                  pl.BlockSpec((NA_INNER + 4 * SSD_HEADS, D_MODEL), const),
                  pl.BlockSpec((1, NA_INNER), const),
                  pl.BlockSpec((1, NA_INNER), const),
                  pl.BlockSpec((1, LANES), const),
                  pl.BlockSpec((1, LANES), const)],
        out_specs=[pl.BlockSpec((1, T, SSD_INNER), tok),
                   pl.BlockSpec((1, T, CONV_CH), tok),
                   pl.BlockSpec((1, T, LANES), tok),
                   pl.BlockSpec((1, HEAD_PAIRS, T, LANES), lambda b, i: (b, 0, i, 0)),
                   pl.BlockSpec((1, HEAD_PAIRS, T, LANES), lambda b, i: (b, 0, i, 0)),
                   pl.BlockSpec((1, HEAD_PAIRS, T // LANES, LANES, LANES), lambda b, i: (b, 0, i, 0, 0))],
        out_shape=[jax.ShapeDtypeStruct((B, L, SSD_INNER), bf16),
                   jax.ShapeDtypeStruct((B, L, CONV_CH), bf16),
                   jax.ShapeDtypeStruct((B, L, LANES), f32),
                   jax.ShapeDtypeStruct((B, HEAD_PAIRS, L, LANES), bf16),
                   jax.ShapeDtypeStruct((B, HEAD_PAIRS, L, LANES), bf16),
                   jax.ShapeDtypeStruct((B, HEAD_PAIRS, L // LANES, LANES, LANES), bf16)],
        compiler_params=pltpu.CompilerParams(dimension_semantics=("arbitrary", "arbitrary"),
                                             vmem_limit_bytes=VMEM_LIMIT),
        name="in_proj",
    )(x, mod, g_mix, w_main, w_vt, gq, gk, dt_bias, a_log)


def _head_selector(base):
    r = lax.broadcasted_iota(jnp.int32, (LANES, SSD_INNER), 0)
    c = lax.broadcasted_iota(jnp.int32, (LANES, SSD_INNER), 1)
    return (lax.shift_right_logical(c, 6) == r - base).astype(bf16)


def _cumsum_matrix():
    row = lax.broadcasted_iota(jnp.int32, (CHUNK, CHUNK), 0)
    col = lax.broadcasted_iota(jnp.int32, (CHUNK, CHUNK), 1)
    return (col <= row).astype(bf16)


def _chunk_decays(p, tri):
    p1, p2, p3 = _split3(p)
    parts = _dot(tri, jnp.concatenate([p1, p2, p3], axis=1))
    incl = parts[:, 0:LANES] + parts[:, LANES:2 * LANES] + parts[:, 2 * LANES:3 * LANES]
    excl = incl - p
    dts = pltpu.roll(p, 64, 1)
    tot = incl[CHUNK - 1:CHUNK, :]
    is_fwd = lax.broadcasted_iota(jnp.int32, (1, LANES), 1) < SSD_HEADS
    w = jnp.exp(jnp.where(is_fwd, tot - incl, excl)) * dts
    sc = jnp.exp(jnp.where(is_fwd, incl, tot - excl))
    return incl, excl, dts, w, sc, jnp.exp(tot)


def _b_transposed(xo_ref, rows, g):
    b_g = xo_ref[0, rows, SSD_INNER + g * SSD_STATE:SSD_INNER + (g + 1) * SSD_STATE].astype(f32)
    return b_g.T.astype(bf16)


def _expand_decays(w, sc, dec, sel):
    v = jnp.concatenate([w, sc, jnp.broadcast_to(dec, (BF16_ROWS, LANES))], axis=0)
    n = v.shape[0]
    hi = v.astype(bf16)
    lo = (v - hi.astype(f32)).astype(bf16)
    both = _dot(jnp.concatenate([hi, lo], axis=0), sel)
    x = both[0:n] + both[n:2 * n]
    return x[0:CHUNK], x[CHUNK:2 * CHUNK], x[2 * CHUNK:2 * CHUNK + 1]


def _state_step(xo_ref, rows, st_ref, expanded, b_t):
    w_x, sc_x, dec_x = expanded
    outs = []
    for g in range(SSD_GROUPS):
        gcols = slice(g * GROUP_W, (g + 1) * GROUP_W)
        xs_g = xo_ref[0, rows, g * GROUP_W:(g + 1) * GROUP_W].astype(f32)
        c_g = xo_ref[0, rows, SSD_INNER + (SSD_GROUPS + g) * SSD_STATE:
                     SSD_INNER + (SSD_GROUPS + g + 1) * SSD_STATE]
        s_g = st_ref[g]
        y_off = _dot(c_g, s_g.astype(bf16)) * sc_x[:, gcols]
        xw = (xs_g * w_x[:, gcols]).astype(bf16)
        st_ref[g] = s_g * dec_x[:, gcols] + _dot(b_t[g], xw)
        outs.append(y_off)
    return jnp.concatenate(outs, axis=1)


def _ssd_bwd_kernel(xc_ref, xp_ref, xn_ref, dtp_ref, cw_ref, cb_ref, xo_ref, yb_ref, win_ref, st_ref, *, n_tiles):
    i = pl.program_id(1)
    t = n_tiles - 1 - i

    @pl.when(i == 0)
    def _():
        st_ref[...] = jnp.zeros_like(st_ref)

    H = CONV_HALO
    win_ref[0:H, :] = jnp.where(t > 0, xp_ref[0], jnp.zeros_like(xp_ref[0]))
    win_ref[H:H + SSD_TILE, :] = xc_ref[0]
    win_ref[H + SSD_TILE:, :] = jnp.where(t < n_tiles - 1, xn_ref[0], jnp.zeros_like(xn_ref[0]))
    r_i = lax.broadcasted_iota(jnp.int32, (CONV_ROWS, CONV_ROWS + 2 * H), 0)
    c_i = lax.broadcasted_iota(jnp.int32, (CONV_ROWS, CONV_ROWS + 2 * H), 1)
    mid = CONV_K // 2
    shift = {k: (c_i == r_i + H + k - mid).astype(bf16) for k in range(CONV_K) if k != mid}
    for c in range(SSD_TILE // CONV_ROWS):
        for cb in range(CONV_CH // CONV_COLS):
            cols = slice(cb * CONV_COLS, (cb + 1) * CONV_COLS)
            xw = win_ref[c * CONV_ROWS:(c + 1) * CONV_ROWS + 2 * H, cols]
            centre = win_ref[H + c * CONV_ROWS:H + (c + 1) * CONV_ROWS, cols].astype(f32)
            acc = cb_ref[:, cols] + centre * cw_ref[mid:mid + 1, cols]
            for k in shift:
                acc = acc + _dot(shift[k], xw) * cw_ref[k:k + 1, cols]
            xo_ref[0, c * CONV_ROWS:(c + 1) * CONV_ROWS, cols] = _silu(acc).astype(bf16)

    sel = _head_selector(SSD_HEADS)
    tri = _cumsum_matrix()

    def body(j, carry):
        c = SSD_TILE // CHUNK - 1 - j
        rows = pl.ds(pl.multiple_of(c * CHUNK, CHUNK), CHUNK)
        _, _, _, w, sc, dec = _chunk_decays(dtp_ref[0, rows, :], tri)
        b_t = [_b_transposed(xo_ref, rows, g) for g in range(SSD_GROUPS)]
        yb_ref[0, rows, :] = _state_step(xo_ref, rows, st_ref, _expand_decays(w, sc, dec, sel), b_t).astype(bf16)
        return carry

    lax.fori_loop(0, SSD_TILE // CHUNK, body, 0, unroll=True)


def _ssd_bwd(xbc, dtp, conv_w, conv_b):
    B, L, _ = xbc.shape
    T = SSD_TILE
    nT = L // T
    hb = T // CONV_HALO
    cur = lambda b, i: (b, nT - 1 - i, 0)
    prev = lambda b, i: (b, jnp.maximum((nT - 1 - i) * hb - 1, 0), 0)
    nxt = lambda b, i: (b, jnp.minimum((nT - i) * hb, L // CONV_HALO - 1), 0)
    const = lambda b, i: (0, 0)
    return pl.pallas_call(
        functools.partial(_ssd_bwd_kernel, n_tiles=nT),
        grid=(B, nT),
        in_specs=[pl.BlockSpec((1, T, CONV_CH), cur),
                  pl.BlockSpec((1, CONV_HALO, CONV_CH), prev),
                  pl.BlockSpec((1, CONV_HALO, CONV_CH), nxt),
                  pl.BlockSpec((1, T, LANES), cur),
                  pl.BlockSpec((CONV_K, CONV_CH), const),
                  pl.BlockSpec((1, CONV_CH), const)],
        out_specs=[pl.BlockSpec((1, T, CONV_CH), cur),
                   pl.BlockSpec((1, T, SSD_INNER), cur)],
        out_shape=[jax.ShapeDtypeStruct((B, L, CONV_CH), bf16),
                   jax.ShapeDtypeStruct((B, L, SSD_INNER), bf16)],
        scratch_shapes=[pltpu.VMEM((T + 2 * CONV_HALO, CONV_CH), bf16),
                        pltpu.VMEM((SSD_GROUPS, SSD_STATE, GROUP_W), f32)],
        compiler_params=pltpu.CompilerParams(dimension_semantics=("arbitrary", "arbitrary"),
                                             vmem_limit_bytes=VMEM_LIMIT),
        name="ssd_bwd",
    )(xbc, xbc, xbc, dtp, conv_w, conv_b)


def _ssd_fwd_kernel(xo_ref, dtp_ref, yb_ref, dsk_ref, o_ref, st_ref):
    @pl.when(pl.program_id(1) == 0)
    def _():
        st_ref[...] = jnp.zeros_like(st_ref)

    row = lax.broadcasted_iota(jnp.int32, (CHUNK, CHUNK), 0)
    col = lax.broadcasted_iota(jnp.int32, (CHUNK, CHUNK), 1)
    low = col < row
    diag = col == row
    me = _even_lane_mask()
    sel = _head_selector(0)
    tri = _cumsum_matrix()

    def body(c, carry):
        rows = pl.ds(pl.multiple_of(c * CHUNK, CHUNK), CHUNK)
        incl, excl, dts, w, sc, dec = _chunk_decays(dtp_ref[0, rows, :], tri)
        b_t = [_b_transposed(xo_ref, rows, g) for g in range(SSD_GROUPS)]
        y = _state_step(xo_ref, rows, st_ref, _expand_decays(w, sc, dec, sel), b_t)

        incl_t = incl.T
        excl_t = excl.T
        dts_t = dts.T
        y_diag = []
        for g in range(SSD_GROUPS):
            c_g = xo_ref[0, rows, SSD_INNER + (SSD_GROUPS + g) * SSD_STATE:
                         SSD_INNER + (SSD_GROUPS + g + 1) * SSD_STATE]
            cb = _dot(c_g, b_t[g])
            for pr in range(2):
                xs_pair = xo_ref[0, rows, g * GROUP_W + pr * LANES:g * GROUP_W + (pr + 1) * LANES]
                wms = []
                for e in range(2):
                    hd = 4 * g + 2 * pr + e
                    af_c = jnp.broadcast_to(incl[:, hd:hd + 1], (CHUNK, CHUNK))
                    eb_c = jnp.broadcast_to(excl[:, SSD_HEADS + hd:SSD_HEADS + hd + 1], (CHUNK, CHUNK))
                    af_r = incl_t[hd:hd + 1, :]
                    eb_r = excl_t[SSD_HEADS + hd:SSD_HEADS + hd + 1, :]
                    df_r = dts_t[hd:hd + 1, :]
                    db_r = dts_t[SSD_HEADS + hd:SSD_HEADS + hd + 1, :]
                    arg = jnp.where(low, af_c - af_r, eb_r - eb_c)
                    coef = jnp.where(low, df_r, jnp.where(diag, df_r + db_r, db_r))
                    wms.append((cb * jnp.exp(arg) * coef).astype(bf16))
                both = _dot(jnp.concatenate(wms, axis=0), xs_pair)
                y_diag.append(jnp.where(me, both[0:CHUNK], both[CHUNK:2 * CHUNK]))
        xs = xo_ref[0, rows, 0:SSD_INNER].astype(f32)
        y = y + jnp.concatenate(y_diag, axis=1) + yb_ref[0, rows, :].astype(f32) + xs * dsk_ref[...]
        o_ref[0, rows, :] = y.astype(bf16)
        return carry

    lax.fori_loop(0, SSD_TILE // CHUNK, body, 0, unroll=True)


def _ssd_fwd(xo, dtp, yb, d_skip_x):
    B, L, _ = xo.shape
    T = SSD_TILE
    tok = lambda b, i: (b, i, 0)
    const = lambda b, i: (0, 0)
    return pl.pallas_call(
        _ssd_fwd_kernel,
        grid=(B, L // T),
        in_specs=[pl.BlockSpec((1, T, CONV_CH), tok),
                  pl.BlockSpec((1, T, LANES), tok),
                  pl.BlockSpec((1, T, SSD_INNER), tok),
                  pl.BlockSpec((1, SSD_INNER), const)],
        out_specs=pl.BlockSpec((1, T, SSD_INNER), tok),
        out_shape=jax.ShapeDtypeStruct((B, L, SSD_INNER), bf16),
        scratch_shapes=[pltpu.VMEM((SSD_GROUPS, SSD_STATE, GROUP_W), f32)],
        compiler_params=pltpu.CompilerParams(dimension_semantics=("arbitrary", "arbitrary"),
                                             vmem_limit_bytes=VMEM_LIMIT),
        name="ssd_fwd",
    )(xo, dtp, yb, d_skip_x)


def _na_bias_tables(rpb):
    variants = [(0, 0, 0), (2, 0, 0), (4, 0, 1), (6, 2, 2), (8, 2, 2)]
    kc = np.arange(GRID_W)[:, None]
    w = np.arange(GRID_W)[None, :]
    cs = np.clip(w - NA_COLS // 2, 0, GRID_W - NA_COLS)
    col_ok = (kc >= cs) & (kc < cs + NA_COLS)
    taps = np.arange(2 * NA_COLS - 1)[:, None, None]
    onehot = (((kc - w + NA_COLS - 1)[None] == taps) & col_ok[None]).astype(np.float32)
    base = jnp.einsum('hrk,kcw->hrcw', rpb.astype(f32) * LOG2E, onehot, precision=lax.Precision.HIGHEST)
    base = jnp.where(col_ok, base, NEG_BIG).reshape(HEAD_PAIRS, 2, 2 * NA_ROWS - 1, GRID_W, GRID_W)
    basep = jnp.pad(base, ((0, 0), (0, 0), (2, 2), (0, 0), (0, 0)), constant_values=NEG_BIG)
    n_blk = 2 * NA_ROWS + 2
    blk = jnp.concatenate([basep[:, 0, 1:1 + n_blk], basep[:, 0, 0:n_blk],
                           basep[:, 1, 1:1 + n_blk], basep[:, 1, 0:n_blk]], axis=-1)
    kr = np.arange(NA_WIN)[:, None] // GRID_W
    par = (np.arange(2 * LANES)[None, :] // GRID_W) % 2
    tabs = []
    for roff, st0, st1 in variants:
        st = np.where(par == 0, st0, st1)
        valid = (kr >= st) & (kr < st + NA_ROWS)
        j0 = NA_ROWS - roff
        rows = blk[:, j0:j0 + NA_WIN_ROWS].reshape(HEAD_PAIRS, NA_WIN, 2 * LANES)
        tabs.append(jnp.where(valid, rows, NEG_BIG))
    return jnp.stack(tabs)


def _na_kernel(q_ref, kp_ref, kc_ref, kn_ref, vp_ref, vc_ref, vn_ref, tab_ref, o_ref, kwin_ref, vwin_ref,
               *, grid_rows):
    step = pl.program_id(2)
    me = _even_lane_mask()

    kwin_ref[0:NA_HALO, :] = kp_ref[0, 0]
    kwin_ref[NA_HALO:NA_HALO + NA_QT, :] = kc_ref[0, 0]
    kwin_ref[NA_HALO + NA_QT:, :] = kn_ref[0, 0]
    ht = NA_HALO // LANES
    vwin_ref[0:ht] = vp_ref[0, 0]
    vwin_ref[ht:ht + NA_QT // LANES] = vc_ref[0, 0]
    vwin_ref[ht + NA_QT // LANES:] = vn_ref[0, 0]
    tok0 = step * NA_QT - NA_HALO

    def window(i):
        r = 2 * (step * NA_PAIRS_PER_STEP + i)
        u0 = jnp.clip(r - NA_ROWS // 2, 0, grid_rows - NA_WIN_ROWS)
        var = jnp.where(r < 4, r // 2, jnp.where(r >= grid_rows - 4, 3 + (r - (grid_rows - 4)) // 2, 2))
        return u0, var

    def run(win_rows):
        n_keys = win_rows * GRID_W
        half = n_keys // 2

        def scores(i):
            u0, var = window(i)
            q2 = q_ref[0, 0, i * LANES:(i + 1) * LANES, :]
            zero = jnp.zeros_like(q2)
            qm = jnp.concatenate([jnp.where(me, q2, zero), jnp.where(me, zero, q2)], axis=0)
            parts = []
            for hk in range(2):
                start = pl.multiple_of(u0 * GRID_W - tok0 + hk * half, BF16_ROWS)
                kk = kwin_ref[pl.ds(start, half), :]
                parts.append(_dot_nt(kk, qm) + tab_ref[var, 0, hk * half:(hk + 1) * half, :])
            return jnp.concatenate(parts, axis=0)

        def finish(i, parts, rl):
            oc = jnp.concatenate([parts[e] * rl[:, e * LANES:(e + 1) * LANES] for e in range(2)], axis=0)
            o_ref[0, 0, i * LANES:(i + 1) * LANES, :] = oc.T.astype(bf16)

        s = scores(0)
        pending = None
        for i in range(NA_PAIRS_PER_STEP):
            s_next = scores(i + 1) if i + 1 < NA_PAIRS_PER_STEP else None
            u0, _ = window(i)
            m = jnp.max(s, axis=0, keepdims=True)
            p = jnp.exp2(s - m)
            l = jnp.sum(p, axis=0, keepdims=True)
            j0 = lax.shift_right_logical(u0 * GRID_W - tok0, 7)
            vtw = jnp.concatenate([vwin_ref[j0 + j] for j in range(NA_WIN // LANES)], axis=1)
            pb = p.astype(bf16)
            if n_keys < NA_WIN:
                pb = jnp.concatenate([pb, jnp.zeros((NA_WIN - n_keys, 2 * LANES), bf16)], axis=0)
            parts = [_dot(vtw[e * HEAD_DIM:(e + 1) * HEAD_DIM, :], pb[:, e * LANES:(e + 1) * LANES])
                     for e in range(2)]
            if pending is not None:
                finish(*pending)
            pending = (i, parts, 1.0 / l)
            s = s_next
        finish(*pending)

    last = pl.num_programs(2) - 1
    pl.when(step != last)(functools.partial(run, NA_WIN_ROWS - 1))
    pl.when(step == last)(functools.partial(run, NA_WIN_ROWS))


def _na(q, k, vt, tab):
    B, _, L, _ = q.shape
    grid_rows = L // GRID_W
    QT, H = NA_QT, NA_HALO
    r = QT // H
    nh = L // H
    nv = tab.shape[0]
    cur = lambda p, b, s: (b, p, s, 0)
    prev = lambda p, b, s: (b, p, jnp.maximum(s * r - 1, 0), 0)
    nxt = lambda p, b, s: (b, p, jnp.minimum((s + 1) * r, nh - 1), 0)
    tile = lambda f: (lambda p, b, s: f(p, b, s) + (0,))
    ht = H // LANES
    return pl.pallas_call(
        functools.partial(_na_kernel, grid_rows=grid_rows),
        grid=(HEAD_PAIRS, B, L // QT),
        in_specs=[pl.BlockSpec((1, 1, QT, LANES), cur),
                  pl.BlockSpec((1, 1, H, LANES), prev),
                  pl.BlockSpec((1, 1, QT, LANES), cur),
                  pl.BlockSpec((1, 1, H, LANES), nxt),
                  pl.BlockSpec((1, 1, ht, LANES, LANES), tile(prev)),
                  pl.BlockSpec((1, 1, QT // LANES, LANES, LANES), tile(cur)),
                  pl.BlockSpec((1, 1, ht, LANES, LANES), tile(nxt)),
                  pl.BlockSpec((nv, 1, NA_WIN, 2 * LANES), lambda p, b, s: (0, p, 0, 0))],
        out_specs=pl.BlockSpec((1, 1, QT, LANES), cur),
        out_shape=jax.ShapeDtypeStruct((B, HEAD_PAIRS, L, LANES), bf16),
        scratch_shapes=[pltpu.VMEM((QT + 2 * H, LANES), bf16),
                        pltpu.VMEM((QT // LANES + 2 * ht, LANES, LANES), bf16)],
        compiler_params=pltpu.CompilerParams(dimension_semantics=("arbitrary", "arbitrary", "arbitrary"),
                                             vmem_limit_bytes=VMEM_LIMIT),
        name="na",
    )(q, k, k, k, vt, vt, vt, tab)


def _out_kernel(x_ref, ys_ref, z_ref, yn_ref, mod_ref, gs_ref, wo_ref, gf_ref, wg_ref, wu_ref, wd_ref, o_ref, a_ref):
    gated = ys_ref[0].astype(f32) * _silu(z_ref[0].astype(f32))
    gms = jnp.mean(gated * gated, axis=-1, keepdims=True)
    y_ssd = (gated * lax.rsqrt(gms + EPS) * gs_ref[...]).astype(bf16)
    y_na = jnp.concatenate([yn_ref[0, p] for p in range(HEAD_PAIRS)], axis=1)
    mix = _dot(y_ssd, wo_ref[0:SSD_INNER, :]) + _dot(y_na, wo_ref[SSD_INNER:, :])
    x1 = x_ref[0] + mod_ref[0, 2:3, :] * mix
    ms = jnp.mean(x1 * x1, axis=-1, keepdims=True)
    h2 = x1 * lax.rsqrt(ms + EPS) * gf_ref[...]
    h2 = (h2 * (1.0 + mod_ref[0, 4:5, :]) + mod_ref[0, 3:4, :]).astype(bf16)
    for c in range(D_FF // FF_BLOCK):
        cols = slice(c * FF_BLOCK, (c + 1) * FF_BLOCK)
        gate = _dot(h2, wg_ref[:, cols])
        up = _dot(h2, wu_ref[:, cols])
        a_ref[:, cols] = (_silu(gate) * up).astype(bf16)
    ff = _dot(a_ref[...], wd_ref[...])
    o_ref[0] = x1 + mod_ref[0, 5:6, :] * ff


def _out_ffn(x, y_pre, z, y_na, mod, g_ssd, w_out, g_ffn, w_gate, w_up, w_down):
    B, L, _ = x.shape
    T = TOKEN_TILE
    tok = lambda b, i: (b, i, 0)
    const = lambda b, i: (0, 0)
    single = pl.Buffered(1)
    return pl.pallas_call(
        _out_kernel,
        grid=(B, L // T),
        in_specs=[pl.BlockSpec((1, T, D_MODEL), tok),
                  pl.BlockSpec((1, T, SSD_INNER), tok),
                  pl.BlockSpec((1, T, SSD_INNER), tok),
                  pl.BlockSpec((1, HEAD_PAIRS, T, LANES), lambda b, i: (b, 0, i, 0)),
                  pl.BlockSpec((1, 6, D_MODEL), lambda b, i: (b, 0, 0)),
                  pl.BlockSpec((1, SSD_INNER), const),
                  pl.BlockSpec((D_MODEL, D_MODEL), const, pipeline_mode=single),
                  pl.BlockSpec((1, D_MODEL), const),
                  pl.BlockSpec((D_MODEL, D_FF), const, pipeline_mode=single),
                  pl.BlockSpec((D_MODEL, D_FF), const, pipeline_mode=single),
                  pl.BlockSpec((D_FF, D_MODEL), const, pipeline_mode=single)],
        out_specs=pl.BlockSpec((1, T, D_MODEL), tok),
        out_shape=jax.ShapeDtypeStruct((B, L, D_MODEL), f32),
        scratch_shapes=[pltpu.VMEM((T, D_FF), bf16)],
        compiler_params=pltpu.CompilerParams(dimension_semantics=("arbitrary", "arbitrary"),
                                             vmem_limit_bytes=VMEM_LIMIT),
        name="out_ffn",
    )(x, y_pre, z, y_na, mod, g_ssd, w_out, g_ffn, w_gate, w_up, w_down)


def _pad_lanes(parts):
    pieces, pos = [], 0
    for off, v in parts:
        pieces += [jnp.zeros((off - pos,), f32), v.astype(f32)]
        pos = off + v.shape[0]
    pieces.append(jnp.zeros((LANES - pos,), f32))
    return jnp.concatenate(pieces).reshape(1, LANES)


def kernel(x_prompt, x_sample, c_prompt, c_sample, w_ada, b_ada, g_mix, w_in, conv_w, conv_b, dt_bias_fwd, dt_bias_bwd, a_log_fwd, a_log_bwd, d_skip, g_ssd, g_q, g_k, rpb, w_out, g_ffn, w_gate, w_up, w_down):
    l = 0
    nb_p = x_prompt.shape[0]
    nb_s = x_sample.shape[0]

    c_all = jnp.concatenate([c_prompt, c_sample, jnp.zeros((8 - nb_p - nb_s, D_MODEL), f32)], axis=0)
    mod = _modulation(c_all.T, w_ada[l], b_ada[l], nb_p + nb_s).reshape(8, 6, D_MODEL)

    o1 = SSD_INNER
    o2 = o1 + CONV_CH
    o3 = o2 + 2 * SSD_HEADS
    o4 = o3 + NA_INNER
    o5 = o4 + NA_INNER
    wi = w_in[l]
    w_main = jnp.concatenate([wi[:, :o2], wi[:, o3:o5]], axis=1).astype(bf16)
    w_vt = jnp.concatenate([wi[:, o5:], wi[:, o2:o3], wi[:, o2:o3]], axis=1).T.astype(bf16)
    dtb = _pad_lanes([(0, dt_bias_fwd[l]), (8, dt_bias_bwd[l]), (64, dt_bias_fwd[l]), (72, dt_bias_bwd[l])])
    alog = _pad_lanes([(0, a_log_fwd[l]), (8, a_log_bwd[l])])
    gq = jnp.tile(g_q[l].astype(f32) * (HEAD_DIM ** -0.5 * LOG2E), NA_HEADS).reshape(1, NA_INNER)
    gk = jnp.tile(g_k[l].astype(f32), NA_HEADS).reshape(1, NA_INNER)
    dsk = jnp.repeat(d_skip[l].astype(f32), HEAD_DIM).reshape(1, SSD_INNER)
    tab = _na_bias_tables(rpb[l])
    wo = w_out[l].astype(bf16)
    wg = w_gate[l].astype(bf16)
    wu = w_up[l].astype(bf16)
    wd = w_down[l].astype(bf16)
    row = lambda v: v.astype(f32).reshape(1, -1)

    def trunk(x, m):
        z, xbc, dtp, q, k, vt = _in_proj(x, m, row(g_mix[l]), w_main, w_vt, gq, gk, dtb, alog)
        xo, yb = _ssd_bwd(xbc, dtp, conv_w[l].astype(f32), row(conv_b[l]))
        y_pre = _ssd_fwd(xo, dtp, yb, dsk)
        y_na = _na(q, k, vt, tab)
        return _out_ffn(x, y_pre, z, y_na, m, row(g_ssd[l]), wo, row(g_ffn[l]), wg, wu, wd)

    return (trunk(x_prompt, mod[:nb_p]), trunk(x_sample, mod[nb_p:nb_p + nb_s]))
```

```python
import functools
import math

import numpy as np
import jax
import jax.numpy as jnp
from jax import lax
from jax.experimental import pallas as pl
from jax.experimental.pallas import tpu as pltpu

D_MODEL = 1024
HEAD_DIM = 64
SSD_HEADS = 8
SSD_INNER = SSD_HEADS * HEAD_DIM
SSD_GROUPS = 2
SSD_STATE = 128
CONV_K = 5
CHUNK = 128
NA_HEADS = 8
NA_INNER = NA_HEADS * HEAD_DIM
NA_ROWS = 8
NA_COLS = 16
GRID_W = 64
CONV_CH = SSD_INNER + 2 * SSD_GROUPS * SSD_STATE
D_FF = 2816
EPS = 1e-6

LANES = 128
BF16_ROWS = 16
CONV_HALO = BF16_ROWS
CONV_ROWS = 128
CONV_COLS = 256
GROUP_W = (SSD_HEADS // SSD_GROUPS) * HEAD_DIM
HEAD_PAIRS = NA_HEADS // 2
LOG2E = math.log2(math.e)
NEG_BIG = -1e30

TOKEN_TILE = 1024
SSD_TILE = 1024
NA_WIN_ROWS = 10
NA_WIN = NA_WIN_ROWS * GRID_W
NA_PAIRS_PER_STEP = 32
NA_PAIRS_SINGLE_STEP = 64
NA_HALO = 256
FF_BLOCK = 256
VMEM_LIMIT = 56 * 1024 * 1024

f32 = jnp.float32
bf16 = jnp.bfloat16


def _dot(a, b):
    return jnp.dot(a, b, preferred_element_type=f32)


def _dot_nt(a, b):
    return lax.dot_general(a, b, (((1,), (1,)), ((), ())), preferred_element_type=f32)


def _split3(v):
    p1 = v.astype(bf16)
    r1 = v - p1.astype(f32)
    p2 = r1.astype(bf16)
    p3 = (r1 - p2.astype(f32)).astype(bf16)
    return p1, p2, p3


def _even_lane_mask():
    return lax.broadcasted_iota(jnp.int32, (1, LANES), 1) < HEAD_DIM


def _silu(v):
    return v * jax.nn.sigmoid(v)


def _mod_kernel(ct_ref, w_ref, b_ref, o_ref, *, n_rows):
    st = _silu(ct_ref[...])
    w = w_ref[...]
    rows = [jnp.sum(w * st[:, b:b + 1], axis=0, keepdims=True) for b in range(n_rows)]
    rows.append(jnp.zeros((8 - n_rows, w.shape[1]), f32))
    o_ref[...] = jnp.concatenate(rows, axis=0) + b_ref[...]


def _modulation(c_t, w_ada, b_ada, n_rows):
    n = 6 * D_MODEL
    return pl.pallas_call(
        functools.partial(_mod_kernel, n_rows=n_rows),
        grid=(n // D_MODEL,),
        in_specs=[pl.BlockSpec((D_MODEL, 8), lambda j: (0, 0)),
                  pl.BlockSpec((D_MODEL, D_MODEL), lambda j: (0, j)),
                  pl.BlockSpec((1, D_MODEL), lambda j: (0, j))],
        out_specs=pl.BlockSpec((8, D_MODEL), lambda j: (0, j)),
        out_shape=jax.ShapeDtypeStruct((8, n), f32),
        name="mod",
    )(c_t, w_ada, b_ada.reshape(1, n))


def _head_rmsnorm(v, gain):
    me = _even_lane_mask()
    s = v * v
    se = jnp.sum(jnp.where(me, s, 0.0), axis=-1, keepdims=True)
    so = jnp.sum(jnp.where(me, 0.0, s), axis=-1, keepdims=True)
    ms = jnp.where(me, se, so) * (1.0 / HEAD_DIM)
    return v * lax.rsqrt(ms + EPS) * gain


def _in_kernel(x_ref, mod_ref, g_ref, w_ref, wvt_ref, gq_ref, gk_ref, dtb_ref, alog_ref,
               z_ref, xbc_ref, dtp_ref, q_ref, k_ref, vt_ref):
    x = x_ref[0]
    ms = jnp.mean(x * x, axis=-1, keepdims=True)
    y = x * lax.rsqrt(ms + EPS) * g_ref[...]
    h = (y * (1.0 + mod_ref[0, 1:2, :]) + mod_ref[0, 0:1, :]).astype(bf16)

    def proj_head_normed(dst_ref, gain_ref, col0):
        full = _dot(h, w_ref[:, col0:col0 + NA_INNER])
        for j in range(HEAD_PAIRS):
            sl = slice(j * LANES, (j + 1) * LANES)
            dst_ref[0, j] = _head_rmsnorm(full[:, sl], gain_ref[:, sl]).astype(bf16)

    def proj_v_dt():
        full = _dot_nt(wvt_ref[...], h)
        vt = full[0:NA_INNER].astype(bf16)
        for p in range(HEAD_PAIRS):
            for tb in range(TOKEN_TILE // LANES):
                vt_ref[0, p, tb] = vt[p * LANES:(p + 1) * LANES, tb * LANES:(tb + 1) * LANES]
        lane = lax.broadcasted_iota(jnp.int32, (1, LANES), 1)
        a_neg = -jnp.exp(alog_ref[...])
        mul = jnp.where(lane < 2 * SSD_HEADS, a_neg,
                        jnp.where((lane >= 64) & (lane < 64 + 2 * SSD_HEADS), 1.0, 0.0))
        n_dt = 2 * SSD_HEADS
        gap = jnp.zeros((LANES // 2 - n_dt, LANES), f32)
        for tb in range(TOKEN_TILE // LANES):
            cols = slice(tb * LANES, (tb + 1) * LANES)
            raw_t = jnp.concatenate([full[NA_INNER:NA_INNER + n_dt, cols], gap,
                                     full[NA_INNER + n_dt:NA_INNER + 2 * n_dt, cols], gap], axis=0)
            raw = raw_t.T + dtb_ref[...]
            dt = jnp.maximum(raw, 0.0) + jnp.log1p(jnp.exp(-jnp.abs(raw)))
            dtp_ref[0, cols, :] = dt * mul

    o = SSD_INNER + CONV_CH
    xbc_ref[0] = _dot(h, w_ref[:, SSD_INNER:o]).astype(bf16)
    z_ref[0] = _dot(h, w_ref[:, 0:SSD_INNER]).astype(bf16)
    proj_head_normed(q_ref, gq_ref, o)
    proj_head_normed(k_ref, gk_ref, o + NA_INNER)
    proj_v_dt()


def _in_proj(x, mod, g_mix, w_main, w_vt, gq, gk, dt_bias, a_log):
    B, L, _ = x.shape
    T = TOKEN_TILE
    nw = w_main.shape[1]
    const = lambda b, i: (0, 0)
    tok = lambda b, i: (b, i, 0)
    return pl.pallas_call(
        _in_kernel,
        grid=(B, L // T),
        in_specs=[pl.BlockSpec((1, T, D_MODEL), tok),
                  pl.BlockSpec((1, 6, D_MODEL), lambda b, i: (b, 0, 0)),
                  pl.BlockSpec((1, D_MODEL), const),
                  pl.BlockSpec((D_MODEL, nw), const),
                  pl.BlockSpec((NA_INNER + 4 * SSD_HEADS, D_MODEL), const),
                  pl.BlockSpec((1, NA_INNER), const),
                  pl.BlockSpec((1, NA_INNER), const),
                  pl.BlockSpec((1, LANES), const),
                  pl.BlockSpec((1, LANES), const)],
        out_specs=[pl.BlockSpec((1, T, SSD_INNER), tok),
                   pl.BlockSpec((1, T, CONV_CH), tok),
                   pl.BlockSpec((1, T, LANES), tok),
                   pl.BlockSpec((1, HEAD_PAIRS, T, LANES), lambda b, i: (b, 0, i, 0)),
                   pl.BlockSpec((1, HEAD_PAIRS, T, LANES), lambda b, i: (b, 0, i, 0)),
                   pl.BlockSpec((1, HEAD_PAIRS, T // LANES, LANES, LANES), lambda b, i: (b, 0, i, 0, 0))],
        out_shape=[jax.ShapeDtypeStruct((B, L, SSD_INNER), bf16),
                   jax.ShapeDtypeStruct((B, L, CONV_CH), bf16),
                   jax.ShapeDtypeStruct((B, L, LANES), f32),
                   jax.ShapeDtypeStruct((B, HEAD_PAIRS, L, LANES), bf16),
                   jax.ShapeDtypeStruct((B, HEAD_PAIRS, L, LANES), bf16),
                   jax.ShapeDtypeStruct((B, HEAD_PAIRS, L // LANES, LANES, LANES), bf16)],
        compiler_params=pltpu.CompilerParams(dimension_semantics=("arbitrary", "arbitrary"),
                                             vmem_limit_bytes=VMEM_LIMIT),
        name="in_proj",
    )(x, mod, g_mix, w_main, w_vt, gq, gk, dt_bias, a_log)


def _head_selector(base):
    r = lax.broadcasted_iota(jnp.int32, (LANES, SSD_INNER), 0)
    c = lax.broadcasted_iota(jnp.int32, (LANES, SSD_INNER), 1)
    return (lax.shift_right_logical(c, 6) == r - base).astype(bf16)


def _cumsum_matrix():
    row = lax.broadcasted_iota(jnp.int32, (CHUNK, CHUNK), 0)
    col = lax.broadcasted_iota(jnp.int32, (CHUNK, CHUNK), 1)
    return (col <= row).astype(bf16)


def _chunk_decays(p, tri):
    p1, p2, p3 = _split3(p)
    parts = _dot(tri, jnp.concatenate([p1, p2, p3], axis=1))
    incl = parts[:, 0:LANES] + parts[:, LANES:2 * LANES] + parts[:, 2 * LANES:3 * LANES]
    excl = incl - p
    dts = pltpu.roll(p, 64, 1)
    tot = incl[CHUNK - 1:CHUNK, :]
    is_fwd = lax.broadcasted_iota(jnp.int32, (1, LANES), 1) < SSD_HEADS
    w = jnp.exp(jnp.where(is_fwd, tot - incl, excl)) * dts
    sc = jnp.exp(jnp.where(is_fwd, incl, tot - excl))
    return incl, excl, dts, w, sc, jnp.exp(tot)


def _b_transposed(xo_ref, rows, g):
    b_g = xo_ref[0, rows, SSD_INNER + g * SSD_STATE:SSD_INNER + (g + 1) * SSD_STATE].astype(f32)
    return b_g.T.astype(bf16)


def _expand_decays(w, sc, dec, sel):
    v = jnp.concatenate([w, sc, jnp.broadcast_to(dec, (BF16_ROWS, LANES))], axis=0)
    n = v.shape[0]
    hi = v.astype(bf16)
    lo = (v - hi.astype(f32)).astype(bf16)
    both = _dot(jnp.concatenate([hi, lo], axis=0), sel)
    x = both[0:n] + both[n:2 * n]
    return x[0:CHUNK], x[CHUNK:2 * CHUNK], x[2 * CHUNK:2 * CHUNK + 1]


def _state_step(xo_ref, rows, st_ref, expanded, b_t):
    w_x, sc_x, dec_x = expanded
    outs = []
    for g in range(SSD_GROUPS):
        gcols = slice(g * GROUP_W, (g + 1) * GROUP_W)
        xs_g = xo_ref[0, rows, g * GROUP_W:(g + 1) * GROUP_W].astype(f32)
        c_g = xo_ref[0, rows, SSD_INNER + (SSD_GROUPS + g) * SSD_STATE:
                     SSD_INNER + (SSD_GROUPS + g + 1) * SSD_STATE]
        s_g = st_ref[g]
        y_off = _dot(c_g, s_g.astype(bf16)) * sc_x[:, gcols]
        xw = (xs_g * w_x[:, gcols]).astype(bf16)
        st_ref[g] = s_g * dec_x[:, gcols] + _dot(b_t[g], xw)
        outs.append(y_off)
    return jnp.concatenate(outs, axis=1)


def _ssd_bwd_kernel(xc_ref, xp_ref, xn_ref, dtp_ref, cw_ref, cb_ref, xo_ref, yb_ref, win_ref, st_ref, *, n_tiles):
    i = pl.program_id(1)
    t = n_tiles - 1 - i

    @pl.when(i == 0)
    def _():
        st_ref[...] = jnp.zeros_like(st_ref)

    H = CONV_HALO
    win_ref[0:H, :] = jnp.where(t > 0, xp_ref[0], jnp.zeros_like(xp_ref[0]))
    win_ref[H:H + SSD_TILE, :] = xc_ref[0]
    win_ref[H + SSD_TILE:, :] = jnp.where(t < n_tiles - 1, xn_ref[0], jnp.zeros_like(xn_ref[0]))
    r_i = lax.broadcasted_iota(jnp.int32, (CONV_ROWS, CONV_ROWS + 2 * H), 0)
    c_i = lax.broadcasted_iota(jnp.int32, (CONV_ROWS, CONV_ROWS + 2 * H), 1)
    mid = CONV_K // 2
    shift = {k: (c_i == r_i + H + k - mid).astype(bf16) for k in range(CONV_K) if k != mid}
    for c in range(SSD_TILE // CONV_ROWS):
        for cb in range(CONV_CH // CONV_COLS):
            cols = slice(cb * CONV_COLS, (cb + 1) * CONV_COLS)
            xw = win_ref[c * CONV_ROWS:(c + 1) * CONV_ROWS + 2 * H, cols]
            centre = win_ref[H + c * CONV_ROWS:H + (c + 1) * CONV_ROWS, cols].astype(f32)
            acc = cb_ref[:, cols] + centre * cw_ref[mid:mid + 1, cols]
            for k in shift:
                acc = acc + _dot(shift[k], xw) * cw_ref[k:k + 1, cols]
            xo_ref[0, c * CONV_ROWS:(c + 1) * CONV_ROWS, cols] = _silu(acc).astype(bf16)

    sel = _head_selector(SSD_HEADS)
    tri = _cumsum_matrix()

    def body(j, carry):
        c = SSD_TILE // CHUNK - 1 - j
        rows = pl.ds(pl.multiple_of(c * CHUNK, CHUNK), CHUNK)
        _, _, _, w, sc, dec = _chunk_decays(dtp_ref[0, rows, :], tri)
        b_t = [_b_transposed(xo_ref, rows, g) for g in range(SSD_GROUPS)]
        yb_ref[0, rows, :] = _state_step(xo_ref, rows, st_ref, _expand_decays(w, sc, dec, sel), b_t).astype(bf16)
        return carry

    lax.fori_loop(0, SSD_TILE // CHUNK, body, 0, unroll=True)


def _ssd_bwd(xbc, dtp, conv_w, conv_b):
    B, L, _ = xbc.shape
    T = SSD_TILE
    nT = L // T
    hb = T // CONV_HALO
    cur = lambda b, i: (b, nT - 1 - i, 0)
    prev = lambda b, i: (b, jnp.maximum((nT - 1 - i) * hb - 1, 0), 0)
    nxt = lambda b, i: (b, jnp.minimum((nT - i) * hb, L // CONV_HALO - 1), 0)
    const = lambda b, i: (0, 0)
    return pl.pallas_call(
        functools.partial(_ssd_bwd_kernel, n_tiles=nT),
        grid=(B, nT),
        in_specs=[pl.BlockSpec((1, T, CONV_CH), cur),
                  pl.BlockSpec((1, CONV_HALO, CONV_CH), prev),
                  pl.BlockSpec((1, CONV_HALO, CONV_CH), nxt),
                  pl.BlockSpec((1, T, LANES), cur),
                  pl.BlockSpec((CONV_K, CONV_CH), const),
                  pl.BlockSpec((1, CONV_CH), const)],
        out_specs=[pl.BlockSpec((1, T, CONV_CH), cur),
                   pl.BlockSpec((1, T, SSD_INNER), cur)],
        out_shape=[jax.ShapeDtypeStruct((B, L, CONV_CH), bf16),
                   jax.ShapeDtypeStruct((B, L, SSD_INNER), bf16)],
        scratch_shapes=[pltpu.VMEM((T + 2 * CONV_HALO, CONV_CH), bf16),
                        pltpu.VMEM((SSD_GROUPS, SSD_STATE, GROUP_W), f32)],
        compiler_params=pltpu.CompilerParams(dimension_semantics=("arbitrary", "arbitrary"),
                                             vmem_limit_bytes=VMEM_LIMIT),
        name="ssd_bwd",
    )(xbc, xbc, xbc, dtp, conv_w, conv_b)


def _ssd_fwd_kernel(xo_ref, dtp_ref, yb_ref, dsk_ref, o_ref, st_ref):
    @pl.when(pl.program_id(1) == 0)
    def _():
        st_ref[...] = jnp.zeros_like(st_ref)

    row = lax.broadcasted_iota(jnp.int32, (CHUNK, CHUNK), 0)
    col = lax.broadcasted_iota(jnp.int32, (CHUNK, CHUNK), 1)
    low = col < row
    diag = col == row
    me = _even_lane_mask()
    sel = _head_selector(0)
    tri = _cumsum_matrix()

    def body(c, carry):
        rows = pl.ds(pl.multiple_of(c * CHUNK, CHUNK), CHUNK)
        incl, excl, dts, w, sc, dec = _chunk_decays(dtp_ref[0, rows, :], tri)
        b_t = [_b_transposed(xo_ref, rows, g) for g in range(SSD_GROUPS)]
        y = _state_step(xo_ref, rows, st_ref, _expand_decays(w, sc, dec, sel), b_t)

        incl_t = incl.T
        excl_t = excl.T
        dts_t = dts.T
        y_diag = []
        for g in range(SSD_GROUPS):
            c_g = xo_ref[0, rows, SSD_INNER + (SSD_GROUPS + g) * SSD_STATE:
                         SSD_INNER + (SSD_GROUPS + g + 1) * SSD_STATE]
            cb = _dot(c_g, b_t[g])
            for pr in range(2):
                xs_pair = xo_ref[0, rows, g * GROUP_W + pr * LANES:g * GROUP_W + (pr + 1) * LANES]
                wms = []
                for e in range(2):
                    hd = 4 * g + 2 * pr + e
                    af_c = jnp.broadcast_to(incl[:, hd:hd + 1], (CHUNK, CHUNK))
                    eb_c = jnp.broadcast_to(excl[:, SSD_HEADS + hd:SSD_HEADS + hd + 1], (CHUNK, CHUNK))
                    af_r = incl_t[hd:hd + 1, :]
                    eb_r = excl_t[SSD_HEADS + hd:SSD_HEADS + hd + 1, :]
                    df_r = dts_t[hd:hd + 1, :]
                    db_r = dts_t[SSD_HEADS + hd:SSD_HEADS + hd + 1, :]
                    arg = jnp.where(low, af_c - af_r, eb_r - eb_c)
                    coef = jnp.where(low, df_r, jnp.where(diag, df_r + db_r, db_r))
                    wms.append((cb * jnp.exp(arg) * coef).astype(bf16))
                both = _dot(jnp.concatenate(wms, axis=0), xs_pair)
                y_diag.append(jnp.where(me, both[0:CHUNK], both[CHUNK:2 * CHUNK]))
        xs = xo_ref[0, rows, 0:SSD_INNER].astype(f32)
        y = y + jnp.concatenate(y_diag, axis=1) + yb_ref[0, rows, :].astype(f32) + xs * dsk_ref[...]
        o_ref[0, rows, :] = y.astype(bf16)
        return carry

    lax.fori_loop(0, SSD_TILE // CHUNK, body, 0, unroll=True)


def _ssd_fwd(xo, dtp, yb, d_skip_x):
    B, L, _ = xo.shape
    T = SSD_TILE
    tok = lambda b, i: (b, i, 0)
    const = lambda b, i: (0, 0)
    return pl.pallas_call(
        _ssd_fwd_kernel,
        grid=(B, L // T),
        in_specs=[pl.BlockSpec((1, T, CONV_CH), tok),
                  pl.BlockSpec((1, T, LANES), tok),
                  pl.BlockSpec((1, T, SSD_INNER), tok),
                  pl.BlockSpec((1, SSD_INNER), const)],
        out_specs=pl.BlockSpec((1, T, SSD_INNER), tok),
        out_shape=jax.ShapeDtypeStruct((B, L, SSD_INNER), bf16),
        scratch_shapes=[pltpu.VMEM((SSD_GROUPS, SSD_STATE, GROUP_W), f32)],
        compiler_params=pltpu.CompilerParams(dimension_semantics=("arbitrary", "arbitrary"),
                                             vmem_limit_bytes=VMEM_LIMIT),
        name="ssd_fwd",
    )(xo, dtp, yb, d_skip_x)


def _na_bias_tables(rpb):
    variants = [(0, 0, 0), (2, 0, 0), (4, 0, 1), (6, 2, 2), (8, 2, 2)]
    kc = np.arange(GRID_W)[:, None]
    w = np.arange(GRID_W)[None, :]
    cs = np.clip(w - NA_COLS // 2, 0, GRID_W - NA_COLS)
    col_ok = (kc >= cs) & (kc < cs + NA_COLS)
    taps = np.arange(2 * NA_COLS - 1)[:, None, None]
    onehot = (((kc - w + NA_COLS - 1)[None] == taps) & col_ok[None]).astype(np.float32)
    base = jnp.einsum('hrk,kcw->hrcw', rpb.astype(f32) * LOG2E, onehot, precision=lax.Precision.HIGHEST)
    base = jnp.where(col_ok, base, NEG_BIG).reshape(HEAD_PAIRS, 2, 2 * NA_ROWS - 1, GRID_W, GRID_W)
    basep = jnp.pad(base, ((0, 0), (0, 0), (2, 2), (0, 0), (0, 0)), constant_values=NEG_BIG)
    n_blk = 2 * NA_ROWS + 2
    blk = jnp.concatenate([basep[:, 0, 1:1 + n_blk], basep[:, 0, 0:n_blk],
                           basep[:, 1, 1:1 + n_blk], basep[:, 1, 0:n_blk]], axis=-1)
    kr = np.arange(NA_WIN)[:, None] // GRID_W
    par = (np.arange(2 * LANES)[None, :] // GRID_W) % 2
    tabs = []
    for roff, st0, st1 in variants:
        st = np.where(par == 0, st0, st1)
        valid = (kr >= st) & (kr < st + NA_ROWS)
        j0 = NA_ROWS - roff
        rows = blk[:, j0:j0 + NA_WIN_ROWS].reshape(HEAD_PAIRS, NA_WIN, 2 * LANES)
        tabs.append(jnp.where(valid, rows, NEG_BIG))
    return jnp.stack(tabs)


def _na_kernel(q_ref, kp_ref, kc_ref, kn_ref, vp_ref, vc_ref, vn_ref, tab_ref, o_ref, kwin_ref, vwin_ref,
               *, grid_rows, pairs):
    step = pl.program_id(2)
    qt = pairs * 2 * GRID_W
    me = _even_lane_mask()

    kwin_ref[0:NA_HALO, :] = kp_ref[0, 0]
    kwin_ref[NA_HALO:NA_HALO + qt, :] = kc_ref[0, 0]
    kwin_ref[NA_HALO + qt:, :] = kn_ref[0, 0]
    ht = NA_HALO // LANES
    vwin_ref[0:ht] = vp_ref[0, 0]
    vwin_ref[ht:ht + qt // LANES] = vc_ref[0, 0]
    vwin_ref[ht + qt // LANES:] = vn_ref[0, 0]
    tok0 = step * qt - NA_HALO

    def window(i):
        r = 2 * (step * pairs + i)
        u0 = jnp.clip(r - NA_ROWS // 2, 0, grid_rows - NA_WIN_ROWS)
        var = jnp.where(r < 4, r // 2, jnp.where(r >= grid_rows - 4, 3 + (r - (grid_rows - 4)) // 2, 2))
        return u0, var

    def run(edge_pairs):
        def keys(i):
            rows = NA_WIN_ROWS if i >= pairs - edge_pairs else NA_WIN_ROWS - 1
            return rows * GRID_W

        def scores(i):
            u0, var = window(i)
            half = keys(i) // 2
            q2 = q_ref[0, 0, i * LANES:(i + 1) * LANES, :]
            zero = jnp.zeros_like(q2)
            qm = jnp.concatenate([jnp.where(me, q2, zero), jnp.where(me, zero, q2)], axis=0)
            parts = []
            for hk in range(2):
                start = pl.multiple_of(u0 * GRID_W - tok0 + hk * half, BF16_ROWS)
                kk = kwin_ref[pl.ds(start, half), :]
                parts.append(_dot_nt(kk, qm) + tab_ref[var, 0, hk * half:(hk + 1) * half, :])
            return jnp.concatenate(parts, axis=0)

        def finish(i, parts, rl):
            oc = jnp.concatenate([parts[e] * rl[:, e * LANES:(e + 1) * LANES] for e in range(2)], axis=0)
            o_ref[0, 0, i * LANES:(i + 1) * LANES, :] = oc.T.astype(bf16)

        s = scores(0)
        pending = None
        for i in range(pairs):
            s_next = scores(i + 1) if i + 1 < pairs else None
            u0, _ = window(i)
            m = jnp.max(s, axis=0, keepdims=True)
            p = jnp.exp2(s - m)
            l = jnp.sum(p, axis=0, keepdims=True)
            j0 = lax.shift_right_logical(u0 * GRID_W - tok0, 7)
            vtw = jnp.concatenate([vwin_ref[j0 + j] for j in range(NA_WIN // LANES)], axis=1)
            pb = p.astype(bf16)
            if keys(i) < NA_WIN:
                pb = jnp.concatenate([pb, jnp.zeros((NA_WIN - keys(i), 2 * LANES), bf16)], axis=0)
            parts = [_dot(vtw[e * HEAD_DIM:(e + 1) * HEAD_DIM, :], pb[:, e * LANES:(e + 1) * LANES])
                     for e in range(2)]
            if pending is not None:
                finish(*pending)
            pending = (i, parts, 1.0 / l)
            s = s_next
        finish(*pending)

    last = pl.num_programs(2) - 1
    pl.when(step != last)(functools.partial(run, 0))
    pl.when(step == last)(functools.partial(run, 2))


def _na(q, k, vt, tab):
    B, _, L, _ = q.shape
    grid_rows = L // GRID_W
    pairs = NA_PAIRS_SINGLE_STEP if L == NA_PAIRS_SINGLE_STEP * 2 * GRID_W else NA_PAIRS_PER_STEP
    QT, H = pairs * 2 * GRID_W, NA_HALO
    r = QT // H
    nh = L // H
    nv = tab.shape[0]
    cur = lambda p, b, s: (b, p, s, 0)
    prev = lambda p, b, s: (b, p, jnp.maximum(s * r - 1, 0), 0)
    nxt = lambda p, b, s: (b, p, jnp.minimum((s + 1) * r, nh - 1), 0)
    tile = lambda f: (lambda p, b, s: f(p, b, s) + (0,))
    ht = H // LANES
    return pl.pallas_call(
        functools.partial(_na_kernel, grid_rows=grid_rows, pairs=pairs),
        grid=(HEAD_PAIRS, B, L // QT),
        in_specs=[pl.BlockSpec((1, 1, QT, LANES), cur),
                  pl.BlockSpec((1, 1, H, LANES), prev),
                  pl.BlockSpec((1, 1, QT, LANES), cur),
                  pl.BlockSpec((1, 1, H, LANES), nxt),
                  pl.BlockSpec((1, 1, ht, LANES, LANES), tile(prev)),
                  pl.BlockSpec((1, 1, QT // LANES, LANES, LANES), tile(cur)),
                  pl.BlockSpec((1, 1, ht, LANES, LANES), tile(nxt)),
                  pl.BlockSpec((nv, 1, NA_WIN, 2 * LANES), lambda p, b, s: (0, p, 0, 0))],
        out_specs=pl.BlockSpec((1, 1, QT, LANES), cur),
        out_shape=jax.ShapeDtypeStruct((B, HEAD_PAIRS, L, LANES), bf16),
        scratch_shapes=[pltpu.VMEM((QT + 2 * H, LANES), bf16),
                        pltpu.VMEM((QT // LANES + 2 * ht, LANES, LANES), bf16)],
        compiler_params=pltpu.CompilerParams(dimension_semantics=("arbitrary", "arbitrary", "arbitrary"),
                                             vmem_limit_bytes=VMEM_LIMIT),
        name="na",
    )(q, k, k, k, vt, vt, vt, tab)


def _out_kernel(x_ref, ys_ref, z_ref, yn_ref, mod_ref, gs_ref, wo_ref, gf_ref, wg_ref, wu_ref, wd_ref, o_ref, a_ref):
    gated = ys_ref[0].astype(f32) * _silu(z_ref[0].astype(f32))
    gms = jnp.mean(gated * gated, axis=-1, keepdims=True)
    y_ssd = (gated * lax.rsqrt(gms + EPS) * gs_ref[...]).astype(bf16)
    y_na = jnp.concatenate([yn_ref[0, p] for p in range(HEAD_PAIRS)], axis=1)
    mix = _dot(y_ssd, wo_ref[0:SSD_INNER, :]) + _dot(y_na, wo_ref[SSD_INNER:, :])
    x1 = x_ref[0] + mod_ref[0, 2:3, :] * mix
    ms = jnp.mean(x1 * x1, axis=-1, keepdims=True)
    h2 = x1 * lax.rsqrt(ms + EPS) * gf_ref[...]
    h2 = (h2 * (1.0 + mod_ref[0, 4:5, :]) + mod_ref[0, 3:4, :]).astype(bf16)
    for c in range(D_FF // FF_BLOCK):
        cols = slice(c * FF_BLOCK, (c + 1) * FF_BLOCK)
        gate = _dot(h2, wg_ref[:, cols])
        up = _dot(h2, wu_ref[:, cols])
        a_ref[:, cols] = (_silu(gate) * up).astype(bf16)
    ff = _dot(a_ref[...], wd_ref[...])
    o_ref[0] = x1 + mod_ref[0, 5:6, :] * ff


def _out_ffn(x, y_pre, z, y_na, mod, g_ssd, w_out, g_ffn, w_gate, w_up, w_down):
    B, L, _ = x.shape
    T = TOKEN_TILE
    tok = lambda b, i: (b, i, 0)
    const = lambda b, i: (0, 0)
    single = pl.Buffered(1)
    return pl.pallas_call(
        _out_kernel,
        grid=(B, L // T),
        in_specs=[pl.BlockSpec((1, T, D_MODEL), tok),
                  pl.BlockSpec((1, T, SSD_INNER), tok),
                  pl.BlockSpec((1, T, SSD_INNER), tok),
                  pl.BlockSpec((1, HEAD_PAIRS, T, LANES), lambda b, i: (b, 0, i, 0)),
                  pl.BlockSpec((1, 6, D_MODEL), lambda b, i: (b, 0, 0)),
                  pl.BlockSpec((1, SSD_INNER), const),
                  pl.BlockSpec((D_MODEL, D_MODEL), const, pipeline_mode=single),
                  pl.BlockSpec((1, D_MODEL), const),
                  pl.BlockSpec((D_MODEL, D_FF), const, pipeline_mode=single),
                  pl.BlockSpec((D_MODEL, D_FF), const, pipeline_mode=single),
                  pl.BlockSpec((D_FF, D_MODEL), const, pipeline_mode=single)],
        out_specs=pl.BlockSpec((1, T, D_MODEL), tok),
        out_shape=jax.ShapeDtypeStruct((B, L, D_MODEL), f32),
        scratch_shapes=[pltpu.VMEM((T, D_FF), bf16)],
        compiler_params=pltpu.CompilerParams(dimension_semantics=("arbitrary", "arbitrary"),
                                             vmem_limit_bytes=VMEM_LIMIT),
        name="out_ffn",
    )(x, y_pre, z, y_na, mod, g_ssd, w_out, g_ffn, w_gate, w_up, w_down)


def _pad_lanes(parts):
    pieces, pos = [], 0
    for off, v in parts:
        pieces += [jnp.zeros((off - pos,), f32), v.astype(f32)]
        pos = off + v.shape[0]
    pieces.append(jnp.zeros((LANES - pos,), f32))
    return jnp.concatenate(pieces).reshape(1, LANES)


def kernel(x_prompt, x_sample, c_prompt, c_sample, w_ada, b_ada, g_mix, w_in, conv_w, conv_b, dt_bias_fwd, dt_bias_bwd, a_log_fwd, a_log_bwd, d_skip, g_ssd, g_q, g_k, rpb, w_out, g_ffn, w_gate, w_up, w_down):
    l = 0
    nb_p = x_prompt.shape[0]
    nb_s = x_sample.shape[0]

    c_all = jnp.concatenate([c_prompt, c_sample, jnp.zeros((8 - nb_p - nb_s, D_MODEL), f32)], axis=0)
    mod = _modulation(c_all.T, w_ada[l], b_ada[l], nb_p + nb_s).reshape(8, 6, D_MODEL)

    o1 = SSD_INNER
    o2 = o1 + CONV_CH
    o3 = o2 + 2 * SSD_HEADS
    o4 = o3 + NA_INNER
    o5 = o4 + NA_INNER
    wi = w_in[l]
    w_main = jnp.concatenate([wi[:, :o2], wi[:, o3:o5]], axis=1).astype(bf16)
    w_vt = jnp.concatenate([wi[:, o5:], wi[:, o2:o3], wi[:, o2:o3]], axis=1).T.astype(bf16)
    dtb = _pad_lanes([(0, dt_bias_fwd[l]), (8, dt_bias_bwd[l]), (64, dt_bias_fwd[l]), (72, dt_bias_bwd[l])])
    alog = _pad_lanes([(0, a_log_fwd[l]), (8, a_log_bwd[l])])
    gq = jnp.tile(g_q[l].astype(f32) * (HEAD_DIM ** -0.5 * LOG2E), NA_HEADS).reshape(1, NA_INNER)
    gk = jnp.tile(g_k[l].astype(f32), NA_HEADS).reshape(1, NA_INNER)
    dsk = jnp.repeat(d_skip[l].astype(f32), HEAD_DIM).reshape(1, SSD_INNER)
    tab = _na_bias_tables(rpb[l])
    wo = w_out[l].astype(bf16)
    wg = w_gate[l].astype(bf16)
    wu = w_up[l].astype(bf16)
    wd = w_down[l].astype(bf16)
    row = lambda v: v.astype(f32).reshape(1, -1)

    def trunk(x, m):
        z, xbc, dtp, q, k, vt = _in_proj(x, m, row(g_mix[l]), w_main, w_vt, gq, gk, dtb, alog)
        xo, yb = _ssd_bwd(xbc, dtp, conv_w[l].astype(f32), row(conv_b[l]))
        y_pre = _ssd_fwd(xo, dtp, yb, dsk)
        y_na = _na(q, k, vt, tab)
        return _out_ffn(x, y_pre, z, y_na, m, row(g_ssd[l]), wo, row(g_ffn[l]), wg, wu, wd)

    return (trunk(x_prompt, mod[:nb_p]), trunk(x_sample, mod[nb_p:nb_p + nb_s]))
```

```python
import functools
import math

import numpy as np
import jax
import jax.numpy as jnp
from jax import lax
from jax.experimental import pallas as pl
from jax.experimental.pallas import tpu as pltpu

D_MODEL = 1024
HEAD_DIM = 64
SSD_HEADS = 8
SSD_INNER = SSD_HEADS * HEAD_DIM
SSD_GROUPS = 2
SSD_STATE = 128
CONV_K = 5
CHUNK = 128
NA_HEADS = 8
NA_INNER = NA_HEADS * HEAD_DIM
NA_ROWS = 8
NA_COLS = 16
GRID_W = 64
CONV_CH = SSD_INNER + 2 * SSD_GROUPS * SSD_STATE
D_FF = 2816
EPS = 1e-6

LANES = 128
BF16_ROWS = 16
CONV_HALO = BF16_ROWS
CONV_ROWS = 128
CONV_COLS = 256
GROUP_W = (SSD_HEADS // SSD_GROUPS) * HEAD_DIM
HEAD_PAIRS = NA_HEADS // 2
LOG2E = math.log2(math.e)
NEG_BIG = -1e30

TOKEN_TILE = 1024
SSD_TILE = 1024
NA_WIN_ROWS = 10
NA_WIN = NA_WIN_ROWS * GRID_W
NA_PAIRS_PER_STEP = 32
NA_PAIRS_SINGLE_STEP = 64
NA_HALO = 256
FF_BLOCK = 256
VMEM_LIMIT = 56 * 1024 * 1024

f32 = jnp.float32
bf16 = jnp.bfloat16


def _dot(a, b):
    return jnp.dot(a, b, preferred_element_type=f32)


def _dot_nt(a, b):
    return lax.dot_general(a, b, (((1,), (1,)), ((), ())), preferred_element_type=f32)


def _split3(v):
    p1 = v.astype(bf16)
    r1 = v - p1.astype(f32)
    p2 = r1.astype(bf16)
    p3 = (r1 - p2.astype(f32)).astype(bf16)
    return p1, p2, p3


def _even_lane_mask():
    return lax.broadcasted_iota(jnp.int32, (1, LANES), 1) < HEAD_DIM


def _silu(v):
    return v * jax.nn.sigmoid(v)


def _mod_kernel(ct_ref, w_ref, b_ref, o_ref, *, n_rows):
    st = _silu(ct_ref[...])
    w = w_ref[...]
    rows = [jnp.sum(w * st[:, b:b + 1], axis=0, keepdims=True) for b in range(n_rows)]
    rows.append(jnp.zeros((8 - n_rows, w.shape[1]), f32))
    o_ref[...] = jnp.concatenate(rows, axis=0) + b_ref[...]


def _modulation(c_t, w_ada, b_ada, n_rows):
    n = 6 * D_MODEL
    return pl.pallas_call(
        functools.partial(_mod_kernel, n_rows=n_rows),
        grid=(n // D_MODEL,),
        in_specs=[pl.BlockSpec((D_MODEL, 8), lambda j: (0, 0)),
                  pl.BlockSpec((D_MODEL, D_MODEL), lambda j: (0, j)),
                  pl.BlockSpec((1, D_MODEL), lambda j: (0, j))],
        out_specs=pl.BlockSpec((8, D_MODEL), lambda j: (0, j)),
        out_shape=jax.ShapeDtypeStruct((8, n), f32),
        name="mod",
    )(c_t, w_ada, b_ada.reshape(1, n))


def _head_rmsnorm(v, gain):
    me = _even_lane_mask()
    s = v * v
    se = jnp.sum(jnp.where(me, s, 0.0), axis=-1, keepdims=True)
    so = jnp.sum(jnp.where(me, 0.0, s), axis=-1, keepdims=True)
    ms = jnp.where(me, se, so) * (1.0 / HEAD_DIM)
    return v * lax.rsqrt(ms + EPS) * gain


def _in_kernel(x_ref, mod_ref, g_ref, w_ref, wvt_ref, gq_ref, gk_ref, dtb_ref, alog_ref,
               z_ref, xbc_ref, dtp_ref, q_ref, k_ref, vt_ref):
    x = x_ref[0]
    ms = jnp.mean(x * x, axis=-1, keepdims=True)
    y = x * lax.rsqrt(ms + EPS) * g_ref[...]
    h = (y * (1.0 + mod_ref[0, 1:2, :]) + mod_ref[0, 0:1, :]).astype(bf16)

    def proj_head_normed(dst_ref, gain_ref, col0):
        full = _dot(h, w_ref[:, col0:col0 + NA_INNER])
        for j in range(HEAD_PAIRS):
            sl = slice(j * LANES, (j + 1) * LANES)
            dst_ref[0, j] = _head_rmsnorm(full[:, sl], gain_ref[:, sl]).astype(bf16)

    def proj_v_dt():
        full = _dot_nt(wvt_ref[...], h)
        vt = full[0:NA_INNER].astype(bf16)
        for p in range(HEAD_PAIRS):
            for tb in range(TOKEN_TILE // LANES):
                vt_ref[0, p, tb] = vt[p * LANES:(p + 1) * LANES, tb * LANES:(tb + 1) * LANES]
        lane = lax.broadcasted_iota(jnp.int32, (1, LANES), 1)
        a_neg = -jnp.exp(alog_ref[...])
        mul = jnp.where(lane < 2 * SSD_HEADS, a_neg,
                        jnp.where((lane >= 64) & (lane < 64 + 2 * SSD_HEADS), 1.0, 0.0))
        n_dt = 2 * SSD_HEADS
        gap = jnp.zeros((LANES // 2 - n_dt, LANES), f32)
        for tb in range(TOKEN_TILE // LANES):
            cols = slice(tb * LANES, (tb + 1) * LANES)
            raw_t = jnp.concatenate([full[NA_INNER:NA_INNER + n_dt, cols], gap,
                                     full[NA_INNER + n_dt:NA_INNER + 2 * n_dt, cols], gap], axis=0)
            raw = raw_t.T + dtb_ref[...]
            dt = jnp.maximum(raw, 0.0) + jnp.log1p(jnp.exp(-jnp.abs(raw)))
            dtp_ref[0, cols, :] = dt * mul

    o = SSD_INNER + CONV_CH
    xbc_ref[0] = _dot(h, w_ref[:, SSD_INNER:o]).astype(bf16)
    z_ref[0] = _dot(h, w_ref[:, 0:SSD_INNER]).astype(bf16)
    proj_head_normed(q_ref, gq_ref, o)
    proj_head_normed(k_ref, gk_ref, o + NA_INNER)
    proj_v_dt()


def _in_proj(x, mod, g_mix, w_main, w_vt, gq, gk, dt_bias, a_log):
    B, L, _ = x.shape
    T = TOKEN_TILE
    nw = w_main.shape[1]
    const = lambda b, i: (0, 0)
    tok = lambda b, i: (b, i, 0)
    return pl.pallas_call(
        _in_kernel,
        grid=(B, L // T),
        in_specs=[pl.BlockSpec((1, T, D_MODEL), tok),
                  pl.BlockSpec((1, 6, D_MODEL), lambda b, i: (b, 0, 0)),
                  pl.BlockSpec((1, D_MODEL), const),
                  pl.BlockSpec((D_MODEL, nw), const),
                  pl.BlockSpec((NA_INNER + 4 * SSD_HEADS, D_MODEL), const),
                  pl.BlockSpec((1, NA_INNER), const),
                  pl.BlockSpec((1, NA_INNER), const),
                  pl.BlockSpec((1, LANES), const),
                  pl.BlockSpec((1, LANES), const)],
        out_specs=[pl.BlockSpec((1, T, SSD_INNER), tok),
                   pl.BlockSpec((1, T, CONV_CH), tok),
                   pl.BlockSpec((1, T, LANES), tok),
                   pl.BlockSpec((1, HEAD_PAIRS, T, LANES), lambda b, i: (b, 0, i, 0)),
                   pl.BlockSpec((1, HEAD_PAIRS, T, LANES), lambda b, i: (b, 0, i, 0)),
                   pl.BlockSpec((1, HEAD_PAIRS, T // LANES, LANES, LANES), lambda b, i: (b, 0, i, 0, 0))],
        out_shape=[jax.ShapeDtypeStruct((B, L, SSD_INNER), bf16),
                   jax.ShapeDtypeStruct((B, L, CONV_CH), bf16),
                   jax.ShapeDtypeStruct((B, L, LANES), f32),
                   jax.ShapeDtypeStruct((B, HEAD_PAIRS, L, LANES), bf16),
                   jax.ShapeDtypeStruct((B, HEAD_PAIRS, L, LANES), bf16),
                   jax.ShapeDtypeStruct((B, HEAD_PAIRS, L // LANES, LANES, LANES), bf16)],
        compiler_params=pltpu.CompilerParams(dimension_semantics=("arbitrary", "arbitrary"),
                                             vmem_limit_bytes=VMEM_LIMIT,
                                             allow_input_fusion=[False, False, False, True, True,
                                                                 False, False, False, False]),
        name="in_proj",
    )(x, mod, g_mix, w_main, w_vt, gq, gk, dt_bias, a_log)


def _head_selector(base):
    r = lax.broadcasted_iota(jnp.int32, (LANES, SSD_INNER), 0)
    c = lax.broadcasted_iota(jnp.int32, (LANES, SSD_INNER), 1)
    return (lax.shift_right_logical(c, 6) == r - base).astype(bf16)


def _cumsum_matrix():
    row = lax.broadcasted_iota(jnp.int32, (CHUNK, CHUNK), 0)
    col = lax.broadcasted_iota(jnp.int32, (CHUNK, CHUNK), 1)
    return (col <= row).astype(bf16)


def _chunk_decays(p, tri):
    p1, p2, p3 = _split3(p)
    parts = _dot(tri, jnp.concatenate([p1, p2, p3], axis=1))
    incl = parts[:, 0:LANES] + parts[:, LANES:2 * LANES] + parts[:, 2 * LANES:3 * LANES]
    excl = incl - p
    dts = pltpu.roll(p, 64, 1)
    tot = incl[CHUNK - 1:CHUNK, :]
    is_fwd = lax.broadcasted_iota(jnp.int32, (1, LANES), 1) < SSD_HEADS
    w = jnp.exp(jnp.where(is_fwd, tot - incl, excl)) * dts
    sc = jnp.exp(jnp.where(is_fwd, incl, tot - excl))
    return incl, excl, dts, w, sc, jnp.exp(tot)


def _b_transposed(xo_ref, rows, g):
    b_g = xo_ref[0, rows, SSD_INNER + g * SSD_STATE:SSD_INNER + (g + 1) * SSD_STATE].astype(f32)
    return b_g.T.astype(bf16)


def _expand_decays(w, sc, dec, sel):
    v = jnp.concatenate([w, sc, jnp.broadcast_to(dec, (BF16_ROWS, LANES))], axis=0)
    n = v.shape[0]
    hi = v.astype(bf16)
    lo = (v - hi.astype(f32)).astype(bf16)
    both = _dot(jnp.concatenate([hi, lo], axis=0), sel)
    x = both[0:n] + both[n:2 * n]
    return x[0:CHUNK], x[CHUNK:2 * CHUNK], x[2 * CHUNK:2 * CHUNK + 1]


def _state_step(xo_ref, rows, st_ref, expanded, b_t):
    w_x, sc_x, dec_x = expanded
    outs = []
    for g in range(SSD_GROUPS):
        gcols = slice(g * GROUP_W, (g + 1) * GROUP_W)
        xs_g = xo_ref[0, rows, g * GROUP_W:(g + 1) * GROUP_W].astype(f32)
        c_g = xo_ref[0, rows, SSD_INNER + (SSD_GROUPS + g) * SSD_STATE:
                     SSD_INNER + (SSD_GROUPS + g + 1) * SSD_STATE]
        s_g = st_ref[g]
        y_off = _dot(c_g, s_g.astype(bf16)) * sc_x[:, gcols]
        xw = (xs_g * w_x[:, gcols]).astype(bf16)
        st_ref[g] = s_g * dec_x[:, gcols] + _dot(b_t[g], xw)
        outs.append(y_off)
    return jnp.concatenate(outs, axis=1)


def _ssd_bwd_kernel(xc_ref, xp_ref, xn_ref, dtp_ref, cw_ref, cb_ref, xo_ref, yb_ref, win_ref, st_ref, *, n_tiles):
    i = pl.program_id(1)
    t = n_tiles - 1 - i

    @pl.when(i == 0)
    def _():
        st_ref[...] = jnp.zeros_like(st_ref)

    H = CONV_HALO
    win_ref[0:H, :] = jnp.where(t > 0, xp_ref[0], jnp.zeros_like(xp_ref[0]))
    win_ref[H:H + SSD_TILE, :] = xc_ref[0]
    win_ref[H + SSD_TILE:, :] = jnp.where(t < n_tiles - 1, xn_ref[0], jnp.zeros_like(xn_ref[0]))
    r_i = lax.broadcasted_iota(jnp.int32, (CONV_ROWS, CONV_ROWS + 2 * H), 0)
    c_i = lax.broadcasted_iota(jnp.int32, (CONV_ROWS, CONV_ROWS + 2 * H), 1)
    mid = CONV_K // 2
    shift = {k: (c_i == r_i + H + k - mid).astype(bf16) for k in range(CONV_K) if k != mid}
    for c in range(SSD_TILE // CONV_ROWS):
        for cb in range(CONV_CH // CONV_COLS):
            cols = slice(cb * CONV_COLS, (cb + 1) * CONV_COLS)
            xw = win_ref[c * CONV_ROWS:(c + 1) * CONV_ROWS + 2 * H, cols]
            centre = win_ref[H + c * CONV_ROWS:H + (c + 1) * CONV_ROWS, cols].astype(f32)
            acc = cb_ref[:, cols] + centre * cw_ref[mid:mid + 1, cols]
            for k in shift:
                acc = acc + _dot(shift[k], xw) * cw_ref[k:k + 1, cols]
            xo_ref[0, c * CONV_ROWS:(c + 1) * CONV_ROWS, cols] = _silu(acc).astype(bf16)

    sel = _head_selector(SSD_HEADS)
    tri = _cumsum_matrix()

    def body(j, carry):
        c = SSD_TILE // CHUNK - 1 - j
        rows = pl.ds(pl.multiple_of(c * CHUNK, CHUNK), CHUNK)
        _, _, _, w, sc, dec = _chunk_decays(dtp_ref[0, rows, :], tri)
        b_t = [_b_transposed(xo_ref, rows, g) for g in range(SSD_GROUPS)]
        yb_ref[0, rows, :] = _state_step(xo_ref, rows, st_ref, _expand_decays(w, sc, dec, sel), b_t).astype(bf16)
        return carry

    lax.fori_loop(0, SSD_TILE // CHUNK, body, 0, unroll=True)


def _ssd_bwd(xbc, dtp, conv_w, conv_b):
    B, L, _ = xbc.shape
    T = SSD_TILE
    nT = L // T
    hb = T // CONV_HALO
    cur = lambda b, i: (b, nT - 1 - i, 0)
    prev = lambda b, i: (b, jnp.maximum((nT - 1 - i) * hb - 1, 0), 0)
    nxt = lambda b, i: (b, jnp.minimum((nT - i) * hb, L // CONV_HALO - 1), 0)
    const = lambda b, i: (0, 0)
    return pl.pallas_call(
        functools.partial(_ssd_bwd_kernel, n_tiles=nT),
        grid=(B, nT),
        in_specs=[pl.BlockSpec((1, T, CONV_CH), cur),
                  pl.BlockSpec((1, CONV_HALO, CONV_CH), prev),
                  pl.BlockSpec((1, CONV_HALO, CONV_CH), nxt),
                  pl.BlockSpec((1, T, LANES), cur),
                  pl.BlockSpec((CONV_K, CONV_CH), const),
                  pl.BlockSpec((1, CONV_CH), const)],
        out_specs=[pl.BlockSpec((1, T, CONV_CH), cur),
                   pl.BlockSpec((1, T, SSD_INNER), cur)],
        out_shape=[jax.ShapeDtypeStruct((B, L, CONV_CH), bf16),
                   jax.ShapeDtypeStruct((B, L, SSD_INNER), bf16)],
        scratch_shapes=[pltpu.VMEM((T + 2 * CONV_HALO, CONV_CH), bf16),
                        pltpu.VMEM((SSD_GROUPS, SSD_STATE, GROUP_W), f32)],
        compiler_params=pltpu.CompilerParams(dimension_semantics=("arbitrary", "arbitrary"),
                                             vmem_limit_bytes=VMEM_LIMIT),
        name="ssd_bwd",
    )(xbc, xbc, xbc, dtp, conv_w, conv_b)


def _ssd_fwd_kernel(xo_ref, dtp_ref, yb_ref, dsk_ref, o_ref, st_ref):
    @pl.when(pl.program_id(1) == 0)
    def _():
        st_ref[...] = jnp.zeros_like(st_ref)

    row = lax.broadcasted_iota(jnp.int32, (CHUNK, CHUNK), 0)
    col = lax.broadcasted_iota(jnp.int32, (CHUNK, CHUNK), 1)
    low = col < row
    diag = col == row
    me = _even_lane_mask()
    sel = _head_selector(0)
    tri = _cumsum_matrix()

    def body(c, carry):
        rows = pl.ds(pl.multiple_of(c * CHUNK, CHUNK), CHUNK)
        incl, excl, dts, w, sc, dec = _chunk_decays(dtp_ref[0, rows, :], tri)
        b_t = [_b_transposed(xo_ref, rows, g) for g in range(SSD_GROUPS)]
        y = _state_step(xo_ref, rows, st_ref, _expand_decays(w, sc, dec, sel), b_t)

        incl_t = incl.T
        excl_t = excl.T
        dts_t = dts.T
        y_diag = []
        for g in range(SSD_GROUPS):
            c_g = xo_ref[0, rows, SSD_INNER + (SSD_GROUPS + g) * SSD_STATE:
                         SSD_INNER + (SSD_GROUPS + g + 1) * SSD_STATE]
            cb = _dot(c_g, b_t[g])
            for pr in range(2):
                xs_pair = xo_ref[0, rows, g * GROUP_W + pr * LANES:g * GROUP_W + (pr + 1) * LANES]
                wms = []
                for e in range(2):
                    hd = 4 * g + 2 * pr + e
                    af_c = jnp.broadcast_to(incl[:, hd:hd + 1], (CHUNK, CHUNK))
                    eb_c = jnp.broadcast_to(excl[:, SSD_HEADS + hd:SSD_HEADS + hd + 1], (CHUNK, CHUNK))
                    af_r = incl_t[hd:hd + 1, :]
                    eb_r = excl_t[SSD_HEADS + hd:SSD_HEADS + hd + 1, :]
                    df_r = dts_t[hd:hd + 1, :]
                    db_r = dts_t[SSD_HEADS + hd:SSD_HEADS + hd + 1, :]
                    arg = jnp.where(low, af_c - af_r, eb_r - eb_c)
                    coef = jnp.where(low, df_r, jnp.where(diag, df_r + db_r, db_r))
                    wms.append((cb * jnp.exp(arg) * coef).astype(bf16))
                both = _dot(jnp.concatenate(wms, axis=0), xs_pair)
                y_diag.append(jnp.where(me, both[0:CHUNK], both[CHUNK:2 * CHUNK]))
        xs = xo_ref[0, rows, 0:SSD_INNER].astype(f32)
        y = y + jnp.concatenate(y_diag, axis=1) + yb_ref[0, rows, :].astype(f32) + xs * dsk_ref[...]
        o_ref[0, rows, :] = y.astype(bf16)
        return carry

    lax.fori_loop(0, SSD_TILE // CHUNK, body, 0, unroll=True)


def _ssd_fwd(xo, dtp, yb, d_skip_x):
    B, L, _ = xo.shape
    T = SSD_TILE
    tok = lambda b, i: (b, i, 0)
    const = lambda b, i: (0, 0)
    return pl.pallas_call(
        _ssd_fwd_kernel,
        grid=(B, L // T),
        in_specs=[pl.BlockSpec((1, T, CONV_CH), tok),
                  pl.BlockSpec((1, T, LANES), tok),
                  pl.BlockSpec((1, T, SSD_INNER), tok),
                  pl.BlockSpec((1, SSD_INNER), const)],
        out_specs=pl.BlockSpec((1, T, SSD_INNER), tok),
        out_shape=jax.ShapeDtypeStruct((B, L, SSD_INNER), bf16),
        scratch_shapes=[pltpu.VMEM((SSD_GROUPS, SSD_STATE, GROUP_W), f32)],
        compiler_params=pltpu.CompilerParams(dimension_semantics=("arbitrary", "arbitrary"),
                                             vmem_limit_bytes=VMEM_LIMIT),
        name="ssd_fwd",
    )(xo, dtp, yb, d_skip_x)


def _na_bias_tables(rpb):
    variants = [(0, 0, 0), (2, 0, 0), (4, 0, 1), (6, 2, 2), (8, 2, 2)]
    kc = np.arange(GRID_W)[:, None]
    w = np.arange(GRID_W)[None, :]
    cs = np.clip(w - NA_COLS // 2, 0, GRID_W - NA_COLS)
    col_ok = (kc >= cs) & (kc < cs + NA_COLS)
    taps = np.arange(2 * NA_COLS - 1)[:, None, None]
    onehot = (((kc - w + NA_COLS - 1)[None] == taps) & col_ok[None]).astype(np.float32)
    base = jnp.einsum('hrk,kcw->hrcw', rpb.astype(f32) * LOG2E, onehot, precision=lax.Precision.HIGHEST)
    base = jnp.where(col_ok, base, NEG_BIG).reshape(HEAD_PAIRS, 2, 2 * NA_ROWS - 1, GRID_W, GRID_W)
    basep = jnp.pad(base, ((0, 0), (0, 0), (2, 2), (0, 0), (0, 0)), constant_values=NEG_BIG)
    n_blk = 2 * NA_ROWS + 2
    blk = jnp.concatenate([basep[:, 0, 1:1 + n_blk], basep[:, 0, 0:n_blk],
                           basep[:, 1, 1:1 + n_blk], basep[:, 1, 0:n_blk]], axis=-1)
    kr = np.arange(NA_WIN)[:, None] // GRID_W
    par = (np.arange(2 * LANES)[None, :] // GRID_W) % 2
    tabs = []
    for roff, st0, st1 in variants:
        st = np.where(par == 0, st0, st1)
        valid = (kr >= st) & (kr < st + NA_ROWS)
        j0 = NA_ROWS - roff
        rows = blk[:, j0:j0 + NA_WIN_ROWS].reshape(HEAD_PAIRS, NA_WIN, 2 * LANES)
        tabs.append(jnp.where(valid, rows, NEG_BIG))
    return jnp.stack(tabs)


def _na_kernel(q_ref, kp_ref, kc_ref, kn_ref, vp_ref, vc_ref, vn_ref, tab_ref, o_ref, kwin_ref, vwin_ref,
               *, grid_rows, pairs):
    step = pl.program_id(2)
    qt = pairs * 2 * GRID_W
    me = _even_lane_mask()

    kwin_ref[0:NA_HALO, :] = kp_ref[0, 0]
    kwin_ref[NA_HALO:NA_HALO + qt, :] = kc_ref[0, 0]
    kwin_ref[NA_HALO + qt:, :] = kn_ref[0, 0]
    ht = NA_HALO // LANES
    vwin_ref[0:ht] = vp_ref[0, 0]
    vwin_ref[ht:ht + qt // LANES] = vc_ref[0, 0]
    vwin_ref[ht + qt // LANES:] = vn_ref[0, 0]
    tok0 = step * qt - NA_HALO

    def window(i):
        r = 2 * (step * pairs + i)
        u0 = jnp.clip(r - NA_ROWS // 2, 0, grid_rows - NA_WIN_ROWS)
        var = jnp.where(r < 4, r // 2, jnp.where(r >= grid_rows - 4, 3 + (r - (grid_rows - 4)) // 2, 2))
        return u0, var

    def run(edge_pairs):
        def keys(i):
            rows = NA_WIN_ROWS if i >= pairs - edge_pairs else NA_WIN_ROWS - 1
            return rows * GRID_W

        def scores(i):
            u0, var = window(i)
            half = keys(i) // 2
            q2 = q_ref[0, 0, i * LANES:(i + 1) * LANES, :]
            zero = jnp.zeros_like(q2)
            qm = jnp.concatenate([jnp.where(me, q2, zero), jnp.where(me, zero, q2)], axis=0)
            parts = []
            for hk in range(2):
                start = pl.multiple_of(u0 * GRID_W - tok0 + hk * half, BF16_ROWS)
                kk = kwin_ref[pl.ds(start, half), :]
                parts.append(_dot_nt(kk, qm) + tab_ref[var, 0, hk * half:(hk + 1) * half, :])
            return jnp.concatenate(parts, axis=0)

        def finish(i, parts, rl):
            oc = jnp.concatenate([parts[e] * rl[:, e * LANES:(e + 1) * LANES] for e in range(2)], axis=0)
            o_ref[0, 0, i * LANES:(i + 1) * LANES, :] = oc.T.astype(bf16)

        s = scores(0)
        pending = None
        for i in range(pairs):
            s_next = scores(i + 1) if i + 1 < pairs else None
            u0, _ = window(i)
            m = jnp.max(s, axis=0, keepdims=True)
            p = jnp.exp2(s - m)
            l = jnp.sum(p, axis=0, keepdims=True)
            j0 = lax.shift_right_logical(u0 * GRID_W - tok0, 7)
            vtw = jnp.concatenate([vwin_ref[j0 + j] for j in range(NA_WIN // LANES)], axis=1)
            pb = p.astype(bf16)
            if keys(i) < NA_WIN:
                pb = jnp.concatenate([pb, jnp.zeros((NA_WIN - keys(i), 2 * LANES), bf16)], axis=0)
            parts = [_dot(vtw[e * HEAD_DIM:(e + 1) * HEAD_DIM, :], pb[:, e * LANES:(e + 1) * LANES])
                     for e in range(2)]
            if pending is not None:
                finish(*pending)
            pending = (i, parts, 1.0 / l)
            s = s_next
        finish(*pending)

    last = pl.num_programs(2) - 1
    pl.when(step != last)(functools.partial(run, 0))
    pl.when(step == last)(functools.partial(run, 2))


def _na(q, k, vt, tab):
    B, _, L, _ = q.shape
    grid_rows = L // GRID_W
    pairs = NA_PAIRS_SINGLE_STEP if L == NA_PAIRS_SINGLE_STEP * 2 * GRID_W else NA_PAIRS_PER_STEP
    QT, H = pairs * 2 * GRID_W, NA_HALO
    r = QT // H
    nh = L // H
    nv = tab.shape[0]
    cur = lambda p, b, s: (b, p, s, 0)
    prev = lambda p, b, s: (b, p, jnp.maximum(s * r - 1, 0), 0)
    nxt = lambda p, b, s: (b, p, jnp.minimum((s + 1) * r, nh - 1), 0)
    tile = lambda f: (lambda p, b, s: f(p, b, s) + (0,))
    ht = H // LANES
    return pl.pallas_call(
        functools.partial(_na_kernel, grid_rows=grid_rows, pairs=pairs),
        grid=(HEAD_PAIRS, B, L // QT),
        in_specs=[pl.BlockSpec((1, 1, QT, LANES), cur),
                  pl.BlockSpec((1, 1, H, LANES), prev),
                  pl.BlockSpec((1, 1, QT, LANES), cur),
                  pl.BlockSpec((1, 1, H, LANES), nxt),
                  pl.BlockSpec((1, 1, ht, LANES, LANES), tile(prev)),
                  pl.BlockSpec((1, 1, QT // LANES, LANES, LANES), tile(cur)),
                  pl.BlockSpec((1, 1, ht, LANES, LANES), tile(nxt)),
                  pl.BlockSpec((nv, 1, NA_WIN, 2 * LANES), lambda p, b, s: (0, p, 0, 0))],
        out_specs=pl.BlockSpec((1, 1, QT, LANES), cur),
        out_shape=jax.ShapeDtypeStruct((B, HEAD_PAIRS, L, LANES), bf16),
        scratch_shapes=[pltpu.VMEM((QT + 2 * H, LANES), bf16),
                        pltpu.VMEM((QT // LANES + 2 * ht, LANES, LANES), bf16)],
        compiler_params=pltpu.CompilerParams(dimension_semantics=("arbitrary", "arbitrary", "arbitrary"),
                                             vmem_limit_bytes=VMEM_LIMIT),
        name="na",
    )(q, k, k, k, vt, vt, vt, tab)


def _out_kernel(x_ref, ys_ref, z_ref, yn_ref, mod_ref, gs_ref, wo_ref, gf_ref, wg_ref, wu_ref, wd_ref, o_ref, a_ref):
    gated = ys_ref[0].astype(f32) * _silu(z_ref[0].astype(f32))
    gms = jnp.mean(gated * gated, axis=-1, keepdims=True)
    y_ssd = (gated * lax.rsqrt(gms + EPS) * gs_ref[...]).astype(bf16)
    y_na = jnp.concatenate([yn_ref[0, p] for p in range(HEAD_PAIRS)], axis=1)
    mix = _dot(y_ssd, wo_ref[0:SSD_INNER, :]) + _dot(y_na, wo_ref[SSD_INNER:, :])
    x1 = x_ref[0] + mod_ref[0, 2:3, :] * mix
    ms = jnp.mean(x1 * x1, axis=-1, keepdims=True)
    h2 = x1 * lax.rsqrt(ms + EPS) * gf_ref[...]
    h2 = (h2 * (1.0 + mod_ref[0, 4:5, :]) + mod_ref[0, 3:4, :]).astype(bf16)
    for c in range(D_FF // FF_BLOCK):
        cols = slice(c * FF_BLOCK, (c + 1) * FF_BLOCK)
        gate = _dot(h2, wg_ref[:, cols])
        up = _dot(h2, wu_ref[:, cols])
        a_ref[:, cols] = (_silu(gate) * up).astype(bf16)
    ff = _dot(a_ref[...], wd_ref[...])
    o_ref[0] = x1 + mod_ref[0, 5:6, :] * ff


def _out_ffn(x, y_pre, z, y_na, mod, g_ssd, w_out, g_ffn, w_gate, w_up, w_down):
    B, L, _ = x.shape
    T = TOKEN_TILE
    tok = lambda b, i: (b, i, 0)
    const = lambda b, i: (0, 0)
    single = pl.Buffered(1)
    return pl.pallas_call(
        _out_kernel,
        grid=(B, L // T),
        in_specs=[pl.BlockSpec((1, T, D_MODEL), tok),
                  pl.BlockSpec((1, T, SSD_INNER), tok),
                  pl.BlockSpec((1, T, SSD_INNER), tok),
                  pl.BlockSpec((1, HEAD_PAIRS, T, LANES), lambda b, i: (b, 0, i, 0)),
                  pl.BlockSpec((1, 6, D_MODEL), lambda b, i: (b, 0, 0)),
                  pl.BlockSpec((1, SSD_INNER), const),
                  pl.BlockSpec((D_MODEL, D_MODEL), const, pipeline_mode=single),
                  pl.BlockSpec((1, D_MODEL), const),
                  pl.BlockSpec((D_MODEL, D_FF), const, pipeline_mode=single),
                  pl.BlockSpec((D_MODEL, D_FF), const, pipeline_mode=single),
                  pl.BlockSpec((D_FF, D_MODEL), const, pipeline_mode=single)],
        out_specs=pl.BlockSpec((1, T, D_MODEL), tok),
        out_shape=jax.ShapeDtypeStruct((B, L, D_MODEL), f32),
        scratch_shapes=[pltpu.VMEM((T, D_FF), bf16)],
        compiler_params=pltpu.CompilerParams(dimension_semantics=("arbitrary", "arbitrary"),
                                             vmem_limit_bytes=VMEM_LIMIT),
        name="out_ffn",
    )(x, y_pre, z, y_na, mod, g_ssd, w_out, g_ffn, w_gate, w_up, w_down)


def _pad_lanes(parts):
    pieces, pos = [], 0
    for off, v in parts:
        pieces += [jnp.zeros((off - pos,), f32), v.astype(f32)]
        pos = off + v.shape[0]
    pieces.append(jnp.zeros((LANES - pos,), f32))
    return jnp.concatenate(pieces).reshape(1, LANES)


def kernel(x_prompt, x_sample, c_prompt, c_sample, w_ada, b_ada, g_mix, w_in, conv_w, conv_b, dt_bias_fwd, dt_bias_bwd, a_log_fwd, a_log_bwd, d_skip, g_ssd, g_q, g_k, rpb, w_out, g_ffn, w_gate, w_up, w_down):
    l = 0
    nb_p = x_prompt.shape[0]
    nb_s = x_sample.shape[0]

    c_all = jnp.concatenate([c_prompt, c_sample, jnp.zeros((8 - nb_p - nb_s, D_MODEL), f32)], axis=0)
    mod = _modulation(c_all.T, w_ada[l], b_ada[l], nb_p + nb_s).reshape(8, 6, D_MODEL)

    o1 = SSD_INNER
    o2 = o1 + CONV_CH
    o3 = o2 + 2 * SSD_HEADS
    o4 = o3 + NA_INNER
    o5 = o4 + NA_INNER
    wi = w_in[l]
    w_main = jnp.concatenate([wi[:, :o2], wi[:, o3:o5]], axis=1).astype(bf16)
    w_vt = jnp.concatenate([wi[:, o5:], wi[:, o2:o3], wi[:, o2:o3]], axis=1).T.astype(bf16)
    dtb = _pad_lanes([(0, dt_bias_fwd[l]), (8, dt_bias_bwd[l]), (64, dt_bias_fwd[l]), (72, dt_bias_bwd[l])])
    alog = _pad_lanes([(0, a_log_fwd[l]), (8, a_log_bwd[l])])
    gq = jnp.tile(g_q[l].astype(f32) * (HEAD_DIM ** -0.5 * LOG2E), NA_HEADS).reshape(1, NA_INNER)
    gk = jnp.tile(g_k[l].astype(f32), NA_HEADS).reshape(1, NA_INNER)
    dsk = jnp.repeat(d_skip[l].astype(f32), HEAD_DIM).reshape(1, SSD_INNER)
    tab = _na_bias_tables(rpb[l])
    wo = w_out[l].astype(bf16)
    wg = w_gate[l].astype(bf16)
    wu = w_up[l].astype(bf16)
    wd = w_down[l].astype(bf16)
    row = lambda v: v.astype(f32).reshape(1, -1)

    def trunk(x, m):
        z, xbc, dtp, q, k, vt = _in_proj(x, m, row(g_mix[l]), w_main, w_vt, gq, gk, dtb, alog)
        xo, yb = _ssd_bwd(xbc, dtp, conv_w[l].astype(f32), row(conv_b[l]))
        y_pre = _ssd_fwd(xo, dtp, yb, dsk)
        y_na = _na(q, k, vt, tab)
        return _out_ffn(x, y_pre, z, y_na, m, row(g_ssd[l]), wo, row(g_ffn[l]), wg, wu, wd)

    return (trunk(x_prompt, mod[:nb_p]), trunk(x_sample, mod[nb_p:nb_p + nb_s]))
```
